```python
import math
import jax, jax.numpy as jnp
from jax import lax
import numpy as np

D_MODEL = 2048
BATCH = 2
SEQ = 4096
DEPTH = 4
DEC_BATCH = 8
DEC_SEQ = 4
PAST_LEN = 16384
PAGE_SIZE = 128

N_MIXERS = 3
N_A = (DEPTH + 2) // 3
N_B = (DEPTH + 1) // 3
N_C = DEPTH // 3

A_HEADS = 16
A_DK = 128
A_DV = D_MODEL // A_HEADS
A_WIDTH = A_HEADS * A_DK
A_CHUNK = 64
B_CHUNK = 128
B_WIDTH = D_MODEL
B_HEADS = 16
B_HD = B_WIDTH // B_HEADS
C_HEADS = 8
C_HD = 128
C_GROUPS = ((128, 1), (512, 4), (2048, 16))
C_QBLOCK = 128
ROPE_THETA = 10000.0
MOE_GROUPS = 4
MOE_EXPERTS = 4
MOE_TOPK = 2
MOE_FF = 256

LN_EPS = 1e-5
RMS_EPS = 1e-6
DEEPNORM_ALPHA = (2 * DEPTH) ** 0.25
DEEPNORM_BETA = (8 * DEPTH) ** -0.25

kernel_name = "hgrn2_gmlp_dilated_hmoe_decoder_step"

F32 = jnp.float32


def layer_norm(x, g, b):
    xf = x.astype(F32)
    mu = jnp.mean(xf, -1, keepdims=True)
    var = jnp.mean(jnp.square(xf - mu), -1, keepdims=True)
    return ((xf - mu) * lax.rsqrt(var + LN_EPS) * g + b).astype(x.dtype)


def rope(x, pos):
    half = x.shape[-1] // 2
    inv = ROPE_THETA ** (-jnp.arange(half, dtype=F32) / half)
    ang = pos.astype(F32)[:, None] * inv[None, :]
    cos = jnp.cos(ang)[None, :, None, :]
    sin = jnp.sin(ang)[None, :, None, :]
    xf = x.astype(F32)
    x1, x2 = xf[..., :half], xf[..., half:]
    return jnp.concatenate([x1 * cos - x2 * sin, x1 * sin + x2 * cos], -1).astype(x.dtype)


def hgrn2_scan(q, k, log_f, v, s0):
    B, T, H, DK = q.shape
    DV = v.shape[-1]
    C = math.gcd(T, A_CHUNK)
    n = T // C

    def to_chunks(a):
        return a.reshape(B, n, C, H, a.shape[-1]).transpose(1, 0, 3, 2, 4)

    qc, kc, gc, vc = to_chunks(q), to_chunks(k), to_chunks(log_f), to_chunks(v)
    causal = jnp.tril(jnp.ones((C, C), bool))[:, :, None]

    def step(S, inp):
        qi, ki, gi, vi = inp
        b = jnp.cumsum(gi, axis=2)
        inter = jnp.einsum('bhck,bhkv->bhcv', qi * jnp.exp(b), S)
        diff = b[:, :, :, None, :] - b[:, :, None, :, :]
        decay = jnp.where(causal, jnp.exp(jnp.where(causal, diff, 0.0)), 0.0)
        att = jnp.einsum('bhtsk,bhsk->bhts', qi[:, :, :, None, :] * decay, ki)
        o = inter + jnp.einsum('bhts,bhsv->bhtv', att, vi)
        b_last = b[:, :, -1:, :]
        S_new = jnp.exp(b_last[:, :, 0, :])[..., None] * S + jnp.einsum(
            'bhsk,bhsv->bhkv', ki * jnp.exp(b_last - b), vi)
        return S_new, o

    S, o = lax.scan(step, s0, (qc, kc, gc, vc))
    o = o.transpose(1, 0, 3, 2, 4).reshape(B, T, H, DV)
    return o, S


def hgrn2_mixer(x, s0, w_in, lb, norm_g, w_out):
    B, T, _ = x.shape
    proj = x @ w_in
    q, f, i, g = jnp.split(proj, [A_WIDTH, 2 * A_WIDTH, 2 * A_WIDTH + A_HEADS * A_DV], axis=-1)
    shp = (B, T, A_HEADS, A_DK)
    q = jax.nn.silu(q.astype(F32)).reshape(shp)
    f = f.astype(F32)
    log_f = jnp.logaddexp(jnp.log(lb), jnp.log1p(-lb) + jax.nn.log_sigmoid(f)).reshape(shp)
    k = ((1.0 - lb) * jax.nn.sigmoid(-f)).reshape(shp)
    v = i.astype(F32).reshape(B, T, A_HEADS, A_DV)
    o, s_new = hgrn2_scan(q, k, log_f, v, s0.astype(F32))
    o = o * lax.rsqrt(jnp.mean(jnp.square(o), -1, keepdims=True) + RMS_EPS) * norm_g
    o = o.reshape(B, T, A_HEADS * A_DV) * jax.nn.silu(g.astype(F32))
    return o.astype(x.dtype) @ w_out, s_new


def chunk_mlp_mixer(x, w_in, b_in, ln_g, ln_b, w_s, b_s, w_out):
    B, T, _ = x.shape
    z = jax.nn.gelu((x @ w_in + b_in).astype(F32))
    u, v = jnp.split(z, 2, axis=-1)
    v = layer_norm(v, ln_g, ln_b)
    Tp = -(-T // B_CHUNK) * B_CHUNK
    vp = jnp.pad(v, ((0, 0), (0, Tp - T), (0, 0))).reshape(B, Tp // B_CHUNK, B_CHUNK, B_HEADS, B_HD)
    w_causal = w_s * jnp.tril(jnp.ones((B_CHUNK, B_CHUNK), w_s.dtype))
    mixed = jnp.einsum('hts,bnshd->bnthd', w_causal.astype(F32), vp) + b_s.T.astype(F32)[None, None, :, :, None]
    mixed = mixed.reshape(B, Tp, B_WIDTH)[:, :T]
    out = (u * mixed).astype(x.dtype) @ w_out
    return out, v.astype(x.dtype)


def dilated_group(q, kv_new, kv_past, pos0, window, dil):
    B, T, H, hd = q.shape
    P = kv_past.shape[1]
    kv_all = jnp.concatenate([jnp.zeros((B, window - P, 2, H, hd), kv_new.dtype),
                              kv_past.astype(kv_new.dtype), kv_new], axis=1)
    n_keys = window // dil + 1
    offs = jnp.arange(n_keys) * dil
    bq = math.gcd(T, C_QBLOCK)
    scale = hd ** -0.5

    def block(bi):
        t = bi * bq + jnp.arange(bq)
        rel = t[:, None] - offs[None, :]
        kvg = jnp.take(kv_all, window + rel, axis=1).astype(F32)
        valid = (pos0 + rel) >= 0
        qb = lax.dynamic_slice_in_dim(q, bi * bq, bq, axis=1).astype(F32)
        s = jnp.einsum('bqhd,bqnhd->bhqn', qb, kvg[:, :, :, 0]) * scale
        s = jnp.where(valid[None, None], s, -jnp.inf)
        m = jnp.max(s, -1, keepdims=True)
        p = jnp.exp(s - m)
        den = jnp.sum(p, -1)
        o = jnp.einsum('bhqn,bqnhd->bqhd', p, kvg[:, :, :, 1]) / den.transpose(0, 2, 1)[..., None]
        lse = m[..., 0] + jnp.log(den)
        return o, lse

    o, lse = lax.map(block, jnp.arange(T // bq))
    o = o.transpose(1, 0, 2, 3, 4).reshape(B, T, H, hd)
    lse = lse.transpose(1, 2, 0, 3).reshape(B, H, T)
    return o, lse


def dilated_mixer(x, kv_past, pos0, w_in, w_out):
    B, T, _ = x.shape
    qkv = (x @ w_in).reshape(B, T, 3, len(C_GROUPS), C_HEADS, C_HD)
    pos = pos0 + jnp.arange(T, dtype=jnp.int32)
    outs, lses, kv_news = [], [], []
    for g, (window, dil) in enumerate(C_GROUPS):
        q = rope(qkv[:, :, 0, g], pos)
        k = rope(qkv[:, :, 1, g], pos)
        kv_new = jnp.stack([k, qkv[:, :, 2, g]], axis=2)
        o, lse = dilated_group(q, kv_new, kv_past[g], pos0, window, dil)
        outs.append(o)
        lses.append(lse)
        kv_news.append(kv_new)
    w = jax.nn.softmax(jnp.stack(lses, 0), axis=0)
    o = jnp.einsum('gbht,gbthd->bthd', w, jnp.stack(outs, 0))
    return o.reshape(B, T, C_HEADS * C_HD).astype(x.dtype) @ w_out, kv_news


def hier_moe(x, w_group, w_expert, w1, w3, w2):
    B, T, D = x.shape
    xt = x.reshape(B * T, D)
    g_prob = jax.nn.softmax((xt @ w_group).astype(F32), -1)
    g_top, g_idx = lax.top_k(g_prob, 1)
    e_logits = (xt @ w_expert).astype(F32).reshape(-1, MOE_GROUPS, MOE_EXPERTS)
    e_in = jnp.take_along_axis(e_logits, g_idx[:, :, None], axis=1)[:, 0]
    e_top, e_idx = lax.top_k(e_in, MOE_TOPK)
    e_w = jax.nn.softmax(e_top, -1) * g_top
    flat = g_idx * MOE_EXPERTS + e_idx
    gate = jnp.sum(jax.nn.one_hot(flat, MOE_GROUPS * MOE_EXPERTS, dtype=F32) * e_w[..., None], axis=1)
    h = jax.nn.silu(jnp.einsum('nd,edf->nef', xt, w1)) * jnp.einsum('nd,edf->nef', xt, w3)
    y = jnp.einsum('nef,efd->nd', h * gate.astype(x.dtype)[..., None], w2)
    return y.reshape(B, T, D)


def setup_inputs(seed: int = 0) -> dict:
    key = jax.random.key(seed)
    ks = list(jax.random.split(key, 32))

    def nrm(shape, scale):
        return jax.random.normal(ks.pop(), shape, F32) * scale

    D = D_MODEL
    GE = MOE_GROUPS * MOE_EXPERTS
    lw = [min(w, PAST_LEN) for w, _ in C_GROUPS]
    return {
        "x_prompt": nrm((BATCH, SEQ, D), 1.0),
        "x_sample": nrm((DEC_BATCH, DEC_SEQ, D), 1.0),
        "state_hgrn": nrm((N_A, DEC_BATCH, A_HEADS, A_DK, A_DV), 0.5),
        "cache_c_kv_w128": nrm((N_C, DEC_BATCH, lw[0], 2, C_HEADS, C_HD), 1.0),
        "cache_c_kv_w512": nrm((N_C, DEC_BATCH, lw[1], 2, C_HEADS, C_HD), 1.0),
        "cache_c_kv_w2048": nrm((N_C, DEC_BATCH, lw[2], 2, C_HEADS, C_HD), 1.0),
        "ln_g": 1.0 + nrm((DEPTH, 2, D), 0.02),
        "ln_b": nrm((DEPTH, 2, D), 0.02),
        "a_w_in": nrm((N_A, D, 2 * A_WIDTH + 2 * A_HEADS * A_DV), D ** -0.5),
        "a_lb_logits": nrm((N_A, A_WIDTH), 1.0),
        "a_norm_g": 1.0 + nrm((N_A, A_DV), 0.02),
        "a_w_out": nrm((N_A, A_HEADS * A_DV, D), (A_HEADS * A_DV) ** -0.5 * DEEPNORM_BETA),
        "b_w_in": nrm((N_B, D, 2 * B_WIDTH), D ** -0.5),
        "b_b_in": nrm((N_B, 2 * B_WIDTH), 0.02),
        "b_ln_g": 1.0 + nrm((N_B, B_WIDTH), 0.02),
        "b_ln_b": nrm((N_B, B_WIDTH), 0.02),
        "b_w_s": nrm((N_B, B_HEADS, B_CHUNK, B_CHUNK), B_CHUNK ** -0.5),
        "b_b_s": 1.0 + nrm((N_B, B_HEADS, B_CHUNK), 0.02),
        "b_w_out": nrm((N_B, B_WIDTH, D), B_WIDTH ** -0.5 * DEEPNORM_BETA),
        "c_w_in": nrm((N_C, D, 3 * len(C_GROUPS) * C_HEADS * C_HD), D ** -0.5),
        "c_w_out": nrm((N_C, C_HEADS * C_HD, D), (C_HEADS * C_HD) ** -0.5 * DEEPNORM_BETA),
        "moe_w_group": nrm((DEPTH, D, MOE_GROUPS), D ** -0.5),
        "moe_w_expert": nrm((DEPTH, D, GE), D ** -0.5),
        "moe_w1": nrm((DEPTH, GE, D, MOE_FF), D ** -0.5),
        "moe_w3": nrm((DEPTH, GE, D, MOE_FF), D ** -0.5),
        "moe_w2": nrm((DEPTH, GE, MOE_FF, D), MOE_FF ** -0.5 * DEEPNORM_BETA),
    }


def reference(x_prompt, x_sample, state_hgrn, cache_c_kv_w128, cache_c_kv_w512, cache_c_kv_w2048,
              ln_g, ln_b, a_w_in, a_lb_logits, a_norm_g, a_w_out,
              b_w_in, b_b_in, b_ln_g, b_ln_b, b_w_s, b_b_s, b_w_out,
              c_w_in, c_w_out, moe_w_group, moe_w_expert, moe_w1, moe_w3, moe_w2):
    lb_p = jax.nn.softmax(a_lb_logits.astype(F32), axis=0)
    lb_all = jnp.clip(jnp.cumsum(lb_p, axis=0) - lb_p[0:1], 0.0, 1.0 - 1e-6)
    c_caches = (cache_c_kv_w128, cache_c_kv_w512, cache_c_kv_w2048)

    xp, xs = x_prompt, x_sample
    bp, tp = xp.shape[0], xp.shape[1]
    hgrn_p, hgrn_s, chunk_v_s = [], [], []
    kv_p = [[] for _ in C_GROUPS]
    kv_s = [[] for _ in C_GROUPS]

    for i in range(DEPTH):
        kind, j = i % N_MIXERS, i // N_MIXERS
        if kind == 0:
            s0p = jnp.zeros((bp, A_HEADS, A_DK, A_DV), F32)
            mp, sp = hgrn2_mixer(xp, s0p, a_w_in[j], lb_all[j], a_norm_g[j], a_w_out[j])
            ms, ss = hgrn2_mixer(xs, state_hgrn[j], a_w_in[j], lb_all[j], a_norm_g[j], a_w_out[j])
            hgrn_p.append(sp.astype(state_hgrn.dtype))
            hgrn_s.append(ss.astype(state_hgrn.dtype))
        elif kind == 1:
            mp, _ = chunk_mlp_mixer(xp, b_w_in[j], b_b_in[j], b_ln_g[j], b_ln_b[j], b_w_s[j], b_b_s[j], b_w_out[j])
            ms, vs = chunk_mlp_mixer(xs, b_w_in[j], b_b_in[j], b_ln_g[j], b_ln_b[j], b_w_s[j], b_b_s[j], b_w_out[j])
            chunk_v_s.append(vs)
        else:
            empty = tuple(jnp.zeros((bp, 0, 2, C_HEADS, C_HD), xp.dtype) for _ in C_GROUPS)
            mp, kvp = dilated_mixer(xp, empty, 0, c_w_in[j], c_w_out[j])
            ms, kvs = dilated_mixer(xs, tuple(c[j] for c in c_caches), PAST_LEN, c_w_in[j], c_w_out[j])
            for g, (window, _) in enumerate(C_GROUPS):
                kv_p[g].append(kvp[g][:, tp - min(window, tp):])
                kv_s[g].append(kvs[g])
        xp = layer_norm(DEEPNORM_ALPHA * xp + mp, ln_g[i, 0], ln_b[i, 0])
        xs = layer_norm(DEEPNORM_ALPHA * xs + ms, ln_g[i, 0], ln_b[i, 0])
        fp = hier_moe(xp, moe_w_group[i], moe_w_expert[i], moe_w1[i], moe_w3[i], moe_w2[i])
        fs = hier_moe(xs, moe_w_group[i], moe_w_expert[i], moe_w1[i], moe_w3[i], moe_w2[i])
        xp = layer_norm(DEEPNORM_ALPHA * xp + fp, ln_g[i, 1], ln_b[i, 1])
        xs = layer_norm(DEEPNORM_ALPHA * xs + fs, ln_g[i, 1], ln_b[i, 1])

    return (xp, xs, jnp.stack(hgrn_p), jnp.stack(hgrn_s), jnp.stack(chunk_v_s),
            jnp.stack(kv_p[0]), jnp.stack(kv_s[0]), jnp.stack(kv_p[1]), jnp.stack(kv_s[1]),
            jnp.stack(kv_p[2]), jnp.stack(kv_s[2]))
```

```python
import functools

import jax
import jax.numpy as jnp
from jax import lax
from jax.experimental import pallas as pl
from jax.experimental.pallas import tpu as pltpu

F32 = jnp.float32
BF16 = jnp.bfloat16
HIGHEST = lax.Precision.HIGHEST

D_MODEL = 2048
DEPTH = 4
PAST_LEN = 16384
A_HEADS = 16
A_DK = 128
A_DV = 128
A_WIDTH = A_HEADS * A_DK
HGRN_CHUNK = 64
HGRN_SUB = 16
B_CHUNK = 128
B_HEADS = 16
B_HD = 128
C_HEADS = 8
C_HD = 128
C_GROUPS = ((128, 1), (512, 4), (2048, 16))
C_KEYS = 129
C_QKV = 3 * len(C_GROUPS) * C_HEADS * C_HD
ROPE_THETA = 10000.0
MOE_GROUPS = 4
MOE_EXPERTS = 4
MOE_GE = MOE_GROUPS * MOE_EXPERTS
MOE_FF = 256
LN_EPS = 1e-5
RMS_EPS = 1e-6
ALPHA = (2 * DEPTH) ** 0.25
LANES = 128
VMEM_LIMIT = 56 * 1024 * 1024


def _cparams(*sem):
    return pltpu.CompilerParams(dimension_semantics=sem, vmem_limit_bytes=VMEM_LIMIT)


def _sigmoid(x):
    return 1.0 / (1.0 + jnp.exp(-x))


def _ln_rows(y, g, b):
    mu = jnp.mean(y, axis=-1, keepdims=True)
    d = y - mu
    var = jnp.mean(d * d, axis=-1, keepdims=True)
    return d * lax.rsqrt(var + LN_EPS) * g + b


def _nt_dot(a, b):
    return lax.dot_general(a, b, (((1,), (1,)), ((), ())), preferred_element_type=F32)


def _tn_dot(a, b):
    return lax.dot_general(a, b, (((0,), (0,)), ((), ())), preferred_element_type=F32)


def _proj_body(x_ref, w_ref, *rest, epilogue, n_extra):
    extras = rest[:n_extra]
    o_ref = rest[n_extra]
    xb_ref = rest[n_extra + 1]
    j = pl.program_id(1)

    @pl.when(j == 0)
    def _cast():
        xb_ref[...] = x_ref[...].astype(BF16)

    acc = jnp.dot(xb_ref[...], w_ref[...], preferred_element_type=F32)
    epilogue(acc, j, extras, o_ref)


def _proj(x, w, extras, extra_specs, epilogue, bm, bn):
    m, k = x.shape
    n = w.shape[1]
    return pl.pallas_call(
        functools.partial(_proj_body, epilogue=epilogue, n_extra=len(extras)),
        grid=(m // bm, n // bn),
        in_specs=[pl.BlockSpec((bm, k), lambda i, j: (i, 0)),
                  pl.BlockSpec((k, bn), lambda i, j: (0, j))] + list(extra_specs),
        out_specs=pl.BlockSpec((bm, bn), lambda i, j: (i, j)),
        out_shape=jax.ShapeDtypeStruct((m, n), F32),
        scratch_shapes=[pltpu.VMEM((bm, k), BF16)],
        compiler_params=_cparams("parallel", "arbitrary"),
    )(x, w, *extras)


def _hgrn_proj_epilogue(acc, j, extras, o_ref, *, bn):
    nq = A_WIDTH // bn
    is_silu = jnp.logical_or(j < nq, j >= 3 * nq)

    @pl.when(is_silu)
    def _():
        o_ref[...] = acc * _sigmoid(acc)

    @pl.when(jnp.logical_not(is_silu))
    def _():
        o_ref[...] = acc


def _gelu_tanh(z):
    return 0.5 * z * (1.0 + jnp.tanh(0.7978845608028654 * (z + 0.044715 * (z * z * z))))


def _gmlp_proj_epilogue(acc, j, extras, o_ref):
    bias_ref, g_ref, b_ref = extras
    z = _gelu_tanh(acc + bias_ref[...])

    @pl.when(j == 0)
    def _():
        o_ref[...] = z

    @pl.when(j == 1)
    def _():
        o_ref[...] = _ln_rows(z, g_ref[...], b_ref[...])


def _attn_proj_epilogue(acc, j, extras, o_ref, *, bn):
    cos_ref, sin_ref = extras
    n_rot = 2 * len(C_GROUPS) * C_HEADS * C_HD // bn

    @pl.when(j < n_rot)
    def _():
        cos = cos_ref[...]
        sin = sin_ref[...]
        for h in range(bn // C_HD):
            xh = acc[:, h * C_HD:(h + 1) * C_HD]
            o_ref[:, h * C_HD:(h + 1) * C_HD] = xh * cos + pltpu.roll(xh, C_HD // 2, 1) * sin

    @pl.when(j >= n_rot)
    def _():
        o_ref[...] = acc


def _out_ln_body(*refs, prologue, n_in):
    ins = refs[:n_in]
    w_ref, r_ref, g_ref, b_ref, o_ref = refs[n_in:n_in + 5]
    a = prologue(*ins)
    acc = jnp.dot(a, w_ref[...], preferred_element_type=F32)
    y = ALPHA * r_ref[...] + acc
    o_ref[...] = _ln_rows(y, g_ref[...], b_ref[...])


def _out_ln(ins, in_specs, prologue, w, resid, g, b, bm):
    m = resid.shape[0]
    k = w.shape[0]
    row = pl.BlockSpec((bm, D_MODEL), lambda i: (i, 0))
    vec = pl.BlockSpec((1, D_MODEL), lambda i: (0, 0))
    return pl.pallas_call(
        functools.partial(_out_ln_body, prologue=prologue, n_in=len(ins)),
        grid=(m // bm,),
        in_specs=list(in_specs) + [pl.BlockSpec((k, D_MODEL), lambda i: (0, 0)), row, vec, vec],
        out_specs=row,
        out_shape=jax.ShapeDtypeStruct((m, D_MODEL), F32),
        compiler_params=_cparams("parallel"),
    )(*ins, w, resid, g, b)


def _cast_prologue(a_ref):
    return a_ref[...].astype(BF16)


def _merge_prologue(o0, o1, o2, l0, l1, l2):
    a0, a1, a2 = l0[...], l1[...], l2[...]
    mx = jnp.maximum(jnp.maximum(a0, a1), a2)
    e0, e1, e2 = jnp.exp(a0 - mx), jnp.exp(a1 - mx), jnp.exp(a2 - mx)
    o = (e0 * o0[...] + e1 * o1[...] + e2 * o2[...]) / (e0 + e1 + e2)
    return o.astype(BF16)


def _moe_gate(logits):
    lane = lax.broadcasted_iota(jnp.int32, logits.shape, 1).astype(F32)
    neg = -jnp.inf
    big = 4.0 * LANES
    gl = jnp.where((lane >= MOE_GE) & (lane < MOE_GE + MOE_GROUPS), logits, neg)
    gmax = jnp.max(gl, axis=-1, keepdims=True)
    g_idx = jnp.min(jnp.where(gl == gmax, lane - MOE_GE, big), axis=-1, keepdims=True)
    g_top = 1.0 / jnp.sum(jnp.exp(gl - gmax), axis=-1, keepdims=True)
    lo = g_idx * MOE_EXPERTS
    el = jnp.where((lane >= lo) & (lane < lo + MOE_EXPERTS), logits, neg)
    m1 = jnp.max(el, axis=-1, keepdims=True)
    i1 = jnp.min(jnp.where(el == m1, lane, big), axis=-1, keepdims=True)
    el2 = jnp.where(lane == i1, neg, el)
    m2 = jnp.max(el2, axis=-1, keepdims=True)
    i2 = jnp.min(jnp.where(el2 == m2, lane, big), axis=-1, keepdims=True)
    r = jnp.exp(m2 - m1)
    w1 = g_top / (1.0 + r)
    w2 = w1 * r
    return jnp.where(lane == i1, w1, 0.0) + jnp.where(lane == i2, w2, 0.0)


def _moe_body(x_ref, wr_ref, w1_ref, w3_ref, w2_ref, g_ref, b_ref, o_ref, xb_ref, gate_ref, acc_ref):
    e = pl.program_id(1)

    @pl.when(e == 0)
    def _route():
        x = x_ref[...]
        xb_ref[...] = x.astype(BF16)
        logits = jnp.dot(x, wr_ref[...], precision=HIGHEST, preferred_element_type=F32)
        gate_ref[...] = _moe_gate(logits)
        acc_ref[...] = jnp.zeros_like(acc_ref)

    xb = xb_ref[...]
    h1 = jnp.dot(xb, w1_ref[0], preferred_element_type=F32)
    h3 = jnp.dot(xb, w3_ref[0], preferred_element_type=F32)
    gate = gate_ref[...]
    lane = lax.broadcasted_iota(jnp.int32, gate.shape, 1)
    ge = jnp.sum(jnp.where(lane == e, gate, 0.0), axis=-1, keepdims=True)
    hg = (h1 * _sigmoid(h1) * h3 * ge).astype(BF16)
    acc_ref[...] += jnp.dot(hg, w2_ref[0], preferred_element_type=F32)

    @pl.when(e == MOE_GE - 1)
    def _finish():
        y = ALPHA * x_ref[...] + acc_ref[...]
        o_ref[...] = _ln_rows(y, g_ref[...], b_ref[...])


def _moe(x, wr, w1, w3, w2, g, b, bm):
    m = x.shape[0]
    row = pl.BlockSpec((bm, D_MODEL), lambda i, e: (i, 0))
    vec = pl.BlockSpec((1, D_MODEL), lambda i, e: (0, 0))
    return pl.pallas_call(
        _moe_body,
        grid=(m // bm, MOE_GE),
        in_specs=[row,
                  pl.BlockSpec((D_MODEL, LANES), lambda i, e: (0, 0)),
                  pl.BlockSpec((1, D_MODEL, MOE_FF), lambda i, e: (e, 0, 0)),
                  pl.BlockSpec((1, D_MODEL, MOE_FF), lambda i, e: (e, 0, 0)),
                  pl.BlockSpec((1, MOE_FF, D_MODEL), lambda i, e: (e, 0, 0)),
                  vec, vec],
        out_specs=row,
        out_shape=jax.ShapeDtypeStruct((m, D_MODEL), F32),
        scratch_shapes=[pltpu.VMEM((bm, D_MODEL), BF16),
                        pltpu.VMEM((bm, LANES), F32),
                        pltpu.VMEM((bm, D_MODEL), F32)],
        compiler_params=_cparams("parallel", "arbitrary"),
    )(x, wr, w1, w3, w2, g, b)


def _hgrn_gates(f, lb):
    log_sig = jnp.minimum(f, 0.0) - jnp.log1p(jnp.exp(-jnp.abs(f)))
    a = jnp.log1p(-lb) + log_sig
    log_lb = jnp.log(lb)
    log_f = jnp.maximum(log_lb, a) + jnp.log1p(jnp.exp(-jnp.abs(log_lb - a)))
    k = (1.0 - lb) / (1.0 + jnp.exp(f))
    return log_f, k


def _hgrn_finish(o, gate, ng):
    o = o * lax.rsqrt(jnp.mean(o * o, axis=-1, keepdims=True) + RMS_EPS) * ng
    return o * gate


def _hgrn_scan_body(q_ref, f_ref, v_ref, g_ref, lb_ref, ng_ref, dm_ref, e_ref, o_ref, s_ref, st_ref,
                    *, n_chunks):
    C, c = HGRN_CHUNK, HGRN_SUB
    nsub = C // c
    tb = pl.program_id(2)

    @pl.when(tb == 0)
    def _init():
        st_ref[...] = jnp.zeros_like(st_ref)

    lb = lb_ref[...]
    ng = ng_ref[...]
    r2 = lax.broadcasted_iota(jnp.int32, (C, C), 0)
    c2 = lax.broadcasted_iota(jnp.int32, (C, C), 1)
    tril = (r2 >= c2).astype(F32)
    trow = lax.broadcasted_iota(jnp.int32, (C, 1), 0)
    srow = lax.broadcasted_iota(jnp.int32, (c, 1), 0)
    ones = jnp.ones((A_DK, LANES), BF16)

    def chunk(ci, carry):
        r0 = pl.multiple_of(ci * C, C)
        q = q_ref[pl.ds(r0, C), :]
        v = v_ref[pl.ds(r0, C), :].astype(BF16)
        log_f, k = _hgrn_gates(f_ref[pl.ds(r0, C), :], lb)
        b = jnp.dot(tril, log_f, precision=HIGHEST, preferred_element_type=F32)
        st = st_ref[...]
        inter = _nt_dot((q * jnp.exp(b)).astype(BF16), st.astype(BF16))

        bend = jnp.concatenate(
            [jnp.broadcast_to(b[c * i + c - 1:c * i + c, :], (c, A_DK)) for i in range(nsub)], axis=0)
        kn = (k * jnp.exp(bend - b)).astype(BF16)
        att = jnp.zeros((C, C), F32)
        for j in range(nsub - 1):
            e = c * (j + 1)
            rm = trow >= e
            qj = jnp.where(rm, q * jnp.exp(jnp.where(rm, b - b[e - 1:e, :], 0.0)), 0.0)
            aj = _nt_dot(qj.astype(BF16), kn)
            att = att + jnp.where((c2 >= c * j) & (c2 < e), aj, 0.0)

        slabs = []
        for t in range(C):
            i = t // c
            msk = srow <= (t - c * i)
            kb = k[c * i:c * i + c, :]
            bb = b[c * i:c * i + c, :]
            dec = jnp.exp(jnp.where(msk, b[t:t + 1, :] - bb, 0.0))
            slabs.append(jnp.where(msk, q[t:t + 1, :] * kb * dec, 0.0))
        x = jnp.concatenate(slabs, axis=0).astype(BF16)
        rsum = jnp.dot(x, ones, preferred_element_type=F32)
        placed = (rsum * dm_ref[...]).astype(BF16)
        att_d = jnp.dot(e_ref[...], placed, preferred_element_type=F32)
        att = att + att_d[:, :C]

        o = inter + jnp.dot(att.astype(BF16), v, preferred_element_type=F32)
        o_ref[pl.ds(r0, C), :] = _hgrn_finish(o, g_ref[pl.ds(r0, C), :], ng).astype(o_ref.dtype)

        bl = b[C - 1:C, :]
        kd = (k * jnp.exp(bl - b)).astype(BF16)
        st_ref[...] = st * jnp.exp(bl) + _tn_dot(v, kd)
        return carry

    lax.fori_loop(0, n_chunks, chunk, 0)

    @pl.when(tb == pl.num_programs(2) - 1)
    def _emit():
        s_ref[0, 0] = st_ref[...].T


def _hgrn_diag_constants():
    C, c = HGRN_CHUNK, HGRN_SUB
    rho = jnp.arange(C * c)
    t, s = rho // c, rho % c
    key = c * (t // c) + s
    dm = (key[:, None] == jnp.arange(LANES)[None, :]).astype(F32)
    em = (jnp.arange(C)[:, None] == t[None, :]).astype(BF16)
    return dm, em


def _hgrn_scan(p, lb, ng, batch, seq, tb):
    nt = seq // tb
    nh = A_HEADS
    dm, em = _hgrn_diag_constants()

    def col(off):
        return pl.BlockSpec((tb, A_DK), lambda b, h, t: (b * nt + t, off * nh + h))

    return pl.pallas_call(
        functools.partial(_hgrn_scan_body, n_chunks=tb // HGRN_CHUNK),
        grid=(batch, nh, nt),
        in_specs=[col(0), col(1), col(2), col(3),
                  pl.BlockSpec((1, A_DK), lambda b, h, t: (0, h)),
                  pl.BlockSpec((1, A_DV), lambda b, h, t: (0, 0)),
                  pl.BlockSpec(dm.shape, lambda b, h, t: (0, 0)),
                  pl.BlockSpec(em.shape, lambda b, h, t: (0, 0))],
        out_specs=[pl.BlockSpec((tb, A_DV), lambda b, h, t: (b * nt + t, h)),
                   pl.BlockSpec((1, 1, A_DK, A_DV), lambda b, h, t: (b, h, 0, 0))],
        out_shape=[jax.ShapeDtypeStruct((batch * seq, nh * A_DV), BF16),
                   jax.ShapeDtypeStruct((batch, nh, A_DK, A_DV), F32)],
        scratch_shapes=[pltpu.VMEM((A_DV, A_DK), F32)],
        compiler_params=_cparams("parallel", "parallel", "arbitrary"),
    )(p, p, p, p, lb, ng, dm, em)


def _hgrn_step_body(p_ref, s0_ref, lb_ref, ng_ref, o_ref, s_ref, *, n_tok):
    R = p_ref.shape[0]
    row = lax.broadcasted_iota(jnp.int32, (R, 1), 0)
    valid = row < n_tok
    r2 = lax.broadcasted_iota(jnp.int32, (R, R), 0)
    c2 = lax.broadcasted_iota(jnp.int32, (R, R), 1)
    tril = (r2 >= c2).astype(F32)
    ng = ng_ref[...]
    for h in range(A_HEADS):
        sl = slice(h * A_DK, (h + 1) * A_DK)
        q = p_ref[:, sl]
        v = p_ref[:, 2 * A_WIDTH + h * A_DV:2 * A_WIDTH + (h + 1) * A_DV]
        gate = p_ref[:, 3 * A_WIDTH + h * A_DV:3 * A_WIDTH + (h + 1) * A_DV]
        log_f, k = _hgrn_gates(p_ref[:, A_WIDTH + h * A_DK:A_WIDTH + (h + 1) * A_DK], lb_ref[:, sl])
        b = jnp.dot(tril, log_f, precision=HIGHEST, preferred_element_type=F32)
        st = s0_ref[0, h].T
        o = _nt_dot((q * jnp.exp(b)).astype(BF16), st.astype(BF16))
        for s in range(n_tok):
            m = row >= s
            w = jnp.where(m, q * k[s:s + 1, :] * jnp.exp(jnp.where(m, b - b[s:s + 1, :], 0.0)), 0.0)
            o = o + jnp.sum(w, axis=-1, keepdims=True) * v[s:s + 1, :]
        o_ref[:, h * A_DV:(h + 1) * A_DV] = _hgrn_finish(o, gate, ng)
        bl = b[n_tok - 1:n_tok, :]
        kd = jnp.where(valid, k * jnp.exp(jnp.where(valid, bl - b, 0.0)), 0.0)
        st_new = st * jnp.exp(bl) + _tn_dot(v.astype(BF16), kd.astype(BF16))
        s_ref[0, h] = st_new.T


def _hgrn_step(p, s0, lb, ng, batch, n_tok):
    rows = p.shape[0] // batch
    return pl.pallas_call(
        functools.partial(_hgrn_step_body, n_tok=n_tok),
        grid=(batch,),
        in_specs=[pl.BlockSpec((rows, 4 * A_WIDTH), lambda b: (b, 0)),
                  pl.BlockSpec((1, A_HEADS, A_DK, A_DV), lambda b: (b, 0, 0, 0)),
                  pl.BlockSpec((1, A_WIDTH), lambda b: (0, 0)),
                  pl.BlockSpec((1, A_DV), lambda b: (0, 0))],
        out_specs=[pl.BlockSpec((rows, A_HEADS * A_DV), lambda b: (b, 0)),
                   pl.BlockSpec((1, A_HEADS, A_DK, A_DV), lambda b: (b, 0, 0, 0))],
        out_shape=[jax.ShapeDtypeStruct((batch * rows, A_HEADS * A_DV), F32),
                   jax.ShapeDtypeStruct((batch, A_HEADS, A_DK, A_DV), F32)],
        compiler_params=_cparams("parallel"),
    )(p, s0, lb, ng)


def _gmlp_prologue(u_ref, v_ref, ws_ref, bs_ref, gated_ref, *, chunk):
    bm = u_ref.shape[0]
    r2 = lax.broadcasted_iota(jnp.int32, (chunk, chunk), 0)
    c2 = lax.broadcasted_iota(jnp.int32, (chunk, chunk), 1)
    causal = r2 >= c2
    for h in range(B_HEADS):
        wc = jnp.where(causal, ws_ref[h], 0.0).astype(BF16)
        bias = bs_ref[:, h:h + 1]
        cols = slice(h * B_HD, (h + 1) * B_HD)
        for n in range(bm // chunk):
            rows = slice(n * chunk, (n + 1) * chunk)
            mixed = jnp.dot(wc, v_ref[rows, cols].astype(BF16), preferred_element_type=F32) + bias
            gated_ref[rows, cols] = (u_ref[rows, cols] * mixed).astype(BF16)
    return gated_ref[...]


def _gmlp_out_body(u_ref, v_ref, ws_ref, bs_ref, w_ref, r_ref, g_ref, b_ref, o_ref, gated_ref, *, chunk):
    a = _gmlp_prologue(u_ref, v_ref, ws_ref, bs_ref, gated_ref, chunk=chunk)
    acc = jnp.dot(a, w_ref[...], preferred_element_type=F32)
    y = ALPHA * r_ref[...] + acc
    o_ref[...] = _ln_rows(y, g_ref[...], b_ref[...])


def _gmlp_out(uv, ws, bs_t, w, resid, g, b, bm, chunk):
    m = resid.shape[0]
    row = pl.BlockSpec((bm, D_MODEL), lambda i: (i, 0))
    vec = pl.BlockSpec((1, D_MODEL), lambda i: (0, 0))
    return pl.pallas_call(
        functools.partial(_gmlp_out_body, chunk=chunk),
        grid=(m // bm,),
        in_specs=[pl.BlockSpec((bm, D_MODEL), lambda i: (i, 0)),
                  pl.BlockSpec((bm, D_MODEL), lambda i: (i, 1)),
                  pl.BlockSpec(ws.shape, lambda i: (0, 0, 0)),
                  pl.BlockSpec(bs_t.shape, lambda i: (0, 0)),
                  pl.BlockSpec((D_MODEL, D_MODEL), lambda i: (0, 0)),
                  row, vec, vec],
        out_specs=row,
        out_shape=jax.ShapeDtypeStruct((m, D_MODEL), F32),
        scratch_shapes=[pltpu.VMEM((bm, D_MODEL), BF16)],
        compiler_params=_cparams("parallel"),
    )(uv, uv, ws, bs_t, w, resid, g, b)


def _band_attn_body(q_ref, k_ref, v_ref, o_ref, l_ref, *, n_blocks, bq):
    scale = C_HD ** -0.5
    span = C_KEYS - 1
    qi = lax.broadcasted_iota(jnp.int32, (bq, 2 * bq), 0)
    ki = lax.broadcasted_iota(jnp.int32, (bq, 2 * bq), 1)

    def block(i, carry):
        r0 = pl.multiple_of(i * bq, bq)
        w0 = pl.multiple_of(jnp.maximum(i - 1, 0) * bq, bq)
        q = q_ref[pl.ds(r0, bq), :].astype(BF16)
        kw = k_ref[pl.ds(w0, 2 * bq), :].astype(BF16)
        vw = v_ref[pl.ds(w0, 2 * bq), :].astype(BF16)
        s = _nt_dot(q, kw) * scale
        rel = (r0 + qi) - (w0 + ki)
        s = jnp.where((rel >= 0) & (rel <= span), s, -jnp.inf)
        mx = jnp.max(s, axis=-1, keepdims=True)
        p = jnp.exp(s - mx)
        den = jnp.sum(p, axis=-1, keepdims=True)
        o = jnp.dot(p.astype(BF16), vw, preferred_element_type=F32) / den
        o_ref[pl.ds(r0, bq), :] = o
        l_ref[pl.ds(r0, bq), :] = jnp.broadcast_to(mx + jnp.log(den), (bq, C_HD))
        return carry

    lax.fori_loop(0, n_blocks, block, 0)


def _band_attn(qkv, g, dil, batch, seq):
    ln = seq // dil
    bq = 128
    ng = len(C_GROUPS)
    wide = C_QKV // C_HD
    view = qkv.reshape(batch * ln, dil * C_QKV)

    def col(part):
        return pl.BlockSpec((ln, C_HD), lambda b, h, r: (b, r * wide + (part * ng + g) * C_HEADS + h))

    out = pl.BlockSpec((ln, C_HD), lambda b, h, r: (b, r * C_HEADS + h))
    o, lse = pl.pallas_call(
        functools.partial(_band_attn_body, n_blocks=ln // bq, bq=bq),
        grid=(batch, C_HEADS, dil),
        in_specs=[col(0), col(1), col(2)],
        out_specs=[out, out],
        out_shape=[jax.ShapeDtypeStruct((batch * ln, dil * C_HEADS * C_HD), F32)] * 2,
        compiler_params=_cparams("parallel", "parallel", "parallel"),
    )(view, view, view)
    return o.reshape(batch * seq, C_HEADS * C_HD), lse.reshape(batch * seq, C_HEADS * C_HD)


def _step_attn_body(qkv_ref, *refs, g, dil, n_tok):
    past_refs, (o_ref, l_ref) = refs[:-2], refs[-2:]
    scale = C_HD ** -0.5
    ng = len(C_GROUPS)
    tok = lax.broadcasted_iota(jnp.int32, (n_tok, 1), 0)
    prow = lax.broadcasted_iota(jnp.int32, (past_refs[0].shape[1], 1), 0)
    for t in range(n_tok):
        past_ref = past_refs[t % dil]
        new_ok = (tok <= t) & (((t - tok) % dil) == 0)
        past_ok = prow >= (t if dil == 1 else 0)
        for h in range(C_HEADS):
            def cols(part):
                base = ((part * ng + g) * C_HEADS + h) * C_HD
                return slice(base, base + C_HD)
            q = qkv_ref[0, t:t + 1, cols(0)]
            kn = qkv_ref[0, :, cols(1)]
            vn = qkv_ref[0, :, cols(2)]
            kp = past_ref[0, :, h * C_HD:(h + 1) * C_HD]
            vp = past_ref[0, :, (C_HEADS + h) * C_HD:(C_HEADS + h + 1) * C_HD]
            sp = jnp.where(past_ok, jnp.sum(kp * q, axis=-1, keepdims=True) * scale, -jnp.inf)
            sn = jnp.where(new_ok, jnp.sum(kn * q, axis=-1, keepdims=True) * scale, -jnp.inf)
            mx = jnp.maximum(jnp.max(sp, axis=0, keepdims=True), jnp.max(sn, axis=0, keepdims=True))
            pp = jnp.exp(sp - mx)
            pn = jnp.exp(sn - mx)
            den = jnp.sum(pp, axis=0, keepdims=True) + jnp.sum(pn, axis=0, keepdims=True)
            o = (jnp.sum(pp * vp, axis=0, keepdims=True) + jnp.sum(pn * vn, axis=0, keepdims=True)) / den
            o_ref[0, t:t + 1, h * C_HD:(h + 1) * C_HD] = o
            l_ref[0, t:t + 1, h * C_HD:(h + 1) * C_HD] = jnp.broadcast_to(mx + jnp.log(den), (1, C_HD))


def _step_attn(qkv, cache, g, window, dil, batch, n_tok):
    kvw = 2 * C_HEADS * C_HD
    past = cache.reshape(batch, window // dil, dil * kvw)
    n_res = min(dil, n_tok)
    out = pl.BlockSpec((1, n_tok, C_HEADS * C_HD), lambda b: (b, 0, 0))
    past_specs = [pl.BlockSpec((1, window // dil, kvw), functools.partial(lambda b, r: (b, 0, r), r=r))
                  for r in range(n_res)]
    return pl.pallas_call(
        functools.partial(_step_attn_body, g=g, dil=dil, n_tok=n_tok),
        grid=(batch,),
        in_specs=[pl.BlockSpec((1, n_tok, C_QKV), lambda b: (b, 0, 0))] + past_specs,
        out_specs=[out, out],
        out_shape=[jax.ShapeDtypeStruct((batch, n_tok, C_HEADS * C_HD), F32)] * 2,
        compiler_params=_cparams("parallel"),
    )(qkv, *([past] * n_res))


def _rope_tables(pos):
    half = C_HD // 2
    inv = ROPE_THETA ** (-jnp.arange(half, dtype=F32) / half)
    ang = pos.astype(F32)[:, None] * inv[None, :]
    cos, sin = jnp.cos(ang), jnp.sin(ang)
    return jnp.concatenate([cos, cos], -1), jnp.concatenate([-sin, sin], -1)


def _row_tile(m, cap):
    return min(m, cap)


def _hgrn_layer(x, batch, seq, s0, w_in, lb, ng, w_out, ln_g, ln_b):
    m = x.shape[0]
    bn = 1024
    p = _proj(x, w_in, (), (), functools.partial(_hgrn_proj_epilogue, bn=bn), _row_tile(m, 512), bn)
    if s0 is None:
        o, s_new = _hgrn_scan(p, lb, ng, batch, seq, min(seq, 512))
    else:
        rows = 8
        pp = jnp.pad(p.reshape(batch, seq, -1), ((0, 0), (0, rows - seq), (0, 0))).reshape(batch * rows, -1)
        o, s_new = _hgrn_step(pp, s0, lb, ng, batch, seq)
        o = o.reshape(batch, rows, -1)[:, :seq].reshape(m, -1)
    bm = _row_tile(m, 512)
    x = _out_ln((o,), (pl.BlockSpec((bm, o.shape[1]), lambda i: (i, 0)),), _cast_prologue,
                w_out, x, ln_g, ln_b, bm)
    return x, s_new


def _gmlp_layer(x, batch, seq, w_in, b_in, g1, b1, ws, bs, w_out, ln_g, ln_b):
    m = x.shape[0]
    bm = _row_tile(m, 512)
    vec = pl.BlockSpec((1, D_MODEL), lambda i, j: (0, 0))
    uv = _proj(x, w_in, (b_in, g1, b1),
               (pl.BlockSpec((1, D_MODEL), lambda i, j: (0, j)), vec, vec),
               _gmlp_proj_epilogue, bm, D_MODEL)
    if seq % B_CHUNK == 0:
        chunk, ws_c, bs_t = B_CHUNK, ws, bs.T
    else:
        chunk = m
        eye = jnp.eye(batch, dtype=ws.dtype)
        ws_c = jnp.einsum("ab,hts->hatbs", eye, ws[:, :seq, :seq]).reshape(B_HEADS, m, m)
        bs_t = jnp.tile(bs[:, :seq].T, (batch, 1))
    x = _gmlp_out(uv, ws_c, bs_t, w_out, x, ln_g, ln_b, _row_tile(m, 256), chunk)
    return x, uv[:, D_MODEL:]


def _attn_layer(x, batch, seq, caches, pos0, w_in, w_out, ln_g, ln_b):
    m = x.shape[0]
    bn = 1024
    bm = _row_tile(m, 512)
    cos, sin = _rope_tables(pos0 + jnp.arange(seq, dtype=jnp.int32))
    cos, sin = jnp.tile(cos, (batch, 1)), jnp.tile(sin, (batch, 1))
    tab = pl.BlockSpec((bm, C_HD), lambda i, j: (i, 0))
    qkv = _proj(x, w_in, (cos, sin), (tab, tab), functools.partial(_attn_proj_epilogue, bn=bn), bm, bn)
    outs, lses = [], []
    for g, (window, dil) in enumerate(C_GROUPS):
        if caches is None:
            o, lse = _band_attn(qkv, g, dil, batch, seq)
        else:
            o, lse = _step_attn(qkv.reshape(batch, seq, C_QKV), caches[g], g, window, dil, batch, seq)
            o, lse = o.reshape(m, -1), lse.reshape(m, -1)
        outs.append(o)
        lses.append(lse)
    bm2 = _row_tile(m, 256)
    spec = pl.BlockSpec((bm2, C_HEADS * C_HD), lambda i: (i, 0))
    x = _out_ln(tuple(outs) + tuple(lses), (spec,) * 6, _merge_prologue, w_out, x, ln_g, ln_b, bm2)
    hw = C_HEADS * C_HD
    ng = len(C_GROUPS)
    kv = []
    for g, (window, _) in enumerate(C_GROUPS):
        k = qkv[:, (ng + g) * hw:(ng + g + 1) * hw].reshape(batch, seq, C_HEADS, C_HD)
        v = qkv[:, (2 * ng + g) * hw:(2 * ng + g + 1) * hw].reshape(batch, seq, C_HEADS, C_HD)
        kv.append(jnp.stack([k, v], axis=2)[:, seq - min(window, seq):])
    return x, kv


def kernel(x_prompt, x_sample, state_hgrn, cache_c_kv_w128, cache_c_kv_w512, cache_c_kv_w2048, ln_g, ln_b, a_w_in, a_lb_logits, a_norm_g, a_w_out, b_w_in, b_b_in, b_ln_g, b_ln_b, b_w_s, b_b_s, b_w_out, c_w_in, c_w_out, moe_w_group, moe_w_expert, moe_w1, moe_w3, moe_w2):
    bp, tp, _ = x_prompt.shape
    bs, ts, _ = x_sample.shape
    assert tp % 512 == 0 and ts <= 8 and ts <= C_GROUPS[1][1]

    lb_p = jax.nn.softmax(a_lb_logits.astype(F32), axis=0)
    lb_all = jnp.clip(jnp.cumsum(lb_p, axis=0) - lb_p[0:1], 0.0, 1.0 - 1e-6)
    caches = (cache_c_kv_w128, cache_c_kv_w512, cache_c_kv_w2048)

    xp = x_prompt.reshape(bp * tp, D_MODEL)
    xs = x_sample.reshape(bs * ts, D_MODEL)
    hgrn_p, hgrn_s, chunk_v_s = [], [], []
    kv_p = [[] for _ in C_GROUPS]
    kv_s = [[] for _ in C_GROUPS]

    for i in range(DEPTH):
        kind, j = i % 3, i // 3
        g0, b0 = ln_g[i, 0][None], ln_b[i, 0][None]
        if kind == 0:
            w_in, w_out = a_w_in[j].astype(BF16), a_w_out[j].astype(BF16)
            lb, ng = lb_all[j][None], a_norm_g[j][None]
            xp, sp = _hgrn_layer(xp, bp, tp, None, w_in, lb, ng, w_out, g0, b0)
            xs, ss = _hgrn_layer(xs, bs, ts, state_hgrn[j].astype(F32), w_in, lb, ng, w_out, g0, b0)
            hgrn_p.append(sp)
            hgrn_s.append(ss)
        elif kind == 1:
            w_in, w_out = b_w_in[j].astype(BF16), b_w_out[j].astype(BF16)
            args = (w_in, b_b_in[j][None], b_ln_g[j][None], b_ln_b[j][None], b_w_s[j], b_b_s[j], w_out, g0, b0)
            xp, _ = _gmlp_layer(xp, bp, tp, *args)
            xs, vs = _gmlp_layer(xs, bs, ts, *args)
            chunk_v_s.append(vs.reshape(bs, ts, D_MODEL))
        else:
            w_in, w_out = c_w_in[j].astype(BF16), c_w_out[j].astype(BF16)
            xp, kvp = _attn_layer(xp, bp, tp, None, 0, w_in, w_out, g0, b0)
            xs, kvs = _attn_layer(xs, bs, ts, tuple(c[j] for c in caches), PAST_LEN, w_in, w_out, g0, b0)
            for g in range(len(C_GROUPS)):
                kv_p[g].append(kvp[g])
                kv_s[g].append(kvs[g])
        wr = jnp.pad(jnp.concatenate([moe_w_expert[i], moe_w_group[i]], axis=1),
                     ((0, 0), (0, LANES - MOE_GE - MOE_GROUPS)))
        w1, w3, w2 = moe_w1[i].astype(BF16), moe_w3[i].astype(BF16), moe_w2[i].astype(BF16)
        g1, b1 = ln_g[i, 1][None], ln_b[i, 1][None]
        xp = _moe(xp, wr, w1, w3, w2, g1, b1, _row_tile(bp * tp, 512))
        xs = _moe(xs, wr, w1, w3, w2, g1, b1, bs * ts)

    return (xp.reshape(bp, tp, D_MODEL), xs.reshape(bs, ts, D_MODEL),
            jnp.stack(hgrn_p), jnp.stack(hgrn_s), jnp.stack(chunk_v_s),
            jnp.stack(kv_p[0]), jnp.stack(kv_s[0]), jnp.stack(kv_p[1]), jnp.stack(kv_s[1]),
            jnp.stack(kv_p[2]), jnp.stack(kv_s[2]))
```

```python
import functools

import jax
import jax.numpy as jnp
from jax import lax
from jax.experimental import pallas as pl
from jax.experimental.pallas import tpu as pltpu

F32 = jnp.float32
BF16 = jnp.bfloat16
HIGHEST = lax.Precision.HIGHEST

D_MODEL = 2048
DEPTH = 4
PAST_LEN = 16384
A_HEADS = 16
A_DK = 128
A_DV = 128
A_WIDTH = A_HEADS * A_DK
HGRN_CHUNK = 64
HGRN_SUB = 16
HGRN_SUPER = 256
B_CHUNK = 128
B_HEADS = 16
B_HD = 128
C_HEADS = 8
C_HD = 128
C_GROUPS = ((128, 1), (512, 4), (2048, 16))
C_KEYS = 129
C_QBLOCK = 128
C_QKV = 3 * len(C_GROUPS) * C_HEADS * C_HD
ROPE_THETA = 10000.0
MOE_GROUPS = 4
MOE_EXPERTS = 4
MOE_GE = MOE_GROUPS * MOE_EXPERTS
MOE_FF = 256
LN_EPS = 1e-5
RMS_EPS = 1e-6
ALPHA = (2 * DEPTH) ** 0.25
LANES = 128
VMEM_LIMIT = 56 * 1024 * 1024


def _cparams(*sem):
    return pltpu.CompilerParams(dimension_semantics=sem, vmem_limit_bytes=VMEM_LIMIT)


def _sigmoid(x):
    return 1.0 / (1.0 + jnp.exp(-x))


def _ln_rows(y, g, b):
    mu = jnp.mean(y, axis=-1, keepdims=True)
    d = y - mu
    var = jnp.mean(d * d, axis=-1, keepdims=True)
    return d * lax.rsqrt(var + LN_EPS) * g + b


def _nt_dot(a, b):
    return lax.dot_general(a, b, (((1,), (1,)), ((), ())), preferred_element_type=F32)


def _tn_dot(a, b):
    return lax.dot_general(a, b, (((0,), (0,)), ((), ())), preferred_element_type=F32)


def _proj_body(x_ref, w_ref, *rest, epilogue, n_extra):
    extras = rest[:n_extra]
    o_ref = rest[n_extra]
    xb_ref = rest[n_extra + 1]
    j = pl.program_id(1)

    @pl.when(j == 0)
    def _cast():
        xb_ref[...] = x_ref[...].astype(BF16)

    acc = jnp.dot(xb_ref[...], w_ref[...], preferred_element_type=F32)
    epilogue(acc, j, extras, o_ref)


def _proj(x, w, extras, extra_specs, epilogue, bm, bn, name):
    m, k = x.shape
    n = w.shape[1]
    return pl.pallas_call(
        functools.partial(_proj_body, epilogue=epilogue, n_extra=len(extras)),
        grid=(m // bm, n // bn),
        in_specs=[pl.BlockSpec((bm, k), lambda i, j: (i, 0)),
                  pl.BlockSpec((k, bn), lambda i, j: (0, j))] + list(extra_specs),
        out_specs=pl.BlockSpec((bm, bn), lambda i, j: (i, j)),
        out_shape=jax.ShapeDtypeStruct((m, n), F32),
        scratch_shapes=[pltpu.VMEM((bm, k), BF16)],
        compiler_params=_cparams("parallel", "arbitrary"),
        name=name,
    )(x, w, *extras)


def _hgrn_proj_epilogue(acc, j, extras, o_ref, *, bn):
    nq = A_WIDTH // bn
    is_silu = jnp.logical_or(j < nq, j >= 3 * nq)

    @pl.when(is_silu)
    def _():
        o_ref[...] = acc * _sigmoid(acc)

    @pl.when(jnp.logical_not(is_silu))
    def _():
        o_ref[...] = acc


def _gelu_tanh(z):
    return 0.5 * z * (1.0 + jnp.tanh(0.7978845608028654 * (z + 0.044715 * (z * z * z))))


def _gmlp_proj_epilogue(acc, j, extras, o_ref):
    bias_ref, g_ref, b_ref = extras
    z = _gelu_tanh(acc + bias_ref[...])

    @pl.when(j == 0)
    def _():
        o_ref[...] = z

    @pl.when(j == 1)
    def _():
        o_ref[...] = _ln_rows(z, g_ref[...], b_ref[...])


def _attn_proj_epilogue(acc, j, extras, o_ref, *, bn):
    cos_ref, sin_ref = extras
    n_rot = 2 * len(C_GROUPS) * C_HEADS * C_HD // bn

    @pl.when(j < n_rot)
    def _():
        cos = cos_ref[...]
        sin = sin_ref[...]
        for h in range(bn // C_HD):
            xh = acc[:, h * C_HD:(h + 1) * C_HD]
            o_ref[:, h * C_HD:(h + 1) * C_HD] = xh * cos + pltpu.roll(xh, C_HD // 2, 1) * sin

    @pl.when(j >= n_rot)
    def _():
        o_ref[...] = acc


def _out_ln_body(*refs, prologue, n_in):
    ins = refs[:n_in]
    w_ref, r_ref, g_ref, b_ref, o_ref = refs[n_in:n_in + 5]
    a = prologue(*ins)
    acc = jnp.dot(a, w_ref[...], preferred_element_type=F32)
    y = ALPHA * r_ref[...] + acc
    o_ref[...] = _ln_rows(y, g_ref[...], b_ref[...])


def _out_ln(ins, in_specs, prologue, w, resid, g, b, bm, name):
    m = resid.shape[0]
    k = w.shape[0]
    row = pl.BlockSpec((bm, D_MODEL), lambda i: (i, 0))
    vec = pl.BlockSpec((1, D_MODEL), lambda i: (0, 0))
    return pl.pallas_call(
        functools.partial(_out_ln_body, prologue=prologue, n_in=len(ins)),
        grid=(m // bm,),
        in_specs=list(in_specs) + [pl.BlockSpec((k, D_MODEL), lambda i: (0, 0)), row, vec, vec],
        out_specs=row,
        out_shape=jax.ShapeDtypeStruct((m, D_MODEL), F32),
        compiler_params=_cparams("parallel"),
        name=name,
    )(*ins, w, resid, g, b)


def _cast_prologue(a_ref):
    return a_ref[...].astype(BF16)


def _merge_prologue(o0, o1, o2, l0, l1, l2):
    a0, a1, a2 = l0[...], l1[...], l2[...]
    mx = jnp.maximum(jnp.maximum(a0, a1), a2)
    e0, e1, e2 = jnp.exp(a0 - mx), jnp.exp(a1 - mx), jnp.exp(a2 - mx)
    o = (e0 * o0[...] + e1 * o1[...] + e2 * o2[...]) / (e0 + e1 + e2)
    return o.astype(BF16)


def _moe_gate(logits):
    lane = lax.broadcasted_iota(jnp.int32, logits.shape, 1).astype(F32)
    neg = -jnp.inf
    big = 4.0 * LANES
    gl = jnp.where((lane >= MOE_GE) & (lane < MOE_GE + MOE_GROUPS), logits, neg)
    gmax = jnp.max(gl, axis=-1, keepdims=True)
    g_idx = jnp.min(jnp.where(gl == gmax, lane - MOE_GE, big), axis=-1, keepdims=True)
    g_top = 1.0 / jnp.sum(jnp.exp(gl - gmax), axis=-1, keepdims=True)
    lo = g_idx * MOE_EXPERTS
    el = jnp.where((lane >= lo) & (lane < lo + MOE_EXPERTS), logits, neg)
    m1 = jnp.max(el, axis=-1, keepdims=True)
    i1 = jnp.min(jnp.where(el == m1, lane, big), axis=-1, keepdims=True)
    el2 = jnp.where(lane == i1, neg, el)
    m2 = jnp.max(el2, axis=-1, keepdims=True)
    i2 = jnp.min(jnp.where(el2 == m2, lane, big), axis=-1, keepdims=True)
    r = jnp.exp(m2 - m1)
    w1 = g_top / (1.0 + r)
    w2 = w1 * r
    return jnp.where(lane == i1, w1, 0.0) + jnp.where(lane == i2, w2, 0.0)


def _moe_body(x_ref, wr_ref, w1_ref, w3_ref, w2_ref, g_ref, b_ref, o_ref, xb_ref, gate_ref, acc_ref):
    e = pl.program_id(1)

    @pl.when(e == 0)
    def _route():
        x = x_ref[...]
        xb_ref[...] = x.astype(BF16)
        logits = jnp.dot(x, wr_ref[...], precision=HIGHEST, preferred_element_type=F32)
        gate_ref[...] = _moe_gate(logits)
        acc_ref[...] = jnp.zeros_like(acc_ref)

    xb = xb_ref[...]
    h1 = jnp.dot(xb, w1_ref[0], preferred_element_type=F32)
    h3 = jnp.dot(xb, w3_ref[0], preferred_element_type=F32)
    gate = gate_ref[...]
    lane = lax.broadcasted_iota(jnp.int32, gate.shape, 1)
    ge = jnp.sum(jnp.where(lane == e, gate, 0.0), axis=-1, keepdims=True)
    hg = (h1 * _sigmoid(h1) * h3 * ge).astype(BF16)
    acc_ref[...] += jnp.dot(hg, w2_ref[0], preferred_element_type=F32)

    @pl.when(e == MOE_GE - 1)
    def _finish():
        y = ALPHA * x_ref[...] + acc_ref[...]
        o_ref[...] = _ln_rows(y, g_ref[...], b_ref[...])


def _moe(x, wr, w1, w3, w2, g, b, bm):
    m = x.shape[0]
    row = pl.BlockSpec((bm, D_MODEL), lambda i, e: (i, 0))
    vec = pl.BlockSpec((1, D_MODEL), lambda i, e: (0, 0))
    return pl.pallas_call(
        _moe_body,
        grid=(m // bm, MOE_GE),
        in_specs=[row,
                  pl.BlockSpec((D_MODEL, LANES), lambda i, e: (0, 0)),
                  pl.BlockSpec((1, D_MODEL, MOE_FF), lambda i, e: (e, 0, 0)),
                  pl.BlockSpec((1, D_MODEL, MOE_FF), lambda i, e: (e, 0, 0)),
                  pl.BlockSpec((1, MOE_FF, D_MODEL), lambda i, e: (e, 0, 0)),
                  vec, vec],
        out_specs=row,
        out_shape=jax.ShapeDtypeStruct((m, D_MODEL), F32),
        scratch_shapes=[pltpu.VMEM((bm, D_MODEL), BF16),
                        pltpu.VMEM((bm, LANES), F32),
                        pltpu.VMEM((bm, D_MODEL), F32)],
        compiler_params=_cparams("parallel", "arbitrary"),
        name="moe",
    )(x, wr, w1, w3, w2, g, b)


def _hgrn_gates(f, lb):
    log_sig = jnp.minimum(f, 0.0) - jnp.log1p(jnp.exp(-jnp.abs(f)))
    a = jnp.log1p(-lb) + log_sig
    log_lb = jnp.log(lb)
    log_f = jnp.maximum(log_lb, a) + jnp.log1p(jnp.exp(-jnp.abs(log_lb - a)))
    k = (1.0 - lb) / (1.0 + jnp.exp(f))
    return log_f, k


def _hgrn_finish(o, gate, ng):
    o = o * lax.rsqrt(jnp.mean(o * o, axis=-1, keepdims=True) + RMS_EPS) * ng
    return o * gate


def _rows_of(x, idx, n):
    return jnp.concatenate([jnp.broadcast_to(x[i:i + 1, :], (n, x.shape[1])) for i in idx], axis=0)


def _hgrn_scan_body(q_ref, f_ref, v_ref, g_ref, lb_ref, ng_ref, tri_ref, o_ref, s_ref, st_ref, *, n_super):
    C, c, N = HGRN_CHUNK, HGRN_SUB, HGRN_SUPER
    half = C // 2
    neg = -1e30
    tb = pl.program_id(2)

    @pl.when(tb == 0)
    def _init():
        st_ref[...] = jnp.zeros_like(st_ref)

    lb = lb_ref[...]
    ng = ng_ref[...]
    row = lax.broadcasted_iota(jnp.int32, (N, 1), 0)
    hi32 = (row & (C - 1)) >= half
    hi16 = (row & (half - 1)) >= c
    lane = lax.broadcasted_iota(jnp.int32, (8, LANES), 1)
    srow = lax.broadcasted_iota(jnp.int32, (8, 1), 0)

    def super_chunk(si, carry):
        r0 = pl.multiple_of(si * N, N)
        q = q_ref[pl.ds(r0, N), :]
        v = v_ref[pl.ds(r0, N), :].astype(BF16)
        gate = g_ref[pl.ds(r0, N), :]
        log_f, k = _hgrn_gates(f_ref[pl.ds(r0, N), :], lb)
        b = jnp.dot(tri_ref[...], log_f, precision=HIGHEST, preferred_element_type=F32)

        ra = _rows_of(b, [C * m + half - 1 for m in range(N // C)], C)
        qa = jnp.where(hi32, q * jnp.exp(jnp.where(hi32, b - ra, 0.0)), 0.0)
        ka = jnp.where(hi32, 0.0, k * jnp.exp(jnp.where(hi32, 0.0, ra - b)))
        rb = _rows_of(b, [half * m + c - 1 for m in range(N // half)], half)
        qb = jnp.where(hi16, q * jnp.exp(jnp.where(hi16, b - rb, 0.0)), 0.0)
        kb = jnp.where(hi16, 0.0, k * jnp.exp(jnp.where(hi16, 0.0, rb - b)))
        s2 = lax.broadcasted_iota(jnp.int32, (N, N), 0)
        t2 = lax.broadcasted_iota(jnp.int32, (N, N), 1)
        same64 = jnp.right_shift(s2, 6) == jnp.right_shift(t2, 6)
        same32 = jnp.right_shift(s2, 5) == jnp.right_shift(t2, 5)
        att_t = (jnp.where(same64, _nt_dot(ka.astype(BF16), qa.astype(BF16)), 0.0)
                 + jnp.where(same32, _nt_dot(kb.astype(BF16), qb.astype(BF16)), 0.0))

        pieces = [jnp.zeros((8, LANES), F32)] * (N // 8)
        for i in range(N // c):
            base = c * i
            ks = (k[base:base + 8, :], k[base + 8:base + c, :])
            bs = (b[base:base + 8, :], b[base + 8:base + c, :])
            for tl in range(c):
                qt = q[base + tl:base + tl + 1, :]
                bt = b[base + tl:base + tl + 1, :]
                for u in range(tl // 8 + 1):
                    d = bt - bs[u]
                    if u == tl // 8:
                        d = jnp.where(srow <= tl - 8 * u, d, neg)
                    col = jnp.sum(qt * ks[u] * jnp.exp(d), axis=-1, keepdims=True)
                    pieces[2 * i + u] = jnp.where(lane == base % LANES + tl, col, pieces[2 * i + u])
        zero = jnp.zeros((N // 2, LANES), F32)
        n_lo = N // 16
        diag = jnp.concatenate(
            [jnp.concatenate(pieces[:n_lo] + [zero], axis=0),
             jnp.concatenate([zero] + pieces[n_lo:], axis=0)], axis=1)
        o_intra = _tn_dot((att_t + diag).astype(BF16), v)

        bl = _rows_of(b, [C * m + C - 1 for m in range(N // C)], C)
        qe = (q * jnp.exp(b)).astype(BF16)
        kd = (k * jnp.exp(bl - b)).astype(BF16)
        st = st_ref[...]
        for m in range(N // C):
            rows = slice(C * m, C * m + C)
            o = o_intra[rows, :] + _nt_dot(qe[rows, :], st.astype(BF16))
            o_ref[pl.ds(r0 + C * m, C), :] = _hgrn_finish(o, gate[rows, :], ng).astype(o_ref.dtype)
            st = st * jnp.exp(b[C * m + C - 1:C * m + C, :]) + _tn_dot(v[rows, :], kd[rows, :])
        st_ref[...] = st
        return carry

    lax.fori_loop(0, n_super, super_chunk, 0)

    @pl.when(tb == pl.num_programs(2) - 1)
    def _emit():
        s_ref[0, 0] = st_ref[...].T


def _hgrn_scan(p, lb, ng, batch, seq, tb):
    nt = seq // tb
    nh = A_HEADS
    n = HGRN_SUPER
    idx = jnp.arange(n)
    tri = ((idx[:, None] >= idx[None, :])
           & (idx[:, None] // HGRN_CHUNK == idx[None, :] // HGRN_CHUNK)).astype(F32)

    def col(off):
        return pl.BlockSpec((tb, A_DK), lambda b, h, t: (b * nt + t, off * nh + h))

    return pl.pallas_call(
        functools.partial(_hgrn_scan_body, n_super=tb // n),
        grid=(batch, nh, nt),
        in_specs=[col(0), col(1), col(2), col(3),
                  pl.BlockSpec((1, A_DK), lambda b, h, t: (0, h)),
                  pl.BlockSpec((1, A_DV), lambda b, h, t: (0, 0)),
                  pl.BlockSpec((n, n), lambda b, h, t: (0, 0))],
        out_specs=[pl.BlockSpec((tb, A_DV), lambda b, h, t: (b * nt + t, h)),
                   pl.BlockSpec((1, 1, A_DK, A_DV), lambda b, h, t: (b, h, 0, 0))],
        out_shape=[jax.ShapeDtypeStruct((batch * seq, nh * A_DV), BF16),
                   jax.ShapeDtypeStruct((batch, nh, A_DK, A_DV), F32)],
        scratch_shapes=[pltpu.VMEM((A_DV, A_DK), F32)],
        compiler_params=_cparams("parallel", "parallel", "arbitrary"),
        name="hgrn_scan",
    )(p, p, p, p, lb, ng, tri)


def _hgrn_step_body(p_ref, s0_ref, lb_ref, ng_ref, o_ref, s_ref, *, n_tok):
    R = p_ref.shape[0]
    row = lax.broadcasted_iota(jnp.int32, (R, 1), 0)
    valid = row < n_tok
    r2 = lax.broadcasted_iota(jnp.int32, (R, R), 0)
    c2 = lax.broadcasted_iota(jnp.int32, (R, R), 1)
    tril = (r2 >= c2).astype(F32)
    ng = ng_ref[...]
    for h in range(A_HEADS):
        sl = slice(h * A_DK, (h + 1) * A_DK)
        q = p_ref[:, sl]
        v = p_ref[:, 2 * A_WIDTH + h * A_DV:2 * A_WIDTH + (h + 1) * A_DV]
        gate = p_ref[:, 3 * A_WIDTH + h * A_DV:3 * A_WIDTH + (h + 1) * A_DV]
        log_f, k = _hgrn_gates(p_ref[:, A_WIDTH + h * A_DK:A_WIDTH + (h + 1) * A_DK], lb_ref[:, sl])
        b = jnp.dot(tril, log_f, precision=HIGHEST, preferred_element_type=F32)
        st = s0_ref[0, h].T
        o = _nt_dot((q * jnp.exp(b)).astype(BF16), st.astype(BF16))
        for s in range(n_tok):
            m = row >= s
            w = jnp.where(m, q * k[s:s + 1, :] * jnp.exp(jnp.where(m, b - b[s:s + 1, :], 0.0)), 0.0)
            o = o + jnp.sum(w, axis=-1, keepdims=True) * v[s:s + 1, :]
        o_ref[:, h * A_DV:(h + 1) * A_DV] = _hgrn_finish(o, gate, ng)
        bl = b[n_tok - 1:n_tok, :]
        kd = jnp.where(valid, k * jnp.exp(jnp.where(valid, bl - b, 0.0)), 0.0)
        st_new = st * jnp.exp(bl) + _tn_dot(v.astype(BF16), kd.astype(BF16))
        s_ref[0, h] = st_new.T


def _hgrn_step(p, s0, lb, ng, batch, n_tok):
    rows = p.shape[0] // batch
    return pl.pallas_call(
        functools.partial(_hgrn_step_body, n_tok=n_tok),
        grid=(batch,),
        in_specs=[pl.BlockSpec((rows, 4 * A_WIDTH), lambda b: (b, 0)),
                  pl.BlockSpec((1, A_HEADS, A_DK, A_DV), lambda b: (b, 0, 0, 0)),
                  pl.BlockSpec((1, A_WIDTH), lambda b: (0, 0)),
                  pl.BlockSpec((1, A_DV), lambda b: (0, 0))],
        out_specs=[pl.BlockSpec((rows, A_HEADS * A_DV), lambda b: (b, 0)),
                   pl.BlockSpec((1, A_HEADS, A_DK, A_DV), lambda b: (b, 0, 0, 0))],
        out_shape=[jax.ShapeDtypeStruct((batch * rows, A_HEADS * A_DV), F32),
                   jax.ShapeDtypeStruct((batch, A_HEADS, A_DK, A_DV), F32)],
        compiler_params=_cparams("parallel"),
        name="hgrn_step",
    )(p, s0, lb, ng)


def _gmlp_prologue(u_ref, v_ref, ws_ref, bs_ref, gated_ref, *, chunk):
    bm = u_ref.shape[0]
    r2 = lax.broadcasted_iota(jnp.int32, (chunk, chunk), 0)
    c2 = lax.broadcasted_iota(jnp.int32, (chunk, chunk), 1)
    causal = r2 >= c2
    for h in range(B_HEADS):
        wc = jnp.where(causal, ws_ref[h], 0.0).astype(BF16)
        bias = bs_ref[:, h:h + 1]
        cols = slice(h * B_HD, (h + 1) * B_HD)
        for n in range(bm // chunk):
            rows = slice(n * chunk, (n + 1) * chunk)
            mixed = jnp.dot(wc, v_ref[rows, cols].astype(BF16), preferred_element_type=F32) + bias
            gated_ref[rows, cols] = (u_ref[rows, cols] * mixed).astype(BF16)
    return gated_ref[...]


def _gmlp_out_body(u_ref, v_ref, ws_ref, bs_ref, w_ref, r_ref, g_ref, b_ref, o_ref, gated_ref, *, chunk):
    a = _gmlp_prologue(u_ref, v_ref, ws_ref, bs_ref, gated_ref, chunk=chunk)
    acc = jnp.dot(a, w_ref[...], preferred_element_type=F32)
    y = ALPHA * r_ref[...] + acc
    o_ref[...] = _ln_rows(y, g_ref[...], b_ref[...])


def _gmlp_out(uv, ws, bs_t, w, resid, g, b, bm, chunk):
    m = resid.shape[0]
    row = pl.BlockSpec((bm, D_MODEL), lambda i: (i, 0))
    vec = pl.BlockSpec((1, D_MODEL), lambda i: (0, 0))
    return pl.pallas_call(
        functools.partial(_gmlp_out_body, chunk=chunk),
        grid=(m // bm,),
        in_specs=[pl.BlockSpec((bm, D_MODEL), lambda i: (i, 0)),
                  pl.BlockSpec((bm, D_MODEL), lambda i: (i, 1)),
                  pl.BlockSpec(ws.shape, lambda i: (0, 0, 0)),
                  pl.BlockSpec(bs_t.shape, lambda i: (0, 0)),
                  pl.BlockSpec((D_MODEL, D_MODEL), lambda i: (0, 0)),
                  row, vec, vec],
        out_specs=row,
        out_shape=jax.ShapeDtypeStruct((m, D_MODEL), F32),
        scratch_shapes=[pltpu.VMEM((bm, D_MODEL), BF16)],
        compiler_params=_cparams("parallel"),
        name="gmlp_out",
    )(uv, uv, ws, bs_t, w, resid, g, b)


def _band_attn_body(q_ref, k_ref, v_ref, o_ref, l_ref, *, dil, seq):
    bq = C_QBLOCK
    span = C_KEYS - 1
    scale = C_HD ** -0.5
    n_blocks = seq // dil // bq
    qi = lax.broadcasted_iota(jnp.int32, (bq, 2 * bq), 0)
    ki = lax.broadcasted_iota(jnp.int32, (bq, 2 * bq), 1)

    def rows(first, n):
        return pl.ds(first, n) if dil == 1 else pl.ds(first, n, stride=dil)

    for r in range(dil):
        for i in range(n_blocks):
            w = max(i - 1, 0)
            qs = rows(r + dil * bq * i, bq)
            ws = rows(r + dil * bq * w, 2 * bq)
            q = q_ref[qs, :].astype(BF16)
            kw = k_ref[ws, :].astype(BF16)
            vw = v_ref[ws, :].astype(BF16)
            s = _nt_dot(q, kw) * scale
            rel = bq * (i - w) + qi - ki
            s = jnp.where((rel >= 0) & (rel <= span), s, -jnp.inf)
            mx = jnp.max(s, axis=-1, keepdims=True)
            p = jnp.exp(s - mx)
            den = jnp.sum(p, axis=-1, keepdims=True)
            o_ref[qs, :] = jnp.dot(p.astype(BF16), vw, preferred_element_type=F32) / den
            l_ref[qs, :] = jnp.broadcast_to(mx + jnp.log(den), (bq, C_HD))


def _band_attn(qkv, g, dil, batch, seq):
    ng = len(C_GROUPS)

    def col(part):
        return pl.BlockSpec((seq, C_HD), lambda b, h: (b, (part * ng + g) * C_HEADS + h))

    out = pl.BlockSpec((seq, C_HD), lambda b, h: (b, h))
    return pl.pallas_call(
        functools.partial(_band_attn_body, dil=dil, seq=seq),
        grid=(batch, C_HEADS),
        in_specs=[col(0), col(1), col(2)],
        out_specs=[out, out],
        out_shape=[jax.ShapeDtypeStruct((batch * seq, C_HEADS * C_HD), F32)] * 2,
        compiler_params=_cparams("parallel", "parallel"),
        name=f"band_attn_d{dil}",
    )(qkv, qkv, qkv)


def _step_attn_body(qkv_ref, *refs, g, dil, n_tok):
    past_refs, (o_ref, l_ref) = refs[:-2], refs[-2:]
    scale = C_HD ** -0.5
    ng = len(C_GROUPS)
    tok = lax.broadcasted_iota(jnp.int32, (n_tok, 1), 0)
    prow = lax.broadcasted_iota(jnp.int32, (past_refs[0].shape[1], 1), 0)
    for t in range(n_tok):
        past_ref = past_refs[t % dil]
        new_ok = (tok <= t) & (((t - tok) % dil) == 0)
        past_ok = prow >= (t if dil == 1 else 0)
        for h in range(C_HEADS):
            def cols(part):
                base = ((part * ng + g) * C_HEADS + h) * C_HD
                return slice(base, base + C_HD)
            q = qkv_ref[0, t:t + 1, cols(0)]
            kn = qkv_ref[0, :, cols(1)]
            vn = qkv_ref[0, :, cols(2)]
            kp = past_ref[0, :, h * C_HD:(h + 1) * C_HD]
            vp = past_ref[0, :, (C_HEADS + h) * C_HD:(C_HEADS + h + 1) * C_HD]
            sp = jnp.where(past_ok, jnp.sum(kp * q, axis=-1, keepdims=True) * scale, -jnp.inf)
            sn = jnp.where(new_ok, jnp.sum(kn * q, axis=-1, keepdims=True) * scale, -jnp.inf)
            mx = jnp.maximum(jnp.max(sp, axis=0, keepdims=True), jnp.max(sn, axis=0, keepdims=True))
            pp = jnp.exp(sp - mx)
            pn = jnp.exp(sn - mx)
            den = jnp.sum(pp, axis=0, keepdims=True) + jnp.sum(pn, axis=0, keepdims=True)
            o = (jnp.sum(pp * vp, axis=0, keepdims=True) + jnp.sum(pn * vn, axis=0, keepdims=True)) / den
            o_ref[0, t:t + 1, h * C_HD:(h + 1) * C_HD] = o
            l_ref[0, t:t + 1, h * C_HD:(h + 1) * C_HD] = jnp.broadcast_to(mx + jnp.log(den), (1, C_HD))


def _step_attn(qkv, cache, g, window, dil, batch, n_tok):
    kvw = 2 * C_HEADS * C_HD
    past = cache.reshape(batch, window // dil, dil * kvw)
    n_res = min(dil, n_tok)
    out = pl.BlockSpec((1, n_tok, C_HEADS * C_HD), lambda b: (b, 0, 0))
    past_specs = [pl.BlockSpec((1, window // dil, kvw), functools.partial(lambda b, r: (b, 0, r), r=r))
                  for r in range(n_res)]
    return pl.pallas_call(
        functools.partial(_step_attn_body, g=g, dil=dil, n_tok=n_tok),
        grid=(batch,),
        in_specs=[pl.BlockSpec((1, n_tok, C_QKV), lambda b: (b, 0, 0))] + past_specs,
        out_specs=[out, out],
        out_shape=[jax.ShapeDtypeStruct((batch, n_tok, C_HEADS * C_HD), F32)] * 2,
        compiler_params=_cparams("parallel"),
        name=f"step_attn_d{dil}",
    )(qkv, *([past] * n_res))


def _rope_tables(pos):
    half = C_HD // 2
    inv = ROPE_THETA ** (-jnp.arange(half, dtype=F32) / half)
    ang = pos.astype(F32)[:, None] * inv[None, :]
    cos, sin = jnp.cos(ang), jnp.sin(ang)
    return jnp.concatenate([cos, cos], -1), jnp.concatenate([-sin, sin], -1)


def _row_tile(m, cap):
    return min(m, cap)


def _hgrn_layer(x, batch, seq, s0, w_in, lb, ng, w_out, ln_g, ln_b):
    m = x.shape[0]
    bn = 1024
    p = _proj(x, w_in, (), (), functools.partial(_hgrn_proj_epilogue, bn=bn), _row_tile(m, 512), bn,
              "hgrn_proj")
    if s0 is None:
        o, s_new = _hgrn_scan(p, lb, ng, batch, seq, min(seq, 512))
    else:
        rows = 8
        pp = jnp.pad(p.reshape(batch, seq, -1), ((0, 0), (0, rows - seq), (0, 0))).reshape(batch * rows, -1)
        o, s_new = _hgrn_step(pp, s0, lb, ng, batch, seq)
        o = o.reshape(batch, rows, -1)[:, :seq].reshape(m, -1)
    bm = _row_tile(m, 512)
    x = _out_ln((o,), (pl.BlockSpec((bm, o.shape[1]), lambda i: (i, 0)),), _cast_prologue,
                w_out, x, ln_g, ln_b, bm, "hgrn_out")
    return x, s_new


def _gmlp_layer(x, batch, seq, w_in, b_in, g1, b1, ws, bs, w_out, ln_g, ln_b):
    m = x.shape[0]
    bm = _row_tile(m, 512)
    vec = pl.BlockSpec((1, D_MODEL), lambda i, j: (0, 0))
    uv = _proj(x, w_in, (b_in, g1, b1),
               (pl.BlockSpec((1, D_MODEL), lambda i, j: (0, j)), vec, vec),
               _gmlp_proj_epilogue, bm, D_MODEL, "gmlp_proj")
    if seq % B_CHUNK == 0:
        chunk, ws_c, bs_t = B_CHUNK, ws, bs.T
    else:
        chunk = m
        eye = jnp.eye(batch, dtype=ws.dtype)
        ws_c = jnp.einsum("ab,hts->hatbs", eye, ws[:, :seq, :seq]).reshape(B_HEADS, m, m)
        bs_t = jnp.tile(bs[:, :seq].T, (batch, 1))
    x = _gmlp_out(uv, ws_c, bs_t, w_out, x, ln_g, ln_b, _row_tile(m, 256), chunk)
    return x, uv


def _attn_layer(x, batch, seq, caches, pos0, w_in, w_out, ln_g, ln_b):
    m = x.shape[0]
    bn = 1024
    bm = _row_tile(m, 512)
    cos, sin = _rope_tables(pos0 + jnp.arange(seq, dtype=jnp.int32))
    cos, sin = jnp.tile(cos, (batch, 1)), jnp.tile(sin, (batch, 1))
    tab = pl.BlockSpec((bm, C_HD), lambda i, j: (i, 0))
    qkv = _proj(x, w_in, (cos, sin), (tab, tab), functools.partial(_attn_proj_epilogue, bn=bn), bm, bn,
                "attn_proj")
    qkv3 = qkv.reshape(batch, seq, C_QKV)
    outs, lses = [], []
    for g, (window, dil) in enumerate(C_GROUPS):
        if caches is None:
            o, lse = _band_attn(qkv, g, dil, batch, seq)
        else:
            o, lse = _step_attn(qkv3, caches[g], g, window, dil, batch, seq)
            o, lse = o.reshape(m, -1), lse.reshape(m, -1)
        outs.append(o)
        lses.append(lse)
    bm2 = _row_tile(m, 256)
    spec = pl.BlockSpec((bm2, C_HEADS * C_HD), lambda i: (i, 0))
    x = _out_ln(tuple(outs) + tuple(lses), (spec,) * 6, _merge_prologue, w_out, x, ln_g, ln_b, bm2,
                "attn_out")
    hw = C_HEADS * C_HD
    ng = len(C_GROUPS)
    kv = []
    for g, (window, _) in enumerate(C_GROUPS):
        tail = qkv3[:, seq - min(window, seq):]
        k = tail[:, :, (ng + g) * hw:(ng + g + 1) * hw].reshape(batch, -1, C_HEADS, C_HD)
        v = tail[:, :, (2 * ng + g) * hw:(2 * ng + g + 1) * hw].reshape(batch, -1, C_HEADS, C_HD)
        kv.append(jnp.stack([k, v], axis=2))
    return x, kv


def kernel(x_prompt, x_sample, state_hgrn, cache_c_kv_w128, cache_c_kv_w512, cache_c_kv_w2048, ln_g, ln_b, a_w_in, a_lb_logits, a_norm_g, a_w_out, b_w_in, b_b_in, b_ln_g, b_ln_b, b_w_s, b_b_s, b_w_out, c_w_in, c_w_out, moe_w_group, moe_w_expert, moe_w1, moe_w3, moe_w2):
    bp, tp, _ = x_prompt.shape
    bs, ts, _ = x_sample.shape
    assert tp % 512 == 0 and tp // C_GROUPS[-1][1] >= 2 * C_QBLOCK
    assert ts <= 8 and ts <= C_GROUPS[1][1]

    lb_p = jax.nn.softmax(a_lb_logits.astype(F32), axis=0)
    lb_all = jnp.clip(jnp.cumsum(lb_p, axis=0) - lb_p[0:1], 0.0, 1.0 - 1e-6)
    caches = (cache_c_kv_w128, cache_c_kv_w512, cache_c_kv_w2048)

    xp = x_prompt.reshape(bp * tp, D_MODEL)
    xs = x_sample.reshape(bs * ts, D_MODEL)
    hgrn_p, hgrn_s, chunk_v_s = [], [], []
    kv_p = [[] for _ in C_GROUPS]
    kv_s = [[] for _ in C_GROUPS]

    for i in range(DEPTH):
        kind, j = i % 3, i // 3
        g0, b0 = ln_g[i, 0][None], ln_b[i, 0][None]
        if kind == 0:
            w_in, w_out = a_w_in[j].astype(BF16), a_w_out[j].astype(BF16)
            lb, ng = lb_all[j][None], a_norm_g[j][None]
            xp, sp = _hgrn_layer(xp, bp, tp, None, w_in, lb, ng, w_out, g0, b0)
            xs, ss = _hgrn_layer(xs, bs, ts, state_hgrn[j].astype(F32), w_in, lb, ng, w_out, g0, b0)
            hgrn_p.append(sp)
            hgrn_s.append(ss)
        elif kind == 1:
            w_in, w_out = b_w_in[j].astype(BF16), b_w_out[j].astype(BF16)
            args = (w_in, b_b_in[j][None], b_ln_g[j][None], b_ln_b[j][None], b_w_s[j], b_b_s[j], w_out, g0, b0)
            xp, _ = _gmlp_layer(xp, bp, tp, *args)
            xs, uvs = _gmlp_layer(xs, bs, ts, *args)
            chunk_v_s.append(uvs[:, D_MODEL:].reshape(bs, ts, D_MODEL))
        else:
            w_in, w_out = c_w_in[j].astype(BF16), c_w_out[j].astype(BF16)
            xp, kvp = _attn_layer(xp, bp, tp, None, 0, w_in, w_out, g0, b0)
            xs, kvs = _attn_layer(xs, bs, ts, tuple(c[j] for c in caches), PAST_LEN, w_in, w_out, g0, b0)
            for g in range(len(C_GROUPS)):
                kv_p[g].append(kvp[g])
                kv_s[g].append(kvs[g])
        wr = jnp.pad(jnp.concatenate([moe_w_expert[i], moe_w_group[i]], axis=1),
                     ((0, 0), (0, LANES - MOE_GE - MOE_GROUPS)))
        w1, w3, w2 = moe_w1[i].astype(BF16), moe_w3[i].astype(BF16), moe_w2[i].astype(BF16)
        g1, b1 = ln_g[i, 1][None], ln_b[i, 1][None]
        xp = _moe(xp, wr, w1, w3, w2, g1, b1, _row_tile(bp * tp, 512))
        xs = _moe(xs, wr, w1, w3, w2, g1, b1, bs * ts)

    return (xp.reshape(bp, tp, D_MODEL), xs.reshape(bs, ts, D_MODEL),
            jnp.stack(hgrn_p), jnp.stack(hgrn_s), jnp.stack(chunk_v_s),
            jnp.stack(kv_p[0]), jnp.stack(kv_s[0]), jnp.stack(kv_p[1]), jnp.stack(kv_s[1]),
            jnp.stack(kv_p[2]), jnp.stack(kv_s[2]))
```

```python
import functools

import jax
import jax.numpy as jnp
from jax import lax
from jax.experimental import pallas as pl
from jax.experimental.pallas import tpu as pltpu

F32 = jnp.float32
BF16 = jnp.bfloat16
HIGHEST = lax.Precision.HIGHEST

D_MODEL = 2048
DEPTH = 4
PAST_LEN = 16384
A_HEADS = 16
A_DK = 128
A_DV = 128
A_WIDTH = A_HEADS * A_DK
HGRN_CHUNK = 64
HGRN_SUB = 16
HGRN_SUPER = 256
B_CHUNK = 128
B_HEADS = 16
B_HD = 128
C_HEADS = 8
C_HD = 128
C_GROUPS = ((128, 1), (512, 4), (2048, 16))
C_KEYS = 129
C_QBLOCK = 128
C_QKV = 3 * len(C_GROUPS) * C_HEADS * C_HD
ROPE_THETA = 10000.0
MOE_GROUPS = 4
MOE_EXPERTS = 4
MOE_GE = MOE_GROUPS * MOE_EXPERTS
MOE_FF = 256
MOE_TILE = 256
LN_EPS = 1e-5
RMS_EPS = 1e-6
ALPHA = (2 * DEPTH) ** 0.25
LANES = 128
VMEM_LIMIT = 56 * 1024 * 1024


def _cparams(*sem):
    return pltpu.CompilerParams(dimension_semantics=sem, vmem_limit_bytes=VMEM_LIMIT)


def _sigmoid(x):
    return 1.0 / (1.0 + jnp.exp(-x))


def _ln_rows(y, g, b):
    mu = jnp.mean(y, axis=-1, keepdims=True)
    d = y - mu
    var = jnp.mean(d * d, axis=-1, keepdims=True)
    return d * lax.rsqrt(var + LN_EPS) * g + b


def _nt_dot(a, b):
    return lax.dot_general(a, b, (((1,), (1,)), ((), ())), preferred_element_type=F32)


def _tn_dot(a, b):
    return lax.dot_general(a, b, (((0,), (0,)), ((), ())), preferred_element_type=F32)


def _proj_body(x_ref, w_ref, *rest, epilogue, n_extra):
    extras = rest[:n_extra]
    o_ref = rest[n_extra]
    xb_ref = rest[n_extra + 1]
    j = pl.program_id(1)

    @pl.when(j == 0)
    def _cast():
        xb_ref[...] = x_ref[...].astype(BF16)

    acc = jnp.dot(xb_ref[...], w_ref[...], preferred_element_type=F32)
    epilogue(acc, j, extras, o_ref)


def _proj(x, w, extras, extra_specs, epilogue, bm, bn, name):
    m, k = x.shape
    n = w.shape[1]
    return pl.pallas_call(
        functools.partial(_proj_body, epilogue=epilogue, n_extra=len(extras)),
        grid=(m // bm, n // bn),
        in_specs=[pl.BlockSpec((bm, k), lambda i, j: (i, 0)),
                  pl.BlockSpec((k, bn), lambda i, j: (0, j))] + list(extra_specs),
        out_specs=pl.BlockSpec((bm, bn), lambda i, j: (i, j)),
        out_shape=jax.ShapeDtypeStruct((m, n), F32),
        scratch_shapes=[pltpu.VMEM((bm, k), BF16)],
        compiler_params=_cparams("parallel", "arbitrary"),
        name=name,
    )(x, w, *extras)


def _hgrn_proj_epilogue(acc, j, extras, o_ref, *, bn):
    nq = A_WIDTH // bn
    is_silu = jnp.logical_or(j < nq, j >= 3 * nq)

    @pl.when(is_silu)
    def _():
        o_ref[...] = acc * _sigmoid(acc)

    @pl.when(jnp.logical_not(is_silu))
    def _():
        o_ref[...] = acc


def _gelu_tanh(z):
    return 0.5 * z * (1.0 + jnp.tanh(0.7978845608028654 * (z + 0.044715 * (z * z * z))))


def _gmlp_proj_epilogue(acc, j, extras, o_ref):
    bias_ref, g_ref, b_ref = extras
    z = _gelu_tanh(acc + bias_ref[...])

    @pl.when(j == 0)
    def _():
        o_ref[...] = z

    @pl.when(j == 1)
    def _():
        o_ref[...] = _ln_rows(z, g_ref[...], b_ref[...])


def _attn_proj_epilogue(acc, j, extras, o_ref, *, bn):
    cos_ref, sin_ref = extras
    n_rot = 2 * len(C_GROUPS) * C_HEADS * C_HD // bn

    @pl.when(j < n_rot)
    def _():
        cos = cos_ref[...]
        sin = sin_ref[...]
        for h in range(bn // C_HD):
            xh = acc[:, h * C_HD:(h + 1) * C_HD]
            o_ref[:, h * C_HD:(h + 1) * C_HD] = xh * cos + pltpu.roll(xh, C_HD // 2, 1) * sin

    @pl.when(j >= n_rot)
    def _():
        o_ref[...] = acc


def _out_ln_body(*refs, prologue, n_in):
    ins = refs[:n_in]
    w_ref, r_ref, g_ref, b_ref, o_ref = refs[n_in:n_in + 5]
    a = prologue(*ins)
    acc = jnp.dot(a, w_ref[...], preferred_element_type=F32)
    y = ALPHA * r_ref[...] + acc
    o_ref[...] = _ln_rows(y, g_ref[...], b_ref[...])


def _out_ln(ins, in_specs, prologue, w, resid, g, b, bm, name):
    m = resid.shape[0]
    k = w.shape[0]
    row = pl.BlockSpec((bm, D_MODEL), lambda i: (i, 0))
    vec = pl.BlockSpec((1, D_MODEL), lambda i: (0, 0))
    return pl.pallas_call(
        functools.partial(_out_ln_body, prologue=prologue, n_in=len(ins)),
        grid=(m // bm,),
        in_specs=list(in_specs) + [pl.BlockSpec((k, D_MODEL), lambda i: (0, 0)), row, vec, vec],
        out_specs=row,
        out_shape=jax.ShapeDtypeStruct((m, D_MODEL), F32),
        compiler_params=_cparams("parallel"),
        name=name,
    )(*ins, w, resid, g, b)


def _cast_prologue(a_ref):
    return a_ref[...].astype(BF16)


def _merge_prologue(o0, o1, o2, l0, l1, l2):
    a0, a1, a2 = l0[...], l1[...], l2[...]
    mx = jnp.maximum(jnp.maximum(a0, a1), a2)
    e0, e1, e2 = jnp.exp(a0 - mx), jnp.exp(a1 - mx), jnp.exp(a2 - mx)
    o = (e0 * o0[...] + e1 * o1[...] + e2 * o2[...]) / (e0 + e1 + e2)
    return o.astype(BF16)


def _moe_gate(logits):
    lane = lax.broadcasted_iota(jnp.int32, logits.shape, 1).astype(F32)
    neg = -jnp.inf
    big = 4.0 * LANES
    gl = jnp.where((lane >= MOE_GE) & (lane < MOE_GE + MOE_GROUPS), logits, neg)
    gmax = jnp.max(gl, axis=-1, keepdims=True)
    g_idx = jnp.min(jnp.where(gl == gmax, lane - MOE_GE, big), axis=-1, keepdims=True)
    g_top = 1.0 / jnp.sum(jnp.exp(gl - gmax), axis=-1, keepdims=True)
    lo = g_idx * MOE_EXPERTS
    el = jnp.where((lane >= lo) & (lane < lo + MOE_EXPERTS), logits, neg)
    m1 = jnp.max(el, axis=-1, keepdims=True)
    i1 = jnp.min(jnp.where(el == m1, lane, big), axis=-1, keepdims=True)
    el2 = jnp.where(lane == i1, neg, el)
    m2 = jnp.max(el2, axis=-1, keepdims=True)
    i2 = jnp.min(jnp.where(el2 == m2, lane, big), axis=-1, keepdims=True)
    r = jnp.exp(m2 - m1)
    w1 = g_top / (1.0 + r)
    w2 = w1 * r
    return jnp.where(lane == i1, w1, 0.0) + jnp.where(lane == i2, w2, 0.0), g_idx


def _gate_columns(gate, first):
    lane = lax.broadcasted_iota(jnp.int32, gate.shape, 1)
    return jnp.concatenate(
        [jnp.broadcast_to(jnp.sum(jnp.where(lane == first + e, gate, 0.0), axis=-1, keepdims=True),
                          (gate.shape[0], MOE_FF)) for e in range(MOE_EXPERTS)], axis=1)


def _group_ffn(xb, gate, first, w1_ref, w3_ref, w2_ref):
    h1 = jnp.dot(xb, w1_ref[0], preferred_element_type=F32)
    h3 = jnp.dot(xb, w3_ref[0], preferred_element_type=F32)
    hg = (h1 * _sigmoid(h1) * h3 * _gate_columns(gate, first)).astype(BF16)
    return jnp.dot(hg, w2_ref[0], preferred_element_type=F32)


def _moe_body(x_ref, wr_ref, w1_ref, w3_ref, w2_ref, g_ref, b_ref, o_ref, xb_ref, gate_ref, acc_ref):
    gi = pl.program_id(1)

    @pl.when(gi == 0)
    def _route():
        x = x_ref[...]
        xb_ref[...] = x.astype(BF16)
        logits = jnp.dot(x, wr_ref[...], precision=HIGHEST, preferred_element_type=F32)
        gate_ref[...] = _moe_gate(logits)[0]
        acc_ref[...] = jnp.zeros_like(acc_ref)

    acc_ref[...] += _group_ffn(xb_ref[...], gate_ref[...], gi * MOE_EXPERTS, w1_ref, w3_ref, w2_ref)

    @pl.when(gi == MOE_GROUPS - 1)
    def _finish():
        y = ALPHA * x_ref[...] + acc_ref[...]
        o_ref[...] = _ln_rows(y, g_ref[...], b_ref[...])


def _moe(x, wr, w1g, w3g, w2g, g, b, bm):
    m = x.shape[0]
    eff = MOE_EXPERTS * MOE_FF
    row = pl.BlockSpec((bm, D_MODEL), lambda i, e: (i, 0))
    vec = pl.BlockSpec((1, D_MODEL), lambda i, e: (0, 0))
    return pl.pallas_call(
        _moe_body,
        grid=(m // bm, MOE_GROUPS),
        in_specs=[row,
                  pl.BlockSpec((D_MODEL, LANES), lambda i, e: (0, 0)),
                  pl.BlockSpec((1, D_MODEL, eff), lambda i, e: (e, 0, 0)),
                  pl.BlockSpec((1, D_MODEL, eff), lambda i, e: (e, 0, 0)),
                  pl.BlockSpec((1, eff, D_MODEL), lambda i, e: (e, 0, 0)),
                  vec, vec],
        out_specs=row,
        out_shape=jax.ShapeDtypeStruct((m, D_MODEL), F32),
        scratch_shapes=[pltpu.VMEM((bm, D_MODEL), BF16),
                        pltpu.VMEM((bm, LANES), F32),
                        pltpu.VMEM((bm, D_MODEL), F32)],
        compiler_params=_cparams("parallel", "arbitrary"),
        name="moe",
    )(x, wr, w1g, w3g, w2g, g, b)


def _router_body(x_ref, wr_ref, gate_ref, gidx_ref):
    logits = jnp.dot(x_ref[...], wr_ref[...], precision=HIGHEST, preferred_element_type=F32)
    gate, g_idx = _moe_gate(logits)
    gate_ref[...] = gate
    gidx_ref[...] = jnp.broadcast_to(g_idx, gidx_ref.shape)


def _router(x, wr, bm):
    m = x.shape[0]
    out = pl.BlockSpec((bm, LANES), lambda i: (i, 0))
    return pl.pallas_call(
        _router_body,
        grid=(m // bm,),
        in_specs=[pl.BlockSpec((bm, D_MODEL), lambda i: (i, 0)),
                  pl.BlockSpec((D_MODEL, LANES), lambda i: (0, 0))],
        out_specs=[out, out],
        out_shape=[jax.ShapeDtypeStruct((m, LANES), F32)] * 2,
        compiler_params=_cparams("parallel"),
        name="moe_router",
    )(x, wr)


def _moe_sorted_body(tg_ref, tv_ref, x_ref, gate_ref, w1_ref, w3_ref, w2_ref, g_ref, b_ref, o_ref):
    i = pl.program_id(0)

    @pl.when(tv_ref[i] == 0)
    def _unused():
        o_ref[...] = jnp.zeros_like(o_ref)

    @pl.when(tv_ref[i] != 0)
    def _tile():
        x = x_ref[...]
        ffn = _group_ffn(x.astype(BF16), gate_ref[...], tg_ref[i] * MOE_EXPERTS, w1_ref, w3_ref, w2_ref)
        o_ref[...] = _ln_rows(ALPHA * x + ffn, g_ref[...], b_ref[...])


def _moe_sorted(tile_group, tile_valid, x, gate, w1g, w3g, w2g, g, b, bm):
    m = x.shape[0]
    eff = MOE_EXPERTS * MOE_FF
    row = pl.BlockSpec((bm, D_MODEL), lambda i, tg, tv: (i, 0))
    vec = pl.BlockSpec((1, D_MODEL), lambda i, tg, tv: (0, 0))
    return pl.pallas_call(
        _moe_sorted_body,
        grid_spec=pltpu.PrefetchScalarGridSpec(
            num_scalar_prefetch=2,
            grid=(m // bm,),
            in_specs=[row,
                      pl.BlockSpec((bm, LANES), lambda i, tg, tv: (i, 0)),
                      pl.BlockSpec((1, D_MODEL, eff), lambda i, tg, tv: (tg[i], 0, 0)),
                      pl.BlockSpec((1, D_MODEL, eff), lambda i, tg, tv: (tg[i], 0, 0)),
                      pl.BlockSpec((1, eff, D_MODEL), lambda i, tg, tv: (tg[i], 0, 0)),
                      vec, vec],
            out_specs=row),
        out_shape=jax.ShapeDtypeStruct((m, D_MODEL), F32),
        compiler_params=_cparams("arbitrary"),
        name="moe_sorted",
    )(tile_group, tile_valid, x, gate, w1g, w3g, w2g, g, b)


def _group_up_weights(w):
    w = w.astype(BF16).reshape(MOE_GROUPS, MOE_EXPERTS, D_MODEL, MOE_FF)
    return w.transpose(0, 2, 1, 3).reshape(MOE_GROUPS, D_MODEL, MOE_EXPERTS * MOE_FF)


def _moe_dispatch(x, wr, w1g, w3g, w2g, g, b, bm):
    n = x.shape[0]
    gate, gidx = _router(x, wr, _row_tile(n, 512))
    grp = gidx[:, 0].astype(jnp.int32)
    onehot = (grp[:, None] == jnp.arange(MOE_GROUPS, dtype=jnp.int32)[None, :]).astype(jnp.int32)
    counts = jnp.sum(onehot, axis=0)
    padded = (counts + bm - 1) // bm * bm
    ends = jnp.cumsum(padded)
    starts = ends - padded
    rank = jnp.cumsum(onehot, axis=0) - onehot
    dest = jnp.sum((starts[None, :] + rank) * onehot, axis=1)
    n_tiles = n // bm + MOE_GROUPS
    src = jnp.zeros((n_tiles * bm,), jnp.int32).at[dest].set(jnp.arange(n, dtype=jnp.int32))
    tile_start = jnp.arange(n_tiles, dtype=jnp.int32) * bm
    tile_group = jnp.minimum(jnp.searchsorted(ends, tile_start, side="right"), MOE_GROUPS - 1).astype(jnp.int32)
    tile_valid = (tile_start < ends[-1]).astype(jnp.int32)
    ys = _moe_sorted(tile_group, tile_valid, jnp.take(x, src, axis=0), jnp.take(gate, src, axis=0),
                     w1g, w3g, w2g, g, b, bm)
    return jnp.take(ys, dest, axis=0)


def _hgrn_gates(f, lb):
    log_sig = jnp.minimum(f, 0.0) - jnp.log1p(jnp.exp(-jnp.abs(f)))
    a = jnp.log1p(-lb) + log_sig
    log_lb = jnp.log(lb)
    log_f = jnp.maximum(log_lb, a) + jnp.log1p(jnp.exp(-jnp.abs(log_lb - a)))
    k = (1.0 - lb) / (1.0 + jnp.exp(f))
    return log_f, k


def _hgrn_finish(o, gate, ng):
    o = o * lax.rsqrt(jnp.mean(o * o, axis=-1, keepdims=True) + RMS_EPS) * ng
    return o * gate


def _rows_of(x, idx, n):
    return jnp.concatenate([jnp.broadcast_to(x[i:i + 1, :], (n, x.shape[1])) for i in idx], axis=0)


def _hgrn_scan_body(q_ref, f_ref, v_ref, g_ref, lb_ref, ng_ref, tri_ref, o_ref, s_ref, st_ref, *, n_super):
    C, c, N = HGRN_CHUNK, HGRN_SUB, HGRN_SUPER
    half = C // 2
    neg = -1e30
    tb = pl.program_id(2)

    @pl.when(tb == 0)
    def _init():
        st_ref[...] = jnp.zeros_like(st_ref)

    lb = lb_ref[...]
    ng = ng_ref[...]
    row = lax.broadcasted_iota(jnp.int32, (N, 1), 0)
    hi32 = (row & (C - 1)) >= half
    hi16 = (row & (half - 1)) >= c
    lane = lax.broadcasted_iota(jnp.int32, (8, LANES), 1)
    srow = lax.broadcasted_iota(jnp.int32, (8, 1), 0)

    def super_chunk(si, carry):
        r0 = pl.multiple_of(si * N, N)
        q = q_ref[pl.ds(r0, N), :]
        v = v_ref[pl.ds(r0, N), :].astype(BF16)
        gate = g_ref[pl.ds(r0, N), :]
        log_f, k = _hgrn_gates(f_ref[pl.ds(r0, N), :], lb)
        b = jnp.dot(tri_ref[...], log_f, precision=HIGHEST, preferred_element_type=F32)

        ra = _rows_of(b, [C * m + half - 1 for m in range(N // C)], C)
        qa = jnp.where(hi32, q * jnp.exp(jnp.where(hi32, b - ra, 0.0)), 0.0)
        ka = jnp.where(hi32, 0.0, k * jnp.exp(jnp.where(hi32, 0.0, ra - b)))
        rb = _rows_of(b, [half * m + c - 1 for m in range(N // half)], half)
        qb = jnp.where(hi16, q * jnp.exp(jnp.where(hi16, b - rb, 0.0)), 0.0)
        kb = jnp.where(hi16, 0.0, k * jnp.exp(jnp.where(hi16, 0.0, rb - b)))
        s2 = lax.broadcasted_iota(jnp.int32, (N, N), 0)
        t2 = lax.broadcasted_iota(jnp.int32, (N, N), 1)
        same64 = jnp.right_shift(s2, 6) == jnp.right_shift(t2, 6)
        same32 = jnp.right_shift(s2, 5) == jnp.right_shift(t2, 5)
        att_t = (jnp.where(same64, _nt_dot(ka.astype(BF16), qa.astype(BF16)), 0.0)
                 + jnp.where(same32, _nt_dot(kb.astype(BF16), qb.astype(BF16)), 0.0))

        pieces = [jnp.zeros((8, LANES), F32)] * (N // 8)
        for i in range(N // c):
            base = c * i
            ks = (k[base:base + 8, :], k[base + 8:base + c, :])
            bs = (b[base:base + 8, :], b[base + 8:base + c, :])
            for tl in range(c):
                qt = q[base + tl:base + tl + 1, :]
                bt = b[base + tl:base + tl + 1, :]
                for u in range(tl // 8 + 1):
                    d = bt - bs[u]
                    if u == tl // 8:
                        d = jnp.where(srow <= tl - 8 * u, d, neg)
                    col = jnp.sum(qt * ks[u] * jnp.exp(d), axis=-1, keepdims=True)
                    pieces[2 * i + u] = jnp.where(lane == base % LANES + tl, col, pieces[2 * i + u])
        zero = jnp.zeros((N // 2, LANES), F32)
        n_lo = N // 16
        diag = jnp.concatenate(
            [jnp.concatenate(pieces[:n_lo] + [zero], axis=0),
             jnp.concatenate([zero] + pieces[n_lo:], axis=0)], axis=1)
        o_intra = _tn_dot((att_t + diag).astype(BF16), v)

        bl = _rows_of(b, [C * m + C - 1 for m in range(N // C)], C)
        qe = (q * jnp.exp(b)).astype(BF16)
        kd = (k * jnp.exp(bl - b)).astype(BF16)
        st = st_ref[...]
        for m in range(N // C):
            rows = slice(C * m, C * m + C)
            o = o_intra[rows, :] + _nt_dot(qe[rows, :], st.astype(BF16))
            o_ref[pl.ds(r0 + C * m, C), :] = _hgrn_finish(o, gate[rows, :], ng).astype(o_ref.dtype)
            st = st * jnp.exp(b[C * m + C - 1:C * m + C, :]) + _tn_dot(v[rows, :], kd[rows, :])
        st_ref[...] = st
        return carry

    lax.fori_loop(0, n_super, super_chunk, 0)

    @pl.when(tb == pl.num_programs(2) - 1)
    def _emit():
        s_ref[0, 0] = st_ref[...].T


def _hgrn_scan(p, lb, ng, batch, seq, tb):
    nt = seq // tb
    nh = A_HEADS
    n = HGRN_SUPER
    idx = jnp.arange(n)
    tri = ((idx[:, None] >= idx[None, :])
           & (idx[:, None] // HGRN_CHUNK == idx[None, :] // HGRN_CHUNK)).astype(F32)

    def col(off):
        return pl.BlockSpec((tb, A_DK), lambda b, h, t: (b * nt + t, off * nh + h))

    return pl.pallas_call(
        functools.partial(_hgrn_scan_body, n_super=tb // n),
        grid=(batch, nh, nt),
        in_specs=[col(0), col(1), col(2), col(3),
                  pl.BlockSpec((1, A_DK), lambda b, h, t: (0, h)),
                  pl.BlockSpec((1, A_DV), lambda b, h, t: (0, 0)),
                  pl.BlockSpec((n, n), lambda b, h, t: (0, 0))],
        out_specs=[pl.BlockSpec((tb, A_DV), lambda b, h, t: (b * nt + t, h)),
                   pl.BlockSpec((1, 1, A_DK, A_DV), lambda b, h, t: (b, h, 0, 0))],
        out_shape=[jax.ShapeDtypeStruct((batch * seq, nh * A_DV), BF16),
                   jax.ShapeDtypeStruct((batch, nh, A_DK, A_DV), F32)],
        scratch_shapes=[pltpu.VMEM((A_DV, A_DK), F32)],
        compiler_params=_cparams("parallel", "parallel", "arbitrary"),
        name="hgrn_scan",
    )(p, p, p, p, lb, ng, tri)


def _hgrn_step_body(p_ref, s0_ref, lb_ref, ng_ref, o_ref, s_ref, *, n_tok):
    R = p_ref.shape[0]
    row = lax.broadcasted_iota(jnp.int32, (R, 1), 0)
    valid = row < n_tok
    r2 = lax.broadcasted_iota(jnp.int32, (R, R), 0)
    c2 = lax.broadcasted_iota(jnp.int32, (R, R), 1)
    tril = (r2 >= c2).astype(F32)
    ng = ng_ref[...]
    for h in range(A_HEADS):
        sl = slice(h * A_DK, (h + 1) * A_DK)
        q = p_ref[:, sl]
        v = p_ref[:, 2 * A_WIDTH + h * A_DV:2 * A_WIDTH + (h + 1) * A_DV]
        gate = p_ref[:, 3 * A_WIDTH + h * A_DV:3 * A_WIDTH + (h + 1) * A_DV]
        log_f, k = _hgrn_gates(p_ref[:, A_WIDTH + h * A_DK:A_WIDTH + (h + 1) * A_DK], lb_ref[:, sl])
        b = jnp.dot(tril, log_f, precision=HIGHEST, preferred_element_type=F32)
        st = s0_ref[0, h].T
        o = _nt_dot((q * jnp.exp(b)).astype(BF16), st.astype(BF16))
        for s in range(n_tok):
            m = row >= s
            w = jnp.where(m, q * k[s:s + 1, :] * jnp.exp(jnp.where(m, b - b[s:s + 1, :], 0.0)), 0.0)
            o = o + jnp.sum(w, axis=-1, keepdims=True) * v[s:s + 1, :]
        o_ref[:, h * A_DV:(h + 1) * A_DV] = _hgrn_finish(o, gate, ng)
        bl = b[n_tok - 1:n_tok, :]
        kd = jnp.where(valid, k * jnp.exp(jnp.where(valid, bl - b, 0.0)), 0.0)
        st_new = st * jnp.exp(bl) + _tn_dot(v.astype(BF16), kd.astype(BF16))
        s_ref[0, h] = st_new.T


def _hgrn_step(p, s0, lb, ng, batch, n_tok):
    rows = p.shape[0] // batch
    return pl.pallas_call(
        functools.partial(_hgrn_step_body, n_tok=n_tok),
        grid=(batch,),
        in_specs=[pl.BlockSpec((rows, 4 * A_WIDTH), lambda b: (b, 0)),
                  pl.BlockSpec((1, A_HEADS, A_DK, A_DV), lambda b: (b, 0, 0, 0)),
                  pl.BlockSpec((1, A_WIDTH), lambda b: (0, 0)),
                  pl.BlockSpec((1, A_DV), lambda b: (0, 0))],
        out_specs=[pl.BlockSpec((rows, A_HEADS * A_DV), lambda b: (b, 0)),
                   pl.BlockSpec((1, A_HEADS, A_DK, A_DV), lambda b: (b, 0, 0, 0))],
        out_shape=[jax.ShapeDtypeStruct((batch * rows, A_HEADS * A_DV), F32),
                   jax.ShapeDtypeStruct((batch, A_HEADS, A_DK, A_DV), F32)],
        compiler_params=_cparams("parallel"),
        name="hgrn_step",
    )(p, s0, lb, ng)


def _gmlp_prologue(u_ref, v_ref, ws_ref, bs_ref, gated_ref, *, chunk):
    bm = u_ref.shape[0]
    r2 = lax.broadcasted_iota(jnp.int32, (chunk, chunk), 0)
    c2 = lax.broadcasted_iota(jnp.int32, (chunk, chunk), 1)
    causal = r2 >= c2
    for h in range(B_HEADS):
        wc = jnp.where(causal, ws_ref[h], 0.0).astype(BF16)
        bias = bs_ref[:, h:h + 1]
        cols = slice(h * B_HD, (h + 1) * B_HD)
        for n in range(bm // chunk):
            rows = slice(n * chunk, (n + 1) * chunk)
            mixed = jnp.dot(wc, v_ref[rows, cols].astype(BF16), preferred_element_type=F32) + bias
            gated_ref[rows, cols] = (u_ref[rows, cols] * mixed).astype(BF16)
    return gated_ref[...]


def _gmlp_out_body(u_ref, v_ref, ws_ref, bs_ref, w_ref, r_ref, g_ref, b_ref, o_ref, gated_ref, *, chunk):
    a = _gmlp_prologue(u_ref, v_ref, ws_ref, bs_ref, gated_ref, chunk=chunk)
    acc = jnp.dot(a, w_ref[...], preferred_element_type=F32)
    y = ALPHA * r_ref[...] + acc
    o_ref[...] = _ln_rows(y, g_ref[...], b_ref[...])


def _gmlp_out(uv, ws, bs_t, w, resid, g, b, bm, chunk):
    m = resid.shape[0]
    row = pl.BlockSpec((bm, D_MODEL), lambda i: (i, 0))
    vec = pl.BlockSpec((1, D_MODEL), lambda i: (0, 0))
    return pl.pallas_call(
        functools.partial(_gmlp_out_body, chunk=chunk),
        grid=(m // bm,),
        in_specs=[pl.BlockSpec((bm, D_MODEL), lambda i: (i, 0)),
                  pl.BlockSpec((bm, D_MODEL), lambda i: (i, 1)),
                  pl.BlockSpec(ws.shape, lambda i: (0, 0, 0)),
                  pl.BlockSpec(bs_t.shape, lambda i: (0, 0)),
                  pl.BlockSpec((D_MODEL, D_MODEL), lambda i: (0, 0)),
                  row, vec, vec],
        out_specs=row,
        out_shape=jax.ShapeDtypeStruct((m, D_MODEL), F32),
        scratch_shapes=[pltpu.VMEM((bm, D_MODEL), BF16)],
        compiler_params=_cparams("parallel"),
        name="gmlp_out",
    )(uv, uv, ws, bs_t, w, resid, g, b)


def _band_attn_body(q_ref, k_ref, v_ref, o_ref, l_ref, *, dil, seq):
    bq = C_QBLOCK
    span = C_KEYS - 1
    scale = C_HD ** -0.5
    n_blocks = seq // dil // bq
    qi = lax.broadcasted_iota(jnp.int32, (bq, 2 * bq), 0)
    ki = lax.broadcasted_iota(jnp.int32, (bq, 2 * bq), 1)

    def rows(first, n):
        return pl.ds(first, n) if dil == 1 else pl.ds(first, n, stride=dil)

    for r in range(dil):
        for i in range(n_blocks):
            w = max(i - 1, 0)
            qs = rows(r + dil * bq * i, bq)
            ws = rows(r + dil * bq * w, 2 * bq)
            q = q_ref[qs, :].astype(BF16)
            kw = k_ref[ws, :].astype(BF16)
            vw = v_ref[ws, :].astype(BF16)
            s = _nt_dot(q, kw) * scale
            rel = bq * (i - w) + qi - ki
            s = jnp.where((rel >= 0) & (rel <= span), s, -jnp.inf)
            mx = jnp.max(s, axis=-1, keepdims=True)
            p = jnp.exp(s - mx)
            den = jnp.sum(p, axis=-1, keepdims=True)
            o_ref[qs, :] = jnp.dot(p.astype(BF16), vw, preferred_element_type=F32) / den
            l_ref[qs, :] = jnp.broadcast_to(mx + jnp.log(den), (bq, C_HD))


def _band_attn(qkv, g, dil, batch, seq):
    ng = len(C_GROUPS)

    def col(part):
        return pl.BlockSpec((seq, C_HD), lambda b, h: (b, (part * ng + g) * C_HEADS + h))

    out = pl.BlockSpec((seq, C_HD), lambda b, h: (b, h))
    return pl.pallas_call(
        functools.partial(_band_attn_body, dil=dil, seq=seq),
        grid=(batch, C_HEADS),
        in_specs=[col(0), col(1), col(2)],
        out_specs=[out, out],
        out_shape=[jax.ShapeDtypeStruct((batch * seq, C_HEADS * C_HD), F32)] * 2,
        compiler_params=_cparams("parallel", "parallel"),
        name=f"band_attn_d{dil}",
    )(qkv, qkv, qkv)


def _step_attn_body(qkv_ref, *refs, g, dil, n_tok):
    past_refs, (o_ref, l_ref) = refs[:-2], refs[-2:]
    scale = C_HD ** -0.5
    ng = len(C_GROUPS)
    hs = C_HEADS
    tok = lax.broadcasted_iota(jnp.int32, (n_tok, 1, 1), 0)
    prow = lax.broadcasted_iota(jnp.int32, (past_refs[0].shape[1], 1, 1), 0)
    kn = qkv_ref[0, :, (ng + g) * hs:(ng + g + 1) * hs, :]
    vn = qkv_ref[0, :, (2 * ng + g) * hs:(2 * ng + g + 1) * hs, :]
    for t in range(n_tok):
        past_ref = past_refs[t % dil]
        new_ok = (tok <= t) & (((t - tok) % dil) == 0)
        past_ok = prow >= (t if dil == 1 else 0)
        q = qkv_ref[0, t, g * hs:(g + 1) * hs, :][None]
        kp = past_ref[0, :, 0:hs, :]
        vp = past_ref[0, :, hs:2 * hs, :]
        sp = jnp.where(past_ok, jnp.sum(kp * q, axis=-1, keepdims=True) * scale, -jnp.inf)
        sn = jnp.where(new_ok, jnp.sum(kn * q, axis=-1, keepdims=True) * scale, -jnp.inf)
        mx = jnp.maximum(jnp.max(sp, axis=0, keepdims=True), jnp.max(sn, axis=0, keepdims=True))
        pp = jnp.exp(sp - mx)
        pn = jnp.exp(sn - mx)
        den = jnp.sum(pp, axis=0, keepdims=True) + jnp.sum(pn, axis=0, keepdims=True)
        o = (jnp.sum(pp * vp, axis=0, keepdims=True) + jnp.sum(pn * vn, axis=0, keepdims=True)) / den
        o_ref[0, t] = o[0]
        l_ref[0, t] = jnp.broadcast_to(mx + jnp.log(den), (1, hs, C_HD))[0]


def _step_attn(qkv, cache, g, window, dil, batch, n_tok):
    past = cache.reshape(batch, window // dil, dil, 2 * C_HEADS, C_HD)
    n_res = min(dil, n_tok)
    out = pl.BlockSpec((1, n_tok, C_HEADS, C_HD), lambda b: (b, 0, 0, 0))
    past_specs = [pl.BlockSpec((1, window // dil, None, 2 * C_HEADS, C_HD),
                               functools.partial(lambda b, r: (b, 0, r, 0, 0), r=r)) for r in range(n_res)]
    return pl.pallas_call(
        functools.partial(_step_attn_body, g=g, dil=dil, n_tok=n_tok),
        grid=(batch,),
        in_specs=[pl.BlockSpec((1, n_tok) + qkv.shape[2:], lambda b: (b, 0, 0, 0))] + past_specs,
        out_specs=[out, out],
        out_shape=[jax.ShapeDtypeStruct((batch, n_tok, C_HEADS, C_HD), F32)] * 2,
        compiler_params=_cparams("parallel"),
        name=f"step_attn_d{dil}",
    )(qkv, *([past] * n_res))


def _rope_tables(pos):
    half = C_HD // 2
    inv = ROPE_THETA ** (-jnp.arange(half, dtype=F32) / half)
    ang = pos.astype(F32)[:, None] * inv[None, :]
    cos, sin = jnp.cos(ang), jnp.sin(ang)
    return jnp.concatenate([cos, cos], -1), jnp.concatenate([-sin, sin], -1)


def _row_tile(m, cap):
    return min(m, cap)


def _hgrn_layer(x, batch, seq, s0, w_in, lb, ng, w_out, ln_g, ln_b):
    m = x.shape[0]
    bn = 1024
    p = _proj(x, w_in, (), (), functools.partial(_hgrn_proj_epilogue, bn=bn), _row_tile(m, 512), bn,
              "hgrn_proj")
    if s0 is None:
        o, s_new = _hgrn_scan(p, lb, ng, batch, seq, min(seq, 512))
    else:
        rows = 8
        pp = jnp.pad(p.reshape(batch, seq, -1), ((0, 0), (0, rows - seq), (0, 0))).reshape(batch * rows, -1)
        o, s_new = _hgrn_step(pp, s0, lb, ng, batch, seq)
        o = o.reshape(batch, rows, -1)[:, :seq].reshape(m, -1)
    bm = _row_tile(m, 512)
    x = _out_ln((o,), (pl.BlockSpec((bm, o.shape[1]), lambda i: (i, 0)),), _cast_prologue,
                w_out, x, ln_g, ln_b, bm, "hgrn_out")
    return x, s_new


def _gmlp_layer(x, batch, seq, w_in, b_in, g1, b1, ws, bs, w_out, ln_g, ln_b):
    m = x.shape[0]
    bm = _row_tile(m, 512)
    vec = pl.BlockSpec((1, D_MODEL), lambda i, j: (0, 0))
    uv = _proj(x, w_in, (b_in, g1, b1),
               (pl.BlockSpec((1, D_MODEL), lambda i, j: (0, j)), vec, vec),
               _gmlp_proj_epilogue, bm, D_MODEL, "gmlp_proj")
    if seq % B_CHUNK == 0:
        chunk, ws_c, bs_t = B_CHUNK, ws, bs.T
    else:
        chunk = m
        eye = jnp.eye(batch, dtype=ws.dtype)
        ws_c = jnp.einsum("ab,hts->hatbs", eye, ws[:, :seq, :seq]).reshape(B_HEADS, m, m)
        bs_t = jnp.tile(bs[:, :seq].T, (batch, 1))
    x = _gmlp_out(uv, ws_c, bs_t, w_out, x, ln_g, ln_b, _row_tile(m, 256), chunk)
    return x, uv


def _attn_layer(x, batch, seq, caches, pos0, w_in, w_out, ln_g, ln_b):
    m = x.shape[0]
    bn = 1024
    bm = _row_tile(m, 512)
    cos, sin = _rope_tables(pos0 + jnp.arange(seq, dtype=jnp.int32))
    cos, sin = jnp.tile(cos, (batch, 1)), jnp.tile(sin, (batch, 1))
    tab = pl.BlockSpec((bm, C_HD), lambda i, j: (i, 0))
    qkv = _proj(x, w_in, (cos, sin), (tab, tab), functools.partial(_attn_proj_epilogue, bn=bn), bm, bn,
                "attn_proj")
    qkv3 = qkv.reshape(batch, seq, C_QKV)
    outs, lses = [], []
    for g, (window, dil) in enumerate(C_GROUPS):
        if caches is None:
            o, lse = _band_attn(qkv, g, dil, batch, seq)
        else:
            o, lse = _step_attn(qkv.reshape(batch, seq, C_QKV // C_HD, C_HD), caches[g], g, window, dil,
                                batch, seq)
            o, lse = o.reshape(m, -1), lse.reshape(m, -1)
        outs.append(o)
        lses.append(lse)
    bm2 = _row_tile(m, 256)
    spec = pl.BlockSpec((bm2, C_HEADS * C_HD), lambda i: (i, 0))
    x = _out_ln(tuple(outs) + tuple(lses), (spec,) * 6, _merge_prologue, w_out, x, ln_g, ln_b, bm2,
                "attn_out")
    hw = C_HEADS * C_HD
    ng = len(C_GROUPS)
    kv = []
    for g, (window, _) in enumerate(C_GROUPS):
        tail = qkv3[:, seq - min(window, seq):]
        k = tail[:, :, (ng + g) * hw:(ng + g + 1) * hw].reshape(batch, -1, C_HEADS, C_HD)
        v = tail[:, :, (2 * ng + g) * hw:(2 * ng + g + 1) * hw].reshape(batch, -1, C_HEADS, C_HD)
        kv.append(jnp.stack([k, v], axis=2))
    return x, kv


def kernel(x_prompt, x_sample, state_hgrn, cache_c_kv_w128, cache_c_kv_w512, cache_c_kv_w2048, ln_g, ln_b, a_w_in, a_lb_logits, a_norm_g, a_w_out, b_w_in, b_b_in, b_ln_g, b_ln_b, b_w_s, b_b_s, b_w_out, c_w_in, c_w_out, moe_w_group, moe_w_expert, moe_w1, moe_w3, moe_w2):
    bp, tp, _ = x_prompt.shape
    bs, ts, _ = x_sample.shape
    assert tp % 512 == 0 and tp // C_GROUPS[-1][1] >= 2 * C_QBLOCK
    assert ts <= 8 and ts <= C_GROUPS[1][1] and (bp * tp) % MOE_TILE == 0

    lb_p = jax.nn.softmax(a_lb_logits.astype(F32), axis=0)
    lb_all = jnp.clip(jnp.cumsum(lb_p, axis=0) - lb_p[0:1], 0.0, 1.0 - 1e-6)
    caches = (cache_c_kv_w128, cache_c_kv_w512, cache_c_kv_w2048)

    xp = x_prompt.reshape(bp * tp, D_MODEL)
    xs = x_sample.reshape(bs * ts, D_MODEL)
    hgrn_p, hgrn_s, chunk_v_s = [], [], []
    kv_p = [[] for _ in C_GROUPS]
    kv_s = [[] for _ in C_GROUPS]

    for i in range(DEPTH):
        kind, j = i % 3, i // 3
        g0, b0 = ln_g[i, 0][None], ln_b[i, 0][None]
        if kind == 0:
            w_in, w_out = a_w_in[j].astype(BF16), a_w_out[j].astype(BF16)
            lb, ng = lb_all[j][None], a_norm_g[j][None]
            xp, sp = _hgrn_layer(xp, bp, tp, None, w_in, lb, ng, w_out, g0, b0)
            xs, ss = _hgrn_layer(xs, bs, ts, state_hgrn[j].astype(F32), w_in, lb, ng, w_out, g0, b0)
            hgrn_p.append(sp)
            hgrn_s.append(ss)
        elif kind == 1:
            w_in, w_out = b_w_in[j].astype(BF16), b_w_out[j].astype(BF16)
            args = (w_in, b_b_in[j][None], b_ln_g[j][None], b_ln_b[j][None], b_w_s[j], b_b_s[j], w_out, g0, b0)
            xp, _ = _gmlp_layer(xp, bp, tp, *args)
            xs, uvs = _gmlp_layer(xs, bs, ts, *args)
            chunk_v_s.append(uvs[:, D_MODEL:].reshape(bs, ts, D_MODEL))
        else:
            w_in, w_out = c_w_in[j].astype(BF16), c_w_out[j].astype(BF16)
            xp, kvp = _attn_layer(xp, bp, tp, None, 0, w_in, w_out, g0, b0)
            xs, kvs = _attn_layer(xs, bs, ts, tuple(c[j] for c in caches), PAST_LEN, w_in, w_out, g0, b0)
            for g in range(len(C_GROUPS)):
                kv_p[g].append(kvp[g])
                kv_s[g].append(kvs[g])
        wr = jnp.pad(jnp.concatenate([moe_w_expert[i], moe_w_group[i]], axis=1),
                     ((0, 0), (0, LANES - MOE_GE - MOE_GROUPS)))
        w1g, w3g = _group_up_weights(moe_w1[i]), _group_up_weights(moe_w3[i])
        w2g = moe_w2[i].astype(BF16).reshape(MOE_GROUPS, MOE_EXPERTS * MOE_FF, D_MODEL)
        g1, b1 = ln_g[i, 1][None], ln_b[i, 1][None]
        xp = _moe_dispatch(xp, wr, w1g, w3g, w2g, g1, b1, MOE_TILE)
        xs = _moe(xs, wr, w1g, w3g, w2g, g1, b1, bs * ts)

    return (xp.reshape(bp, tp, D_MODEL), xs.reshape(bs, ts, D_MODEL),
            jnp.stack(hgrn_p), jnp.stack(hgrn_s), jnp.stack(chunk_v_s),
            jnp.stack(kv_p[0]), jnp.stack(kv_s[0]), jnp.stack(kv_p[1]), jnp.stack(kv_s[1]),
            jnp.stack(kv_p[2]), jnp.stack(kv_s[2]))
```

```python
import functools

import jax
import jax.numpy as jnp
from jax import lax
from jax.experimental import pallas as pl
from jax.experimental.pallas import tpu as pltpu

F32 = jnp.float32
BF16 = jnp.bfloat16
HIGHEST = lax.Precision.HIGHEST

D_MODEL = 2048
DEPTH = 4
PAST_LEN = 16384
A_HEADS = 16
A_DK = 128
A_DV = 128
A_WIDTH = A_HEADS * A_DK
HGRN_CHUNK = 64
HGRN_SUB = 16
HGRN_SUPER = 256
B_CHUNK = 128
B_HEADS = 16
B_HD = 128
C_HEADS = 8
C_HD = 128
C_GROUPS = ((128, 1), (512, 4), (2048, 16))
C_KEYS = 129
C_QBLOCK = 128
C_QKV = 3 * len(C_GROUPS) * C_HEADS * C_HD
ROPE_THETA = 10000.0
MOE_GROUPS = 4
MOE_EXPERTS = 4
MOE_GE = MOE_GROUPS * MOE_EXPERTS
MOE_FF = 256
MOE_TILE = 256
LN_EPS = 1e-5
RMS_EPS = 1e-6
ALPHA = (2 * DEPTH) ** 0.25
LANES = 128
VMEM_LIMIT = 56 * 1024 * 1024
PROJ_ROWS = 1024


def _cparams(*sem):
    return pltpu.CompilerParams(dimension_semantics=sem, vmem_limit_bytes=VMEM_LIMIT)


def _sigmoid(x):
    return 1.0 / (1.0 + jnp.exp(-x))


def _ln_rows(y, g, b):
    mu = jnp.mean(y, axis=-1, keepdims=True)
    d = y - mu
    var = jnp.mean(d * d, axis=-1, keepdims=True)
    return d * lax.rsqrt(var + LN_EPS) * g + b


def _nt_dot(a, b):
    return lax.dot_general(a, b, (((1,), (1,)), ((), ())), preferred_element_type=F32)


def _tn_dot(a, b):
    return lax.dot_general(a, b, (((0,), (0,)), ((), ())), preferred_element_type=F32)


def _proj_body(x_ref, w_ref, *rest, epilogue, n_extra):
    extras = rest[:n_extra]
    o_ref = rest[n_extra]
    xb_ref = rest[n_extra + 1]
    j = pl.program_id(1)

    @pl.when(j == 0)
    def _cast():
        xb_ref[...] = x_ref[...].astype(BF16)

    acc = jnp.dot(xb_ref[...], w_ref[...], preferred_element_type=F32)
    epilogue(acc, j, extras, o_ref)


def _proj(x, w, extras, extra_specs, epilogue, m, bm, bn, name):
    k = x.shape[1]
    n = w.shape[1]
    return pl.pallas_call(
        functools.partial(_proj_body, epilogue=epilogue, n_extra=len(extras)),
        grid=(m // bm, n // bn),
        in_specs=[pl.BlockSpec((bm, k), lambda i, j: (i, 0)),
                  pl.BlockSpec((k, bn), lambda i, j: (0, j))] + list(extra_specs),
        out_specs=pl.BlockSpec((bm, bn), lambda i, j: (i, j)),
        out_shape=jax.ShapeDtypeStruct((m, n), F32),
        scratch_shapes=[pltpu.VMEM((bm, k), BF16)],
        compiler_params=_cparams("parallel", "arbitrary"),
        name=name,
    )(x, w, *extras)


def _hgrn_proj_epilogue(acc, j, extras, o_ref, *, bn):
    nq = A_WIDTH // bn
    is_silu = jnp.logical_or(j < nq, j >= 3 * nq)

    @pl.when(is_silu)
    def _():
        o_ref[...] = acc * _sigmoid(acc)

    @pl.when(jnp.logical_not(is_silu))
    def _():
        o_ref[...] = acc


def _gelu_tanh(z):
    return 0.5 * z * (1.0 + jnp.tanh(0.7978845608028654 * (z + 0.044715 * (z * z * z))))


def _gmlp_proj_epilogue(acc, j, extras, o_ref):
    bias_ref, g_ref, b_ref = extras
    z = _gelu_tanh(acc + bias_ref[...])

    @pl.when(j == 0)
    def _():
        o_ref[...] = z

    @pl.when(j == 1)
    def _():
        o_ref[...] = _ln_rows(z, g_ref[...], b_ref[...])


def _attn_proj_epilogue(acc, j, extras, o_ref, *, bn):
    cos_ref, sin_ref = extras
    n_rot = 2 * len(C_GROUPS) * C_HEADS * C_HD // bn

    @pl.when(j < n_rot)
    def _():
        cos = cos_ref[...]
        sin = sin_ref[...]
        for h in range(bn // C_HD):
            xh = acc[:, h * C_HD:(h + 1) * C_HD]
            o_ref[:, h * C_HD:(h + 1) * C_HD] = xh * cos + pltpu.roll(xh, C_HD // 2, 1) * sin

    @pl.when(j >= n_rot)
    def _():
        o_ref[...] = acc


def _out_ln_body(*refs, prologue, n_in):
    ins = refs[:n_in]
    w_ref, r_ref, g_ref, b_ref, o_ref = refs[n_in:n_in + 5]
    a = prologue(*ins)
    acc = jnp.dot(a, w_ref[...], preferred_element_type=F32)
    y = ALPHA * r_ref[...] + acc
    o_ref[...] = _ln_rows(y, g_ref[...], b_ref[...])


def _out_ln(ins, in_specs, prologue, w, resid, g, b, bm, name):
    m = ins[0].shape[0]
    k = w.shape[0]
    row = pl.BlockSpec((bm, D_MODEL), lambda i: (i, 0))
    vec = pl.BlockSpec((1, D_MODEL), lambda i: (0, 0))
    return pl.pallas_call(
        functools.partial(_out_ln_body, prologue=prologue, n_in=len(ins)),
        grid=(m // bm,),
        in_specs=list(in_specs) + [pl.BlockSpec((k, D_MODEL), lambda i: (0, 0)), row, vec, vec],
        out_specs=row,
        out_shape=jax.ShapeDtypeStruct((m, D_MODEL), F32),
        compiler_params=_cparams("parallel"),
        name=name,
    )(*ins, w, resid, g, b)


def _cast_prologue(a_ref):
    return a_ref[...].astype(BF16)


def _merge_prologue(o0, o1, o2, l0, l1, l2):
    a0, a1, a2 = l0[...], l1[...], l2[...]
    mx = jnp.maximum(jnp.maximum(a0, a1), a2)
    e0, e1, e2 = jnp.exp(a0 - mx), jnp.exp(a1 - mx), jnp.exp(a2 - mx)
    o = (e0 * o0[...] + e1 * o1[...] + e2 * o2[...]) / (e0 + e1 + e2)
    return o.astype(BF16)


def _moe_gate(logits):
    lane = lax.broadcasted_iota(jnp.int32, logits.shape, 1).astype(F32)
    neg = -jnp.inf
    big = 4.0 * LANES
    gl = jnp.where((lane >= MOE_GE) & (lane < MOE_GE + MOE_GROUPS), logits, neg)
    gmax = jnp.max(gl, axis=-1, keepdims=True)
    g_idx = jnp.min(jnp.where(gl == gmax, lane - MOE_GE, big), axis=-1, keepdims=True)
    g_top = 1.0 / jnp.sum(jnp.exp(gl - gmax), axis=-1, keepdims=True)
    lo = g_idx * MOE_EXPERTS
    el = jnp.where((lane >= lo) & (lane < lo + MOE_EXPERTS), logits, neg)
    m1 = jnp.max(el, axis=-1, keepdims=True)
    i1 = jnp.min(jnp.where(el == m1, lane, big), axis=-1, keepdims=True)
    el2 = jnp.where(lane == i1, neg, el)
    m2 = jnp.max(el2, axis=-1, keepdims=True)
    i2 = jnp.min(jnp.where(el2 == m2, lane, big), axis=-1, keepdims=True)
    r = jnp.exp(m2 - m1)
    w1 = g_top / (1.0 + r)
    w2 = w1 * r
    return jnp.where(lane == i1, w1, 0.0) + jnp.where(lane == i2, w2, 0.0), g_idx


def _gate_columns(gate, first):
    lane = lax.broadcasted_iota(jnp.int32, gate.shape, 1)
    return jnp.concatenate(
        [jnp.broadcast_to(jnp.sum(jnp.where(lane == first + e, gate, 0.0), axis=-1, keepdims=True),
                          (gate.shape[0], MOE_FF)) for e in range(MOE_EXPERTS)], axis=1)


def _group_ffn(xb, gate, first, w1_ref, w3_ref, w2_ref):
    h1 = jnp.dot(xb, w1_ref[0], preferred_element_type=F32)
    h3 = jnp.dot(xb, w3_ref[0], preferred_element_type=F32)
    hg = (h1 * _sigmoid(h1) * h3 * _gate_columns(gate, first)).astype(BF16)
    return jnp.dot(hg, w2_ref[0], preferred_element_type=F32)


def _moe_body(x_ref, wr_ref, w1_ref, w3_ref, w2_ref, g_ref, b_ref, o_ref, xb_ref, gate_ref, acc_ref):
    gi = pl.program_id(1)

    @pl.when(gi == 0)
    def _route():
        x = x_ref[...]
        xb_ref[...] = x.astype(BF16)
        logits = jnp.dot(x, wr_ref[...], precision=HIGHEST, preferred_element_type=F32)
        gate_ref[...] = _moe_gate(logits)[0]
        acc_ref[...] = jnp.zeros_like(acc_ref)

    acc_ref[...] += _group_ffn(xb_ref[...], gate_ref[...], gi * MOE_EXPERTS, w1_ref, w3_ref, w2_ref)

    @pl.when(gi == MOE_GROUPS - 1)
    def _finish():
        y = ALPHA * x_ref[...] + acc_ref[...]
        o_ref[...] = _ln_rows(y, g_ref[...], b_ref[...])


def _moe(x, wr, w1g, w3g, w2g, g, b, bm):
    m = x.shape[0]
    eff = MOE_EXPERTS * MOE_FF
    row = pl.BlockSpec((bm, D_MODEL), lambda i, e: (i, 0))
    vec = pl.BlockSpec((1, D_MODEL), lambda i, e: (0, 0))
    return pl.pallas_call(
        _moe_body,
        grid=(m // bm, MOE_GROUPS),
        in_specs=[row,
                  pl.BlockSpec((D_MODEL, LANES), lambda i, e: (0, 0)),
                  pl.BlockSpec((1, D_MODEL, eff), lambda i, e: (e, 0, 0)),
                  pl.BlockSpec((1, D_MODEL, eff), lambda i, e: (e, 0, 0)),
                  pl.BlockSpec((1, eff, D_MODEL), lambda i, e: (e, 0, 0)),
                  vec, vec],
        out_specs=row,
        out_shape=jax.ShapeDtypeStruct((m, D_MODEL), F32),
        scratch_shapes=[pltpu.VMEM((bm, D_MODEL), BF16),
                        pltpu.VMEM((bm, LANES), F32),
                        pltpu.VMEM((bm, D_MODEL), F32)],
        compiler_params=_cparams("parallel", "arbitrary"),
        name="moe",
    )(x, wr, w1g, w3g, w2g, g, b)


def _router_body(x_ref, wr_ref, gate_ref, gidx_ref):
    logits = jnp.dot(x_ref[...], wr_ref[...], precision=HIGHEST, preferred_element_type=F32)
    gate, g_idx = _moe_gate(logits)
    gate_ref[...] = gate
    gidx_ref[...] = jnp.broadcast_to(g_idx, gidx_ref.shape)


def _router(x, wr, m, bm):
    out = pl.BlockSpec((bm, LANES), lambda i: (i, 0))
    return pl.pallas_call(
        _router_body,
        grid=(m // bm,),
        in_specs=[pl.BlockSpec((bm, D_MODEL), lambda i: (i, 0)),
                  pl.BlockSpec((D_MODEL, LANES), lambda i: (0, 0))],
        out_specs=[out, out],
        out_shape=[jax.ShapeDtypeStruct((m, LANES), F32)] * 2,
        compiler_params=_cparams("parallel"),
        name="moe_router",
    )(x, wr)


def _moe_sorted_body(tg_ref, src_ref, dst_ref, x_hbm, gate_hbm, w1_ref, w3_ref, w2_ref, g_ref, b_ref, o_hbm,
                     xbuf, gbuf, obuf, xsem, gsem, osem, *, bm):
    i = pl.program_id(0)
    last = pl.num_programs(0) - 1
    slot = i % 2

    def gather(tile, s):
        for j in range(bm):
            r = src_ref[tile * bm + j]
            pltpu.make_async_copy(x_hbm.at[pl.ds(r, 1), :], xbuf.at[s, pl.ds(j, 1), :], xsem.at[s]).start()
            pltpu.make_async_copy(gate_hbm.at[pl.ds(r, 1), :], gbuf.at[s, pl.ds(j, 1), :], gsem.at[s]).start()

    def wait_gather(s):
        pltpu.make_async_copy(x_hbm.at[pl.ds(0, bm), :], xbuf.at[s], xsem.at[s]).wait()
        pltpu.make_async_copy(gate_hbm.at[pl.ds(0, bm), :], gbuf.at[s], gsem.at[s]).wait()

    def scatter(tile, s):
        for j in range(bm):
            r = dst_ref[(tile + 1) * bm + j]
            pltpu.make_async_copy(obuf.at[s, pl.ds(j, 1), :], o_hbm.at[pl.ds(r, 1), :], osem.at[s]).start()

    def wait_scatter(s):
        pltpu.make_async_copy(obuf.at[s], o_hbm.at[pl.ds(0, bm), :], osem.at[s]).wait()

    @pl.when(i == 0)
    def _first():
        gather(0, 0)
        obuf[1] = jnp.zeros(obuf.shape[1:], F32)

    wait_gather(slot)

    @pl.when(i >= 1)
    def _reuse():
        wait_scatter(slot)

    gather(i + 1, 1 - slot)
    scatter(i - 1, 1 - slot)
    x = xbuf[slot]
    ffn = _group_ffn(x.astype(BF16), gbuf[slot], tg_ref[i] * MOE_EXPERTS, w1_ref, w3_ref, w2_ref)
    obuf[slot] = _ln_rows(ALPHA * x + ffn, g_ref[...], b_ref[...])

    @pl.when(i == last)
    def _drain():
        scatter(i, slot)
        wait_gather(1 - slot)
        wait_scatter(1 - slot)
        wait_scatter(slot)


def _moe_sorted(tile_group, src, dst, x, gate, w1g, w3g, w2g, g, b, bm, out_rows):
    n_tiles = tile_group.shape[0]
    eff = MOE_EXPERTS * MOE_FF
    vec = pl.BlockSpec((1, D_MODEL), lambda i, tg, sr, ds: (0, 0))
    hbm = pl.BlockSpec(memory_space=pl.ANY)
    return pl.pallas_call(
        functools.partial(_moe_sorted_body, bm=bm),
        grid_spec=pltpu.PrefetchScalarGridSpec(
            num_scalar_prefetch=3,
            grid=(n_tiles,),
            in_specs=[hbm, hbm,
                      pl.BlockSpec((1, D_MODEL, eff), lambda i, tg, sr, ds: (tg[i], 0, 0)),
                      pl.BlockSpec((1, D_MODEL, eff), lambda i, tg, sr, ds: (tg[i], 0, 0)),
                      pl.BlockSpec((1, eff, D_MODEL), lambda i, tg, sr, ds: (tg[i], 0, 0)),
                      vec, vec],
            out_specs=hbm,
            scratch_shapes=[pltpu.VMEM((2, bm, D_MODEL), F32),
                            pltpu.VMEM((2, bm, LANES), F32),
                            pltpu.VMEM((2, bm, D_MODEL), F32),
                            pltpu.SemaphoreType.DMA((2,)),
                            pltpu.SemaphoreType.DMA((2,)),
                            pltpu.SemaphoreType.DMA((2,))]),
        out_shape=jax.ShapeDtypeStruct((out_rows, D_MODEL), F32),
        compiler_params=_cparams("arbitrary"),
        name="moe_sorted",
    )(tile_group, src, dst, x, gate, w1g, w3g, w2g, g, b)


def _group_up_weights(w):
    w = w.astype(BF16).reshape(MOE_GROUPS, MOE_EXPERTS, D_MODEL, MOE_FF)
    return w.transpose(0, 2, 1, 3).reshape(MOE_GROUPS, D_MODEL, MOE_EXPERTS * MOE_FF)


def _moe_dispatch(x, n, wr, w1g, w3g, w2g, g, b, bm):
    gate, gidx = _router(x, wr, n, _row_tile(n, 512))
    grp = gidx[:, 0].astype(jnp.int32)
    onehot = (grp[:, None] == jnp.arange(MOE_GROUPS, dtype=jnp.int32)[None, :]).astype(jnp.int32)
    counts = jnp.sum(onehot, axis=0)
    padded = (counts + bm - 1) // bm * bm
    ends = jnp.cumsum(padded)
    starts = ends - padded
    rank = jnp.cumsum(onehot, axis=0) - onehot
    dest = jnp.sum((starts[None, :] + rank) * onehot, axis=1)
    n_tiles = n // bm + MOE_GROUPS
    token = jnp.arange(n, dtype=jnp.int32)
    src = jnp.zeros(((n_tiles + 1) * bm,), jnp.int32).at[dest].set(token)
    srt = jnp.arange(-bm, n_tiles * bm, dtype=jnp.int32)
    dst = (n + (srt // bm % 2) * bm + srt % bm).at[bm + dest].set(token)
    tile_start = jnp.arange(n_tiles, dtype=jnp.int32) * bm
    tile_group = jnp.minimum(jnp.searchsorted(ends, tile_start, side="right"), MOE_GROUPS - 1).astype(jnp.int32)
    return _moe_sorted(tile_group, src, dst, x, gate, w1g, w3g, w2g, g, b, bm, n + 2 * bm)


def _hgrn_gates(f, lb):
    log_sig = jnp.minimum(f, 0.0) - jnp.log1p(jnp.exp(-jnp.abs(f)))
    a = jnp.log1p(-lb) + log_sig
    log_lb = jnp.log(lb)
    log_f = jnp.maximum(log_lb, a) + jnp.log1p(jnp.exp(-jnp.abs(log_lb - a)))
    k = (1.0 - lb) / (1.0 + jnp.exp(f))
    return log_f, k


def _hgrn_finish(o, gate, ng):
    o = o * lax.rsqrt(jnp.mean(o * o, axis=-1, keepdims=True) + RMS_EPS) * ng
    return o * gate


def _rows_of(x, idx, n):
    return jnp.concatenate([jnp.broadcast_to(x[i:i + 1, :], (n, x.shape[1])) for i in idx], axis=0)


def _hgrn_scan_body(q_ref, f_ref, v_ref, g_ref, lb_ref, ng_ref, tri_ref, o_ref, s_ref, st_ref, *, n_super):
    C, c, N = HGRN_CHUNK, HGRN_SUB, HGRN_SUPER
    half = C // 2
    neg = -1e30
    tb = pl.program_id(2)

    @pl.when(tb == 0)
    def _init():
        st_ref[...] = jnp.zeros_like(st_ref)

    lb = lb_ref[...]
    ng = ng_ref[...]
    row = lax.broadcasted_iota(jnp.int32, (N, 1), 0)
    hi32 = (row & (C - 1)) >= half
    hi16 = (row & (half - 1)) >= c
    lane = lax.broadcasted_iota(jnp.int32, (8, LANES), 1)
    srow = lax.broadcasted_iota(jnp.int32, (8, 1), 0)

    def super_chunk(si, carry):
        r0 = pl.multiple_of(si * N, N)
        q = q_ref[pl.ds(r0, N), :]
        v = v_ref[pl.ds(r0, N), :].astype(BF16)
        gate = g_ref[pl.ds(r0, N), :]
        log_f, k = _hgrn_gates(f_ref[pl.ds(r0, N), :], lb)
        b = jnp.dot(tri_ref[...], log_f, precision=HIGHEST, preferred_element_type=F32)

        ra = _rows_of(b, [C * m + half - 1 for m in range(N // C)], C)
        qa = jnp.where(hi32, q * jnp.exp(jnp.where(hi32, b - ra, 0.0)), 0.0)
        ka = jnp.where(hi32, 0.0, k * jnp.exp(jnp.where(hi32, 0.0, ra - b)))
        rb = _rows_of(b, [half * m + c - 1 for m in range(N // half)], half)
        qb = jnp.where(hi16, q * jnp.exp(jnp.where(hi16, b - rb, 0.0)), 0.0)
        kb = jnp.where(hi16, 0.0, k * jnp.exp(jnp.where(hi16, 0.0, rb - b)))
        s2 = lax.broadcasted_iota(jnp.int32, (N, N), 0)
        t2 = lax.broadcasted_iota(jnp.int32, (N, N), 1)
        same64 = jnp.right_shift(s2, 6) == jnp.right_shift(t2, 6)
        same32 = jnp.right_shift(s2, 5) == jnp.right_shift(t2, 5)
        att_t = (jnp.where(same64, _nt_dot(ka.astype(BF16), qa.astype(BF16)), 0.0)
                 + jnp.where(same32, _nt_dot(kb.astype(BF16), qb.astype(BF16)), 0.0))

        pieces = [jnp.zeros((8, LANES), F32)] * (N // 8)
        for i in range(N // c):
            base = c * i
            ks = (k[base:base + 8, :], k[base + 8:base + c, :])
            bs = (b[base:base + 8, :], b[base + 8:base + c, :])
            for tl in range(c):
                qt = q[base + tl:base + tl + 1, :]
                bt = b[base + tl:base + tl + 1, :]
                for u in range(tl // 8 + 1):
                    d = bt - bs[u]
                    if u == tl // 8:
                        d = jnp.where(srow <= tl - 8 * u, d, neg)
                    col = jnp.sum(qt * ks[u] * jnp.exp(d), axis=-1, keepdims=True)
                    pieces[2 * i + u] = jnp.where(lane == base % LANES + tl, col, pieces[2 * i + u])
        zero = jnp.zeros((N // 2, LANES), F32)
        n_lo = N // 16
        diag = jnp.concatenate(
            [jnp.concatenate(pieces[:n_lo] + [zero], axis=0),
             jnp.concatenate([zero] + pieces[n_lo:], axis=0)], axis=1)
        o_intra = _tn_dot((att_t + diag).astype(BF16), v)

        bl = _rows_of(b, [C * m + C - 1 for m in range(N // C)], C)
        qe = (q * jnp.exp(b)).astype(BF16)
        kd = (k * jnp.exp(bl - b)).astype(BF16)
        st = st_ref[...]
        for m in range(N // C):
            rows = slice(C * m, C * m + C)
            o = o_intra[rows, :] + _nt_dot(qe[rows, :], st.astype(BF16))
            o_ref[pl.ds(r0 + C * m, C), :] = _hgrn_finish(o, gate[rows, :], ng).astype(o_ref.dtype)
            st = st * jnp.exp(b[C * m + C - 1:C * m + C, :]) + _tn_dot(v[rows, :], kd[rows, :])
        st_ref[...] = st
        return carry

    lax.fori_loop(0, n_super, super_chunk, 0)

    @pl.when(tb == pl.num_programs(2) - 1)
    def _emit():
        s_ref[0, 0] = st_ref[...].T


def _hgrn_scan(p, lb, ng, batch, seq, tb):
    nt = seq // tb
    nh = A_HEADS
    n = HGRN_SUPER
    idx = jnp.arange(n)
    tri = ((idx[:, None] >= idx[None, :])
           & (idx[:, None] // HGRN_CHUNK == idx[None, :] // HGRN_CHUNK)).astype(F32)

    def col(off):
        return pl.BlockSpec((tb, A_DK), lambda b, h, t: (b * nt + t, off * nh + h))

    return pl.pallas_call(
        functools.partial(_hgrn_scan_body, n_super=tb // n),
        grid=(batch, nh, nt),
        in_specs=[col(0), col(1), col(2), col(3),
                  pl.BlockSpec((1, A_DK), lambda b, h, t: (0, h)),
                  pl.BlockSpec((1, A_DV), lambda b, h, t: (0, 0)),
                  pl.BlockSpec((n, n), lambda b, h, t: (0, 0))],
        out_specs=[pl.BlockSpec((tb, A_DV), lambda b, h, t: (b * nt + t, h)),
                   pl.BlockSpec((1, 1, A_DK, A_DV), lambda b, h, t: (b, h, 0, 0))],
        out_shape=[jax.ShapeDtypeStruct((batch * seq, nh * A_DV), BF16),
                   jax.ShapeDtypeStruct((batch, nh, A_DK, A_DV), F32)],
        scratch_shapes=[pltpu.VMEM((A_DV, A_DK), F32)],
        compiler_params=_cparams("parallel", "parallel", "arbitrary"),
        name="hgrn_scan",
    )(p, p, p, p, lb, ng, tri)


def _hgrn_step_body(p_ref, s0_ref, lb_ref, ng_ref, o_ref, s_ref, *, n_tok):
    R = p_ref.shape[0]
    row = lax.broadcasted_iota(jnp.int32, (R, 1), 0)
    valid = row < n_tok
    r2 = lax.broadcasted_iota(jnp.int32, (R, R), 0)
    c2 = lax.broadcasted_iota(jnp.int32, (R, R), 1)
    tril = (r2 >= c2).astype(F32)
    ng = ng_ref[...]
    for h in range(A_HEADS):
        sl = slice(h * A_DK, (h + 1) * A_DK)
        q = p_ref[:, sl]
        v = p_ref[:, 2 * A_WIDTH + h * A_DV:2 * A_WIDTH + (h + 1) * A_DV]
        gate = p_ref[:, 3 * A_WIDTH + h * A_DV:3 * A_WIDTH + (h + 1) * A_DV]
        log_f, k = _hgrn_gates(p_ref[:, A_WIDTH + h * A_DK:A_WIDTH + (h + 1) * A_DK], lb_ref[:, sl])
        b = jnp.dot(tril, log_f, precision=HIGHEST, preferred_element_type=F32)
        st = s0_ref[0, h].T
        o = _nt_dot((q * jnp.exp(b)).astype(BF16), st.astype(BF16))
        for s in range(n_tok):
            m = row >= s
            w = jnp.where(m, q * k[s:s + 1, :] * jnp.exp(jnp.where(m, b - b[s:s + 1, :], 0.0)), 0.0)
            o = o + jnp.sum(w, axis=-1, keepdims=True) * v[s:s + 1, :]
        o_ref[:, h * A_DV:(h + 1) * A_DV] = _hgrn_finish(o, gate, ng)
        bl = b[n_tok - 1:n_tok, :]
        kd = jnp.where(valid, k * jnp.exp(jnp.where(valid, bl - b, 0.0)), 0.0)
        st_new = st * jnp.exp(bl) + _tn_dot(v.astype(BF16), kd.astype(BF16))
        s_ref[0, h] = st_new.T


def _hgrn_step(p, s0, lb, ng, batch, n_tok):
    rows = p.shape[0] // batch
    return pl.pallas_call(
        functools.partial(_hgrn_step_body, n_tok=n_tok),
        grid=(batch,),
        in_specs=[pl.BlockSpec((rows, 4 * A_WIDTH), lambda b: (b, 0)),
                  pl.BlockSpec((1, A_HEADS, A_DK, A_DV), lambda b: (b, 0, 0, 0)),
                  pl.BlockSpec((1, A_WIDTH), lambda b: (0, 0)),
                  pl.BlockSpec((1, A_DV), lambda b: (0, 0))],
        out_specs=[pl.BlockSpec((rows, A_HEADS * A_DV), lambda b: (b, 0)),
                   pl.BlockSpec((1, A_HEADS, A_DK, A_DV), lambda b: (b, 0, 0, 0))],
        out_shape=[jax.ShapeDtypeStruct((batch * rows, A_HEADS * A_DV), F32),
                   jax.ShapeDtypeStruct((batch, A_HEADS, A_DK, A_DV), F32)],
        compiler_params=_cparams("parallel"),
        name="hgrn_step",
    )(p, s0, lb, ng)


def _gmlp_prologue(u_ref, v_ref, ws_ref, bs_ref, gated_ref, *, chunk):
    bm = u_ref.shape[0]
    r2 = lax.broadcasted_iota(jnp.int32, (chunk, chunk), 0)
    c2 = lax.broadcasted_iota(jnp.int32, (chunk, chunk), 1)
    causal = r2 >= c2
    for h in range(B_HEADS):
        wc = jnp.where(causal, ws_ref[h], 0.0).astype(BF16)
        bias = bs_ref[:, h:h + 1]
        cols = slice(h * B_HD, (h + 1) * B_HD)
        for n in range(bm // chunk):
            rows = slice(n * chunk, (n + 1) * chunk)
            mixed = jnp.dot(wc, v_ref[rows, cols].astype(BF16), preferred_element_type=F32) + bias
            gated_ref[rows, cols] = (u_ref[rows, cols] * mixed).astype(BF16)
    return gated_ref[...]


def _gmlp_out_body(u_ref, v_ref, ws_ref, bs_ref, w_ref, r_ref, g_ref, b_ref, o_ref, gated_ref, *, chunk):
    a = _gmlp_prologue(u_ref, v_ref, ws_ref, bs_ref, gated_ref, chunk=chunk)
    acc = jnp.dot(a, w_ref[...], preferred_element_type=F32)
    y = ALPHA * r_ref[...] + acc
    o_ref[...] = _ln_rows(y, g_ref[...], b_ref[...])


def _gmlp_out(uv, ws, bs_t, w, resid, g, b, bm, chunk):
    m = uv.shape[0]
    row = pl.BlockSpec((bm, D_MODEL), lambda i: (i, 0))
    vec = pl.BlockSpec((1, D_MODEL), lambda i: (0, 0))
    return pl.pallas_call(
        functools.partial(_gmlp_out_body, chunk=chunk),
        grid=(m // bm,),
        in_specs=[pl.BlockSpec((bm, D_MODEL), lambda i: (i, 0)),
                  pl.BlockSpec((bm, D_MODEL), lambda i: (i, 1)),
                  pl.BlockSpec(ws.shape, lambda i: (0, 0, 0)),
                  pl.BlockSpec(bs_t.shape, lambda i: (0, 0)),
                  pl.BlockSpec((D_MODEL, D_MODEL), lambda i: (0, 0)),
                  row, vec, vec],
        out_specs=row,
        out_shape=jax.ShapeDtypeStruct((m, D_MODEL), F32),
        scratch_shapes=[pltpu.VMEM((bm, D_MODEL), BF16)],
        compiler_params=_cparams("parallel"),
        name="gmlp_out",
    )(uv, uv, ws, bs_t, w, resid, g, b)


def _band_attn_body(q_ref, k_ref, v_ref, o_ref, l_ref, *, dil, seq):
    bq = C_QBLOCK
    span = C_KEYS - 1
    scale = C_HD ** -0.5
    n_blocks = seq // dil // bq
    qi = lax.broadcasted_iota(jnp.int32, (bq, 2 * bq), 0)
    ki = lax.broadcasted_iota(jnp.int32, (bq, 2 * bq), 1)

    def rows(first, n):
        return pl.ds(first, n) if dil == 1 else pl.ds(first, n, stride=dil)

    for r in range(dil):
        for i in range(n_blocks):
            w = max(i - 1, 0)
            qs = rows(r + dil * bq * i, bq)
            ws = rows(r + dil * bq * w, 2 * bq)
            q = q_ref[qs, :].astype(BF16)
            kw = k_ref[ws, :].astype(BF16)
            vw = v_ref[ws, :].astype(BF16)
            s = _nt_dot(q, kw) * scale
            rel = bq * (i - w) + qi - ki
            s = jnp.where((rel >= 0) & (rel <= span), s, -jnp.inf)
            mx = jnp.max(s, axis=-1, keepdims=True)
            p = jnp.exp(s - mx)
            den = jnp.sum(p, axis=-1, keepdims=True)
            o_ref[qs, :] = jnp.dot(p.astype(BF16), vw, preferred_element_type=F32) / den
            l_ref[qs, :] = jnp.broadcast_to(mx + jnp.log(den), (bq, C_HD))


def _band_attn(qkv, g, dil, batch, seq):
    ng = len(C_GROUPS)

    def col(part):
        return pl.BlockSpec((seq, C_HD), lambda b, h: (b, (part * ng + g) * C_HEADS + h))

    out = pl.BlockSpec((seq, C_HD), lambda b, h: (b, h))
    return pl.pallas_call(
        functools.partial(_band_attn_body, dil=dil, seq=seq),
        grid=(batch, C_HEADS),
        in_specs=[col(0), col(1), col(2)],
        out_specs=[out, out],
        out_shape=[jax.ShapeDtypeStruct((batch * seq, C_HEADS * C_HD), F32)] * 2,
        compiler_params=_cparams("parallel", "parallel"),
        name=f"band_attn_d{dil}",
    )(qkv, qkv, qkv)


def _step_attn_body(qkv_ref, *refs, g, dil, n_tok):
    past_refs, (o_ref, l_ref) = refs[:-2], refs[-2:]
    scale = C_HD ** -0.5
    ng = len(C_GROUPS)
    hs = C_HEADS
    tok = lax.broadcasted_iota(jnp.int32, (n_tok, 1, 1), 0)
    prow = lax.broadcasted_iota(jnp.int32, (past_refs[0].shape[1], 1, 1), 0)
    kn = qkv_ref[0, :, (ng + g) * hs:(ng + g + 1) * hs, :]
    vn = qkv_ref[0, :, (2 * ng + g) * hs:(2 * ng + g + 1) * hs, :]
    for t in range(n_tok):
        past_ref = past_refs[t % dil]
        new_ok = (tok <= t) & (((t - tok) % dil) == 0)
        past_ok = prow >= (t if dil == 1 else 0)
        q = qkv_ref[0, t, g * hs:(g + 1) * hs, :][None]
        kp = past_ref[0, :, 0:hs, :]
        vp = past_ref[0, :, hs:2 * hs, :]
        sp = jnp.where(past_ok, jnp.sum(kp * q, axis=-1, keepdims=True) * scale, -jnp.inf)
        sn = jnp.where(new_ok, jnp.sum(kn * q, axis=-1, keepdims=True) * scale, -jnp.inf)
        mx = jnp.maximum(jnp.max(sp, axis=0, keepdims=True), jnp.max(sn, axis=0, keepdims=True))
        pp = jnp.exp(sp - mx)
        pn = jnp.exp(sn - mx)
        den = jnp.sum(pp, axis=0, keepdims=True) + jnp.sum(pn, axis=0, keepdims=True)
        o = (jnp.sum(pp * vp, axis=0, keepdims=True) + jnp.sum(pn * vn, axis=0, keepdims=True)) / den
        o_ref[0, t] = o[0]
        l_ref[0, t] = jnp.broadcast_to(mx + jnp.log(den), (1, hs, C_HD))[0]


def _step_attn(qkv, cache, g, window, dil, batch, n_tok):
    past = cache.reshape(batch, window // dil, dil, 2 * C_HEADS, C_HD)
    n_res = min(dil, n_tok)
    out = pl.BlockSpec((1, n_tok, C_HEADS, C_HD), lambda b: (b, 0, 0, 0))
    past_specs = [pl.BlockSpec((1, window // dil, None, 2 * C_HEADS, C_HD),
                               functools.partial(lambda b, r: (b, 0, r, 0, 0), r=r)) for r in range(n_res)]
    return pl.pallas_call(
        functools.partial(_step_attn_body, g=g, dil=dil, n_tok=n_tok),
        grid=(batch,),
        in_specs=[pl.BlockSpec((1, n_tok) + qkv.shape[2:], lambda b: (b, 0, 0, 0))] + past_specs,
        out_specs=[out, out],
        out_shape=[jax.ShapeDtypeStruct((batch, n_tok, C_HEADS, C_HD), F32)] * 2,
        compiler_params=_cparams("parallel"),
        name=f"step_attn_d{dil}",
    )(qkv, *([past] * n_res))


def _rope_tables(pos):
    half = C_HD // 2
    inv = ROPE_THETA ** (-jnp.arange(half, dtype=F32) / half)
    ang = pos.astype(F32)[:, None] * inv[None, :]
    cos, sin = jnp.cos(ang), jnp.sin(ang)
    return jnp.concatenate([cos, cos], -1), jnp.concatenate([-sin, sin], -1)


def _row_tile(m, cap):
    return min(m, cap)


def _hgrn_layer(x, batch, seq, s0, w_in, lb, ng, w_out, ln_g, ln_b):
    m = batch * seq
    bn = 1024
    p = _proj(x, w_in, (), (), functools.partial(_hgrn_proj_epilogue, bn=bn), m, _row_tile(m, PROJ_ROWS), bn,
              "hgrn_proj")
    if s0 is None:
        o, s_new = _hgrn_scan(p, lb, ng, batch, seq, min(seq, 512))
    else:
        rows = 8
        pp = jnp.pad(p.reshape(batch, seq, -1), ((0, 0), (0, rows - seq), (0, 0))).reshape(batch * rows, -1)
        o, s_new = _hgrn_step(pp, s0, lb, ng, batch, seq)
        o = o.reshape(batch, rows, -1)[:, :seq].reshape(m, -1)
    bm = _row_tile(m, 512)
    x = _out_ln((o,), (pl.BlockSpec((bm, o.shape[1]), lambda i: (i, 0)),), _cast_prologue,
                w_out, x, ln_g, ln_b, bm, "hgrn_out")
    return x, s_new


def _gmlp_layer(x, batch, seq, w_in, b_in, g1, b1, ws, bs, w_out, ln_g, ln_b):
    m = batch * seq
    bm = _row_tile(m, 512)
    vec = pl.BlockSpec((1, D_MODEL), lambda i, j: (0, 0))
    uv = _proj(x, w_in, (b_in, g1, b1),
               (pl.BlockSpec((1, D_MODEL), lambda i, j: (0, j)), vec, vec),
               _gmlp_proj_epilogue, m, bm, D_MODEL, "gmlp_proj")
    if seq % B_CHUNK == 0:
        chunk, ws_c, bs_t = B_CHUNK, ws, bs.T
    else:
        chunk = m
        eye = jnp.eye(batch, dtype=ws.dtype)
        ws_c = jnp.einsum("ab,hts->hatbs", eye, ws[:, :seq, :seq]).reshape(B_HEADS, m, m)
        bs_t = jnp.tile(bs[:, :seq].T, (batch, 1))
    x = _gmlp_out(uv, ws_c, bs_t, w_out, x, ln_g, ln_b, _row_tile(m, 256), chunk)
    return x, uv


def _attn_layer(x, batch, seq, caches, pos0, w_in, w_out, ln_g, ln_b):
    m = batch * seq
    bn = 1024
    bm = _row_tile(m, PROJ_ROWS)
    cos, sin = _rope_tables(pos0 + jnp.arange(seq, dtype=jnp.int32))
    cos, sin = jnp.tile(cos, (batch, 1)), jnp.tile(sin, (batch, 1))
    tab = pl.BlockSpec((bm, C_HD), lambda i, j: (i, 0))
    qkv = _proj(x, w_in, (cos, sin), (tab, tab), functools.partial(_attn_proj_epilogue, bn=bn), m, bm, bn,
                "attn_proj")
    qkv3 = qkv.reshape(batch, seq, C_QKV)
    outs, lses = [], []
    for g, (window, dil) in enumerate(C_GROUPS):
        if caches is None:
            o, lse = _band_attn(qkv, g, dil, batch, seq)
        else:
            o, lse = _step_attn(qkv.reshape(batch, seq, C_QKV // C_HD, C_HD), caches[g], g, window, dil,
                                batch, seq)
            o, lse = o.reshape(m, -1), lse.reshape(m, -1)
        outs.append(o)
        lses.append(lse)
    bm2 = _row_tile(m, 256)
    spec = pl.BlockSpec((bm2, C_HEADS * C_HD), lambda i: (i, 0))
    x = _out_ln(tuple(outs) + tuple(lses), (spec,) * 6, _merge_prologue, w_out, x, ln_g, ln_b, bm2,
                "attn_out")
    hw = C_HEADS * C_HD
    ng = len(C_GROUPS)
    kv = []
    for g, (window, _) in enumerate(C_GROUPS):
        tail = qkv3[:, seq - min(window, seq):]
        k = tail[:, :, (ng + g) * hw:(ng + g + 1) * hw].reshape(batch, -1, C_HEADS, C_HD)
        v = tail[:, :, (2 * ng + g) * hw:(2 * ng + g + 1) * hw].reshape(batch, -1, C_HEADS, C_HD)
        kv.append(jnp.stack([k, v], axis=2))
    return x, kv


def kernel(x_prompt, x_sample, state_hgrn, cache_c_kv_w128, cache_c_kv_w512, cache_c_kv_w2048, ln_g, ln_b, a_w_in, a_lb_logits, a_norm_g, a_w_out, b_w_in, b_b_in, b_ln_g, b_ln_b, b_w_s, b_b_s, b_w_out, c_w_in, c_w_out, moe_w_group, moe_w_expert, moe_w1, moe_w3, moe_w2):
    bp, tp, _ = x_prompt.shape
    bs, ts, _ = x_sample.shape
    assert tp % 512 == 0 and tp // C_GROUPS[-1][1] >= 2 * C_QBLOCK
    assert ts <= 8 and ts <= C_GROUPS[1][1] and (bp * tp) % MOE_TILE == 0

    lb_p = jax.nn.softmax(a_lb_logits.astype(F32), axis=0)
    lb_all = jnp.clip(jnp.cumsum(lb_p, axis=0) - lb_p[0:1], 0.0, 1.0 - 1e-6)
    caches = (cache_c_kv_w128, cache_c_kv_w512, cache_c_kv_w2048)

    xp = x_prompt.reshape(bp * tp, D_MODEL)
    xs = x_sample.reshape(bs * ts, D_MODEL)
    hgrn_p, hgrn_s, chunk_v_s = [], [], []
    kv_p = [[] for _ in C_GROUPS]
    kv_s = [[] for _ in C_GROUPS]

    for i in range(DEPTH):
        kind, j = i % 3, i // 3
        g0, b0 = ln_g[i, 0][None], ln_b[i, 0][None]
        if kind == 0:
            w_in, w_out = a_w_in[j].astype(BF16), a_w_out[j].astype(BF16)
            lb, ng = lb_all[j][None], a_norm_g[j][None]
            xp, sp = _hgrn_layer(xp, bp, tp, None, w_in, lb, ng, w_out, g0, b0)
            xs, ss = _hgrn_layer(xs, bs, ts, state_hgrn[j].astype(F32), w_in, lb, ng, w_out, g0, b0)
            hgrn_p.append(sp)
            hgrn_s.append(ss)
        elif kind == 1:
            w_in, w_out = b_w_in[j].astype(BF16), b_w_out[j].astype(BF16)
            args = (w_in, b_b_in[j][None], b_ln_g[j][None], b_ln_b[j][None], b_w_s[j], b_b_s[j], w_out, g0, b0)
            xp, _ = _gmlp_layer(xp, bp, tp, *args)
            xs, uvs = _gmlp_layer(xs, bs, ts, *args)
            chunk_v_s.append(uvs[:, D_MODEL:].reshape(bs, ts, D_MODEL))
        else:
            w_in, w_out = c_w_in[j].astype(BF16), c_w_out[j].astype(BF16)
            xp, kvp = _attn_layer(xp, bp, tp, None, 0, w_in, w_out, g0, b0)
            xs, kvs = _attn_layer(xs, bs, ts, tuple(c[j] for c in caches), PAST_LEN, w_in, w_out, g0, b0)
            for g in range(len(C_GROUPS)):
                kv_p[g].append(kvp[g])
                kv_s[g].append(kvs[g])
        wr = jnp.pad(jnp.concatenate([moe_w_expert[i], moe_w_group[i]], axis=1),
                     ((0, 0), (0, LANES - MOE_GE - MOE_GROUPS)))
        w1g, w3g = _group_up_weights(moe_w1[i]), _group_up_weights(moe_w3[i])
        w2g = moe_w2[i].astype(BF16).reshape(MOE_GROUPS, MOE_EXPERTS * MOE_FF, D_MODEL)
        g1, b1 = ln_g[i, 1][None], ln_b[i, 1][None]
        xp = _moe_dispatch(xp, bp * tp, wr, w1g, w3g, w2g, g1, b1, MOE_TILE)
        xs = _moe(xs, wr, w1g, w3g, w2g, g1, b1, bs * ts)

    return (xp[:bp * tp].reshape(bp, tp, D_MODEL), xs.reshape(bs, ts, D_MODEL),
            jnp.stack(hgrn_p), jnp.stack(hgrn_s), jnp.stack(chunk_v_s),
            jnp.stack(kv_p[0]), jnp.stack(kv_s[0]), jnp.stack(kv_p[1]), jnp.stack(kv_s[1]),
            jnp.stack(kv_p[2]), jnp.stack(kv_s[2]))
```

```python
import functools

import jax
import jax.numpy as jnp
from jax import lax
from jax.experimental import pallas as pl
from jax.experimental.pallas import tpu as pltpu

F32 = jnp.float32
BF16 = jnp.bfloat16
HIGHEST = lax.Precision.HIGHEST

D_MODEL = 2048
DEPTH = 4
PAST_LEN = 16384
A_HEADS = 16
A_DK = 128
A_DV = 128
A_WIDTH = A_HEADS * A_DK
HGRN_CHUNK = 64
HGRN_SUB = 8
LOG2E = 1.4426950408889634
HGRN_SUPER = 256
B_CHUNK = 128
B_HEADS = 16
B_HD = 128
C_HEADS = 8
C_HD = 128
C_GROUPS = ((128, 1), (512, 4), (2048, 16))
C_KEYS = 129
C_QBLOCK = 128
C_QKV = 3 * len(C_GROUPS) * C_HEADS * C_HD
ROPE_THETA = 10000.0
MOE_GROUPS = 4
MOE_EXPERTS = 4
MOE_GE = MOE_GROUPS * MOE_EXPERTS
MOE_FF = 256
MOE_TILE = 256
LN_EPS = 1e-5
RMS_EPS = 1e-6
ALPHA = (2 * DEPTH) ** 0.25
LANES = 128
VMEM_LIMIT = 56 * 1024 * 1024
PROJ_ROWS = 1024


def _cparams(*sem):
    return pltpu.CompilerParams(dimension_semantics=sem, vmem_limit_bytes=VMEM_LIMIT)


def _sigmoid(x):
    return 1.0 / (1.0 + jnp.exp(-x))


def _ln_rows(y, g, b):
    mu = jnp.mean(y, axis=-1, keepdims=True)
    d = y - mu
    var = jnp.mean(d * d, axis=-1, keepdims=True)
    return d * lax.rsqrt(var + LN_EPS) * g + b


def _nt_dot(a, b):
    return lax.dot_general(a, b, (((1,), (1,)), ((), ())), preferred_element_type=F32)


def _tn_dot(a, b):
    return lax.dot_general(a, b, (((0,), (0,)), ((), ())), preferred_element_type=F32)


def _proj_body(x_ref, w_ref, *rest, epilogue, n_extra):
    extras = rest[:n_extra]
    o_ref = rest[n_extra]
    xb_ref = rest[n_extra + 1]
    j = pl.program_id(1)

    @pl.when(j == 0)
    def _cast():
        xb_ref[...] = x_ref[...].astype(BF16)

    acc = jnp.dot(xb_ref[...], w_ref[...], preferred_element_type=F32)
    epilogue(acc, j, extras, o_ref)


def _proj(x, w, extras, extra_specs, epilogue, m, bm, bn, name):
    k = x.shape[1]
    n = w.shape[1]
    return pl.pallas_call(
        functools.partial(_proj_body, epilogue=epilogue, n_extra=len(extras)),
        grid=(m // bm, n // bn),
        in_specs=[pl.BlockSpec((bm, k), lambda i, j: (i, 0)),
                  pl.BlockSpec((k, bn), lambda i, j: (0, j))] + list(extra_specs),
        out_specs=pl.BlockSpec((bm, bn), lambda i, j: (i, j)),
        out_shape=jax.ShapeDtypeStruct((m, n), F32),
        scratch_shapes=[pltpu.VMEM((bm, k), BF16)],
        compiler_params=_cparams("parallel", "arbitrary"),
        name=name,
    )(x, w, *extras)


def _hgrn_proj_epilogue(acc, j, extras, o_ref, *, bn):
    nq = A_WIDTH // bn
    is_silu = jnp.logical_or(j < nq, j >= 3 * nq)

    @pl.when(is_silu)
    def _():
        o_ref[...] = acc * _sigmoid(acc)

    @pl.when(jnp.logical_not(is_silu))
    def _():
        o_ref[...] = acc


def _gelu_tanh(z):
    return 0.5 * z * (1.0 + jnp.tanh(0.7978845608028654 * (z + 0.044715 * (z * z * z))))


def _gmlp_proj_epilogue(acc, j, extras, o_ref):
    bias_ref, g_ref, b_ref = extras
    z = _gelu_tanh(acc + bias_ref[...])

    @pl.when(j == 0)
    def _():
        o_ref[...] = z

    @pl.when(j == 1)
    def _():
        o_ref[...] = _ln_rows(z, g_ref[...], b_ref[...])


def _attn_proj_epilogue(acc, j, extras, o_ref, *, bn):
    cos_ref, sin_ref = extras
    n_rot = 2 * len(C_GROUPS) * C_HEADS * C_HD // bn

    @pl.when(j < n_rot)
    def _():
        cos = cos_ref[...]
        sin = sin_ref[...]
        for h in range(bn // C_HD):
            xh = acc[:, h * C_HD:(h + 1) * C_HD]
            o_ref[:, h * C_HD:(h + 1) * C_HD] = xh * cos + pltpu.roll(xh, C_HD // 2, 1) * sin

    @pl.when(j >= n_rot)
    def _():
        o_ref[...] = acc


def _out_ln_body(*refs, prologue, n_in):
    ins = refs[:n_in]
    w_ref, r_ref, g_ref, b_ref, o_ref = refs[n_in:n_in + 5]
    a = prologue(*ins)
    acc = jnp.dot(a, w_ref[...], preferred_element_type=F32)
    y = ALPHA * r_ref[...] + acc
    o_ref[...] = _ln_rows(y, g_ref[...], b_ref[...])


def _out_ln(ins, in_specs, prologue, w, resid, g, b, bm, name):
    m = ins[0].shape[0]
    k = w.shape[0]
    row = pl.BlockSpec((bm, D_MODEL), lambda i: (i, 0))
    vec = pl.BlockSpec((1, D_MODEL), lambda i: (0, 0))
    return pl.pallas_call(
        functools.partial(_out_ln_body, prologue=prologue, n_in=len(ins)),
        grid=(m // bm,),
        in_specs=list(in_specs) + [pl.BlockSpec((k, D_MODEL), lambda i: (0, 0)), row, vec, vec],
        out_specs=row,
        out_shape=jax.ShapeDtypeStruct((m, D_MODEL), F32),
        compiler_params=_cparams("parallel"),
        name=name,
    )(*ins, w, resid, g, b)


def _cast_prologue(a_ref):
    return a_ref[...].astype(BF16)


def _merge_prologue(o0, o1, o2, l0, l1, l2):
    a0, a1, a2 = l0[...], l1[...], l2[...]
    mx = jnp.maximum(jnp.maximum(a0, a1), a2)
    e0, e1, e2 = jnp.exp(a0 - mx), jnp.exp(a1 - mx), jnp.exp(a2 - mx)
    o = (e0 * o0[...] + e1 * o1[...] + e2 * o2[...]) / (e0 + e1 + e2)
    return o.astype(BF16)


def _moe_gate(logits):
    lane = lax.broadcasted_iota(jnp.int32, logits.shape, 1).astype(F32)
    neg = -jnp.inf
    big = 4.0 * LANES
    gl = jnp.where((lane >= MOE_GE) & (lane < MOE_GE + MOE_GROUPS), logits, neg)
    gmax = jnp.max(gl, axis=-1, keepdims=True)
    g_idx = jnp.min(jnp.where(gl == gmax, lane - MOE_GE, big), axis=-1, keepdims=True)
    g_top = 1.0 / jnp.sum(jnp.exp(gl - gmax), axis=-1, keepdims=True)
    lo = g_idx * MOE_EXPERTS
    el = jnp.where((lane >= lo) & (lane < lo + MOE_EXPERTS), logits, neg)
    m1 = jnp.max(el, axis=-1, keepdims=True)
    i1 = jnp.min(jnp.where(el == m1, lane, big), axis=-1, keepdims=True)
    el2 = jnp.where(lane == i1, neg, el)
    m2 = jnp.max(el2, axis=-1, keepdims=True)
    i2 = jnp.min(jnp.where(el2 == m2, lane, big), axis=-1, keepdims=True)
    r = jnp.exp(m2 - m1)
    w1 = g_top / (1.0 + r)
    w2 = w1 * r
    return jnp.where(lane == i1, w1, 0.0) + jnp.where(lane == i2, w2, 0.0), g_idx


def _gate_columns(gate, first):
    lane = lax.broadcasted_iota(jnp.int32, gate.shape, 1)
    return jnp.concatenate(
        [jnp.broadcast_to(jnp.sum(jnp.where(lane == first + e, gate, 0.0), axis=-1, keepdims=True),
                          (gate.shape[0], MOE_FF)) for e in range(MOE_EXPERTS)], axis=1)


def _group_ffn(xb, gate, first, w1_ref, w3_ref, w2_ref):
    h1 = jnp.dot(xb, w1_ref[0], preferred_element_type=F32)
    h3 = jnp.dot(xb, w3_ref[0], preferred_element_type=F32)
    hg = (h1 * _sigmoid(h1) * h3 * _gate_columns(gate, first)).astype(BF16)
    return jnp.dot(hg, w2_ref[0], preferred_element_type=F32)


def _moe_body(x_ref, wr_ref, w1_ref, w3_ref, w2_ref, g_ref, b_ref, o_ref, xb_ref, gate_ref, acc_ref):
    gi = pl.program_id(1)

    @pl.when(gi == 0)
    def _route():
        x = x_ref[...]
        xb_ref[...] = x.astype(BF16)
        logits = jnp.dot(x, wr_ref[...], precision=HIGHEST, preferred_element_type=F32)
        gate_ref[...] = _moe_gate(logits)[0]
        acc_ref[...] = jnp.zeros_like(acc_ref)

    acc_ref[...] += _group_ffn(xb_ref[...], gate_ref[...], gi * MOE_EXPERTS, w1_ref, w3_ref, w2_ref)

    @pl.when(gi == MOE_GROUPS - 1)
    def _finish():
        y = ALPHA * x_ref[...] + acc_ref[...]
        o_ref[...] = _ln_rows(y, g_ref[...], b_ref[...])


def _moe(x, wr, w1g, w3g, w2g, g, b, bm):
    m = x.shape[0]
    eff = MOE_EXPERTS * MOE_FF
    row = pl.BlockSpec((bm, D_MODEL), lambda i, e: (i, 0))
    vec = pl.BlockSpec((1, D_MODEL), lambda i, e: (0, 0))
    return pl.pallas_call(
        _moe_body,
        grid=(m // bm, MOE_GROUPS),
        in_specs=[row,
                  pl.BlockSpec((D_MODEL, LANES), lambda i, e: (0, 0)),
                  pl.BlockSpec((1, D_MODEL, eff), lambda i, e: (e, 0, 0)),
                  pl.BlockSpec((1, D_MODEL, eff), lambda i, e: (e, 0, 0)),
                  pl.BlockSpec((1, eff, D_MODEL), lambda i, e: (e, 0, 0)),
                  vec, vec],
        out_specs=row,
        out_shape=jax.ShapeDtypeStruct((m, D_MODEL), F32),
        scratch_shapes=[pltpu.VMEM((bm, D_MODEL), BF16),
                        pltpu.VMEM((bm, LANES), F32),
                        pltpu.VMEM((bm, D_MODEL), F32)],
        compiler_params=_cparams("parallel", "arbitrary"),
        name="moe",
    )(x, wr, w1g, w3g, w2g, g, b)


def _router_body(x_ref, wr_ref, gate_ref, gidx_ref):
    x = x_ref[...]
    xh = x.astype(BF16)
    xl = (x - xh.astype(F32)).astype(BF16)
    a = jnp.dot(xh, wr_ref[...], preferred_element_type=F32)
    logits = a[:, :LANES] + a[:, LANES:] + jnp.dot(xl, wr_ref[:, :LANES], preferred_element_type=F32)
    gate, g_idx = _moe_gate(logits)
    gate_ref[...] = gate
    gidx_ref[...] = jnp.broadcast_to(g_idx, gidx_ref.shape)


def _router(x, wr, m, bm):
    out = pl.BlockSpec((bm, LANES), lambda i: (i, 0))
    return pl.pallas_call(
        _router_body,
        grid=(m // bm,),
        in_specs=[pl.BlockSpec((bm, D_MODEL), lambda i: (i, 0)),
                  pl.BlockSpec((D_MODEL, 2 * LANES), lambda i: (0, 0))],
        out_specs=[out, out],
        out_shape=[jax.ShapeDtypeStruct((m, LANES), F32)] * 2,
        compiler_params=_cparams("parallel"),
        name="moe_router",
    )(x, wr)


def _moe_sorted_body(tg_ref, dest_ref, x_hbm, gate_hbm, w1_ref, w3_ref, w2_ref, g_ref, b_ref, o_hbm,
                     xbuf, gbuf, obuf, src_ref, dst_ref, xsem, gsem, osem, *, bm, n):
    i = pl.program_id(0)
    last = pl.num_programs(0) - 1
    slot = i % 2
    shift = bm.bit_length() - 1

    def plan():
        def init(u, carry):
            src_ref[u] = 0
            dst_ref[u] = n + (((u >> shift) + 1) & 1) * bm + (u & (bm - 1))
            return carry

        def fill(t, carry):
            d = dest_ref[t]
            src_ref[d] = t
            dst_ref[bm + d] = t
            return carry

        lax.fori_loop(0, src_ref.shape[0], init, 0, unroll=8)
        lax.fori_loop(0, n, fill, 0, unroll=8)

    def gather(tile, s):
        for j in range(bm):
            r = src_ref[tile * bm + j]
            pltpu.make_async_copy(x_hbm.at[pl.ds(r, 1), :], xbuf.at[s, pl.ds(j, 1), :], xsem.at[s]).start()
            pltpu.make_async_copy(gate_hbm.at[pl.ds(r, 1), :], gbuf.at[s, pl.ds(j, 1), :], gsem.at[s]).start()

    def wait_gather(s):
        pltpu.make_async_copy(x_hbm.at[pl.ds(0, bm), :], xbuf.at[s], xsem.at[s]).wait()
        pltpu.make_async_copy(gate_hbm.at[pl.ds(0, bm), :], gbuf.at[s], gsem.at[s]).wait()

    def scatter(tile, s):
        for j in range(bm):
            r = dst_ref[(tile + 1) * bm + j]
            pltpu.make_async_copy(obuf.at[s, pl.ds(j, 1), :], o_hbm.at[pl.ds(r, 1), :], osem.at[s]).start()

    def wait_scatter(s):
        pltpu.make_async_copy(obuf.at[s], o_hbm.at[pl.ds(0, bm), :], osem.at[s]).wait()

    @pl.when(i == 0)
    def _first():
        plan()
        gather(0, 0)
        obuf[1] = jnp.zeros(obuf.shape[1:], F32)

    wait_gather(slot)

    @pl.when(i >= 1)
    def _reuse():
        wait_scatter(slot)

    gather(i + 1, 1 - slot)
    scatter(i - 1, 1 - slot)
    x = xbuf[slot]
    ffn = _group_ffn(x.astype(BF16), gbuf[slot], tg_ref[i] * MOE_EXPERTS, w1_ref, w3_ref, w2_ref)
    obuf[slot] = _ln_rows(ALPHA * x + ffn, g_ref[...], b_ref[...])

    @pl.when(i == last)
    def _drain():
        scatter(i, slot)
        wait_gather(1 - slot)
        wait_scatter(1 - slot)
        wait_scatter(slot)


def _moe_sorted(tile_group, dest, x, gate, w1g, w3g, w2g, g, b, bm):
    n_tiles = tile_group.shape[0]
    n = dest.shape[0]
    assert bm & (bm - 1) == 0
    eff = MOE_EXPERTS * MOE_FF
    vec = pl.BlockSpec((1, D_MODEL), lambda i, tg, de: (0, 0))
    hbm = pl.BlockSpec(memory_space=pl.ANY)
    return pl.pallas_call(
        functools.partial(_moe_sorted_body, bm=bm, n=n),
        grid_spec=pltpu.PrefetchScalarGridSpec(
            num_scalar_prefetch=2,
            grid=(n_tiles,),
            in_specs=[hbm, hbm,
                      pl.BlockSpec((1, D_MODEL, eff), lambda i, tg, de: (tg[i], 0, 0)),
                      pl.BlockSpec((1, D_MODEL, eff), lambda i, tg, de: (tg[i], 0, 0)),
                      pl.BlockSpec((1, eff, D_MODEL), lambda i, tg, de: (tg[i], 0, 0)),
                      vec, vec],
            out_specs=hbm,
            scratch_shapes=[pltpu.VMEM((2, bm, D_MODEL), F32),
                            pltpu.VMEM((2, bm, LANES), F32),
                            pltpu.VMEM((2, bm, D_MODEL), F32),
                            pltpu.SMEM(((n_tiles + 1) * bm,), jnp.int32),
                            pltpu.SMEM(((n_tiles + 1) * bm,), jnp.int32),
                            pltpu.SemaphoreType.DMA((2,)),
                            pltpu.SemaphoreType.DMA((2,)),
                            pltpu.SemaphoreType.DMA((2,))]),
        out_shape=jax.ShapeDtypeStruct((n + 2 * bm, D_MODEL), F32),
        compiler_params=_cparams("arbitrary"),
        name="moe_sorted",
    )(tile_group, dest, x, gate, w1g, w3g, w2g, g, b)


def _group_up_weights(w):
    w = w.astype(BF16).reshape(MOE_GROUPS, MOE_EXPERTS, D_MODEL, MOE_FF)
    return w.transpose(0, 2, 1, 3).reshape(MOE_GROUPS, D_MODEL, MOE_EXPERTS * MOE_FF)


def _moe_dispatch(x, n, wr, w1g, w3g, w2g, g, b, bm):
    wr_hi = wr.astype(BF16)
    wr_lo = (wr - wr_hi.astype(F32)).astype(BF16)
    gate, gidx = _router(x, jnp.concatenate([wr_hi, wr_lo], axis=1), n, _row_tile(n, 512))
    grp = gidx[:, 0].astype(jnp.int32)
    onehot = (grp[:, None] == jnp.arange(MOE_GROUPS, dtype=jnp.int32)[None, :]).astype(jnp.int32)
    counts = jnp.sum(onehot, axis=0)
    padded = (counts + bm - 1) // bm * bm
    ends = jnp.cumsum(padded)
    starts = ends - padded
    rank = jnp.cumsum(onehot, axis=0) - onehot
    dest = jnp.sum((starts[None, :] + rank) * onehot, axis=1)
    n_tiles = n // bm + MOE_GROUPS
    tile_start = jnp.arange(n_tiles, dtype=jnp.int32) * bm
    tile_group = jnp.minimum(jnp.sum(tile_start[:, None] >= ends[None, :], axis=1), MOE_GROUPS - 1)
    return _moe_sorted(tile_group.astype(jnp.int32), dest.astype(jnp.int32), x, gate, w1g, w3g, w2g, g, b, bm)


def _hgrn_gates(f, lb):
    log_sig = jnp.minimum(f, 0.0) - jnp.log1p(jnp.exp(-jnp.abs(f)))
    a = jnp.log1p(-lb) + log_sig
    log_lb = jnp.log(lb)
    log_f = jnp.maximum(log_lb, a) + jnp.log1p(jnp.exp(-jnp.abs(log_lb - a)))
    k = (1.0 - lb) / (1.0 + jnp.exp(f))
    return log_f, k


def _hgrn_finish(o, gate, ng):
    o = o * lax.rsqrt(jnp.mean(o * o, axis=-1, keepdims=True) + RMS_EPS) * ng
    return o * gate


def _rows_of(x, idx, n):
    return jnp.concatenate([jnp.broadcast_to(x[i:i + 1, :], (n, x.shape[1])) for i in idx], axis=0)


def _hgrn_scan_body(q_ref, f_ref, v_ref, g_ref, lb_ref, ng_ref, tri_ref, lm_ref, bsel_ref, dm_ref,
                    o_ref, s_ref, st_ref, *, n_super):
    C, c, N = HGRN_CHUNK, HGRN_SUB, HGRN_SUPER
    neg = -1e30
    tb = pl.program_id(2)

    @pl.when(tb == 0)
    def _init():
        st_ref[...] = jnp.zeros_like(st_ref)

    lb = lb_ref[...]
    ng = ng_ref[...]
    row = lax.broadcasted_iota(jnp.int32, (N, 1), 0)
    srow = lax.broadcasted_iota(jnp.int32, (c, 1), 0)
    levels = [(C >> l, C >> (l + 1)) for l in range(lm_ref.shape[0])]

    def super_chunk(si, carry):
        r0 = pl.multiple_of(si * N, N)
        q = q_ref[pl.ds(r0, N), :]
        v = v_ref[pl.ds(r0, N), :].astype(BF16)
        gate = g_ref[pl.ds(r0, N), :]
        log_f, k = _hgrn_gates(f_ref[pl.ds(r0, N), :], lb)

        hi = log_f.astype(BF16)
        r1 = log_f - hi.astype(F32)
        mid = r1.astype(BF16)
        lo = (r1 - mid.astype(F32)).astype(BF16)
        cs = jnp.dot(tri_ref[...], jnp.concatenate([hi, mid, lo], axis=1), preferred_element_type=F32)
        b2 = (cs[:, :A_DK] + cs[:, A_DK:2 * A_DK] + cs[:, 2 * A_DK:]) * LOG2E

        att_t = None
        for li, (blk, half) in enumerate(levels):
            up = (row & (blk - 1)) >= half
            ref = _rows_of(b2, [blk * m + half - 1 for m in range(N // blk)], blk)
            e = jnp.exp2(jnp.where(up, b2 - ref, ref - b2))
            ql = jnp.where(up, q * e, 0.0).astype(BF16)
            kl = jnp.where(up, 0.0, k * e).astype(BF16)
            term = _nt_dot(kl, ql) * lm_ref[li]
            att_t = term if att_t is None else att_t + term

        ys = []
        for i in range(N // c):
            base = c * i
            kb = k[base:base + c, :]
            bb = b2[base:base + c, :]
            units = []
            for tl in range(c):
                d = jnp.where(srow <= tl, b2[base + tl:base + tl + 1, :] - bb, neg)
                units.append(q[base + tl:base + tl + 1, :] * kb * jnp.exp2(d))
            ys.append(jnp.concatenate(units, axis=1))
        y = jnp.concatenate(ys, axis=0).astype(BF16)
        r = jnp.dot(y, bsel_ref[...], preferred_element_type=F32)
        diag = jnp.concatenate([r * dm_ref[0], r * dm_ref[1]], axis=1)
        o_intra = _tn_dot((att_t + diag).astype(BF16), v)

        bl = _rows_of(b2, [C * m + C - 1 for m in range(N // C)], C)
        qe = (q * jnp.exp2(b2)).astype(BF16)
        kd = (k * jnp.exp2(bl - b2)).astype(BF16)
        st = st_ref[...]
        for m in range(N // C):
            rows = slice(C * m, C * m + C)
            o = o_intra[rows, :] + _nt_dot(qe[rows, :], st.astype(BF16))
            o_ref[pl.ds(r0 + C * m, C), :] = _hgrn_finish(o, gate[rows, :], ng).astype(o_ref.dtype)
            st = st * jnp.exp2(b2[C * m + C - 1:C * m + C, :]) + _tn_dot(v[rows, :], kd[rows, :])
        st_ref[...] = st
        return carry

    for si in range(n_super):
        super_chunk(si, 0)

    @pl.when(tb == pl.num_programs(2) - 1)
    def _emit():
        s_ref[0, 0] = st_ref[...].T


def _hgrn_scan(p, lb, ng, batch, seq, tb):
    nt = seq // tb
    nh = A_HEADS
    n, c = HGRN_SUPER, HGRN_SUB
    assert n == 2 * LANES and LANES % c == 0
    s_i = jnp.arange(n)[:, None]
    t_i = jnp.arange(n)[None, :]
    tri = ((s_i >= t_i) & (s_i // HGRN_CHUNK == t_i // HGRN_CHUNK)).astype(BF16)
    blocks = []
    blk = HGRN_CHUNK
    while blk > c:
        blocks.append(blk)
        blk //= 2
    lm = jnp.stack([(s_i // bk == t_i // bk) for bk in blocks]).astype(F32)
    lane = jnp.arange(LANES)[None, :]
    bsel = (jnp.arange(c * A_DK)[:, None] // A_DK == lane % c).astype(BF16)
    dm = jnp.stack([(s_i // c == g * (LANES // c) + lane // c) for g in range(2)]).astype(F32)

    def col(off):
        return pl.BlockSpec((tb, A_DK), lambda b, h, t: (b * nt + t, off * nh + h))

    def const(a):
        return pl.BlockSpec(a.shape, lambda b, h, t: (0,) * a.ndim)

    return pl.pallas_call(
        functools.partial(_hgrn_scan_body, n_super=tb // n),
        grid=(batch, nh, nt),
        in_specs=[col(0), col(1), col(2), col(3),
                  pl.BlockSpec((1, A_DK), lambda b, h, t: (0, h)),
                  pl.BlockSpec((1, A_DV), lambda b, h, t: (0, 0)),
                  const(tri), const(lm), const(bsel), const(dm)],
        out_specs=[pl.BlockSpec((tb, A_DV), lambda b, h, t: (b * nt + t, h)),
                   pl.BlockSpec((1, 1, A_DK, A_DV), lambda b, h, t: (b, h, 0, 0))],
        out_shape=[jax.ShapeDtypeStruct((batch * seq, nh * A_DV), BF16),
                   jax.ShapeDtypeStruct((batch, nh, A_DK, A_DV), F32)],
        scratch_shapes=[pltpu.VMEM((A_DV, A_DK), F32)],
        compiler_params=_cparams("parallel", "parallel", "arbitrary"),
        name="hgrn_scan",
    )(p, p, p, p, lb, ng, tri, lm, bsel, dm)


def _hgrn_step_body(p_ref, s0_ref, lb_ref, ng_ref, o_ref, s_ref, *, n_tok):
    R = p_ref.shape[0]
    row = lax.broadcasted_iota(jnp.int32, (R, 1), 0)
    valid = row < n_tok
    r2 = lax.broadcasted_iota(jnp.int32, (R, R), 0)
    c2 = lax.broadcasted_iota(jnp.int32, (R, R), 1)
    tril = (r2 >= c2).astype(F32)
    ng = ng_ref[...]
    for h in range(A_HEADS):
        sl = slice(h * A_DK, (h + 1) * A_DK)
        q = p_ref[:, sl]
        v = p_ref[:, 2 * A_WIDTH + h * A_DV:2 * A_WIDTH + (h + 1) * A_DV]
        gate = p_ref[:, 3 * A_WIDTH + h * A_DV:3 * A_WIDTH + (h + 1) * A_DV]
        log_f, k = _hgrn_gates(p_ref[:, A_WIDTH + h * A_DK:A_WIDTH + (h + 1) * A_DK], lb_ref[:, sl])
        b = jnp.dot(tril, log_f, precision=HIGHEST, preferred_element_type=F32)
        st = s0_ref[0, h].T
        o = _nt_dot((q * jnp.exp(b)).astype(BF16), st.astype(BF16))
        for s in range(n_tok):
            m = row >= s
            w = jnp.where(m, q * k[s:s + 1, :] * jnp.exp(jnp.where(m, b - b[s:s + 1, :], 0.0)), 0.0)
            o = o + jnp.sum(w, axis=-1, keepdims=True) * v[s:s + 1, :]
        o_ref[:, h * A_DV:(h + 1) * A_DV] = _hgrn_finish(o, gate, ng)
        bl = b[n_tok - 1:n_tok, :]
        kd = jnp.where(valid, k * jnp.exp(jnp.where(valid, bl - b, 0.0)), 0.0)
        st_new = st * jnp.exp(bl) + _tn_dot(v.astype(BF16), kd.astype(BF16))
        s_ref[0, h] = st_new.T


def _hgrn_step(p, s0, lb, ng, batch, n_tok):
    rows = p.shape[0] // batch
    return pl.pallas_call(
        functools.partial(_hgrn_step_body, n_tok=n_tok),
        grid=(batch,),
        in_specs=[pl.BlockSpec((rows, 4 * A_WIDTH), lambda b: (b, 0)),
                  pl.BlockSpec((1, A_HEADS, A_DK, A_DV), lambda b: (b, 0, 0, 0)),
                  pl.BlockSpec((1, A_WIDTH), lambda b: (0, 0)),
                  pl.BlockSpec((1, A_DV), lambda b: (0, 0))],
        out_specs=[pl.BlockSpec((rows, A_HEADS * A_DV), lambda b: (b, 0)),
                   pl.BlockSpec((1, A_HEADS, A_DK, A_DV), lambda b: (b, 0, 0, 0))],
        out_shape=[jax.ShapeDtypeStruct((batch * rows, A_HEADS * A_DV), F32),
                   jax.ShapeDtypeStruct((batch, A_HEADS, A_DK, A_DV), F32)],
        compiler_params=_cparams("parallel"),
        name="hgrn_step",
    )(p, s0, lb, ng)


def _gmlp_prologue(u_ref, v_ref, ws_ref, bs_ref, gated_ref, *, chunk):
    bm = u_ref.shape[0]
    r2 = lax.broadcasted_iota(jnp.int32, (chunk, chunk), 0)
    c2 = lax.broadcasted_iota(jnp.int32, (chunk, chunk), 1)
    causal = r2 >= c2
    for h in range(B_HEADS):
        wc = jnp.where(causal, ws_ref[h], 0.0).astype(BF16)
        bias = bs_ref[:, h:h + 1]
        cols = slice(h * B_HD, (h + 1) * B_HD)
        for n in range(bm // chunk):
            rows = slice(n * chunk, (n + 1) * chunk)
            mixed = jnp.dot(wc, v_ref[rows, cols].astype(BF16), preferred_element_type=F32) + bias
            gated_ref[rows, cols] = (u_ref[rows, cols] * mixed).astype(BF16)
    return gated_ref[...]


def _gmlp_out_body(u_ref, v_ref, ws_ref, bs_ref, w_ref, r_ref, g_ref, b_ref, o_ref, gated_ref, *, chunk):
    a = _gmlp_prologue(u_ref, v_ref, ws_ref, bs_ref, gated_ref, chunk=chunk)
    acc = jnp.dot(a, w_ref[...], preferred_element_type=F32)
    y = ALPHA * r_ref[...] + acc
    o_ref[...] = _ln_rows(y, g_ref[...], b_ref[...])


def _gmlp_out(uv, ws, bs_t, w, resid, g, b, bm, chunk):
    m = uv.shape[0]
    row = pl.BlockSpec((bm, D_MODEL), lambda i: (i, 0))
    vec = pl.BlockSpec((1, D_MODEL), lambda i: (0, 0))
    return pl.pallas_call(
        functools.partial(_gmlp_out_body, chunk=chunk),
        grid=(m // bm,),
        in_specs=[pl.BlockSpec((bm, D_MODEL), lambda i: (i, 0)),
                  pl.BlockSpec((bm, D_MODEL), lambda i: (i, 1)),
                  pl.BlockSpec(ws.shape, lambda i: (0, 0, 0)),
                  pl.BlockSpec(bs_t.shape, lambda i: (0, 0)),
                  pl.BlockSpec((D_MODEL, D_MODEL), lambda i: (0, 0)),
                  row, vec, vec],
        out_specs=row,
        out_shape=jax.ShapeDtypeStruct((m, D_MODEL), F32),
        scratch_shapes=[pltpu.VMEM((bm, D_MODEL), BF16)],
        compiler_params=_cparams("parallel"),
        name="gmlp_out",
    )(uv, uv, ws, bs_t, w, resid, g, b)


def _band_attn_body(q_ref, k_ref, v_ref, o_ref, l_ref, *, dil, seq):
    bq = C_QBLOCK
    span = C_KEYS - 1
    scale = C_HD ** -0.5
    n_blocks = seq // dil // bq
    qi = lax.broadcasted_iota(jnp.int32, (bq, 2 * bq), 0)
    ki = lax.broadcasted_iota(jnp.int32, (bq, 2 * bq), 1)

    def rows(first, n):
        return pl.ds(first, n) if dil == 1 else pl.ds(first, n, stride=dil)

    for r in range(dil):
        for i in range(n_blocks):
            w = max(i - 1, 0)
            qs = rows(r + dil * bq * i, bq)
            ws = rows(r + dil * bq * w, 2 * bq)
            q = q_ref[qs, :].astype(BF16)
            kw = k_ref[ws, :].astype(BF16)
            vw = v_ref[ws, :].astype(BF16)
            s = _nt_dot(q, kw) * scale
            rel = bq * (i - w) + qi - ki
            s = jnp.where((rel >= 0) & (rel <= span), s, -jnp.inf)
            mx = jnp.max(s, axis=-1, keepdims=True)
            p = jnp.exp(s - mx)
            den = jnp.sum(p, axis=-1, keepdims=True)
            o_ref[qs, :] = jnp.dot(p.astype(BF16), vw, preferred_element_type=F32) / den
            l_ref[qs, :] = jnp.broadcast_to(mx + jnp.log(den), (bq, C_HD))


def _band_attn(qkv, g, dil, batch, seq):
    ng = len(C_GROUPS)

    def col(part):
        return pl.BlockSpec((seq, C_HD), lambda b, h: (b, (part * ng + g) * C_HEADS + h))

    out = pl.BlockSpec((seq, C_HD), lambda b, h: (b, h))
    return pl.pallas_call(
        functools.partial(_band_attn_body, dil=dil, seq=seq),
        grid=(batch, C_HEADS),
        in_specs=[col(0), col(1), col(2)],
        out_specs=[out, out],
        out_shape=[jax.ShapeDtypeStruct((batch * seq, C_HEADS * C_HD), F32)] * 2,
        compiler_params=_cparams("parallel", "parallel"),
        name=f"band_attn_d{dil}",
    )(qkv, qkv, qkv)


def _step_attn_body(qkv_ref, *refs, g, dil, n_tok):
    past_refs, (o_ref, l_ref) = refs[:-2], refs[-2:]
    scale = C_HD ** -0.5
    ng = len(C_GROUPS)
    hs = C_HEADS
    tok = lax.broadcasted_iota(jnp.int32, (n_tok, 1, 1), 0)
    prow = lax.broadcasted_iota(jnp.int32, (past_refs[0].shape[1], 1, 1), 0)
    kn = qkv_ref[0, :, (ng + g) * hs:(ng + g + 1) * hs, :]
    vn = qkv_ref[0, :, (2 * ng + g) * hs:(2 * ng + g + 1) * hs, :]
    for t in range(n_tok):
        past_ref = past_refs[t % dil]
        new_ok = (tok <= t) & (((t - tok) % dil) == 0)
        past_ok = prow >= (t if dil == 1 else 0)
        q = qkv_ref[0, t, g * hs:(g + 1) * hs, :][None]
        kp = past_ref[0, :, 0:hs, :]
        vp = past_ref[0, :, hs:2 * hs, :]
        sp = jnp.where(past_ok, jnp.sum(kp * q, axis=-1, keepdims=True) * scale, -jnp.inf)
        sn = jnp.where(new_ok, jnp.sum(kn * q, axis=-1, keepdims=True) * scale, -jnp.inf)
        mx = jnp.maximum(jnp.max(sp, axis=0, keepdims=True), jnp.max(sn, axis=0, keepdims=True))
        pp = jnp.exp(sp - mx)
        pn = jnp.exp(sn - mx)
        den = jnp.sum(pp, axis=0, keepdims=True) + jnp.sum(pn, axis=0, keepdims=True)
        o = (jnp.sum(pp * vp, axis=0, keepdims=True) + jnp.sum(pn * vn, axis=0, keepdims=True)) / den
        o_ref[0, t] = o[0]
        l_ref[0, t] = jnp.broadcast_to(mx + jnp.log(den), (1, hs, C_HD))[0]


def _step_attn(qkv, cache, g, window, dil, batch, n_tok):
    past = cache.reshape(batch, window // dil, dil, 2 * C_HEADS, C_HD)
    n_res = min(dil, n_tok)
    out = pl.BlockSpec((1, n_tok, C_HEADS, C_HD), lambda b: (b, 0, 0, 0))
    past_specs = [pl.BlockSpec((1, window // dil, None, 2 * C_HEADS, C_HD),
                               functools.partial(lambda b, r: (b, 0, r, 0, 0), r=r)) for r in range(n_res)]
    return pl.pallas_call(
        functools.partial(_step_attn_body, g=g, dil=dil, n_tok=n_tok),
        grid=(batch,),
        in_specs=[pl.BlockSpec((1, n_tok) + qkv.shape[2:], lambda b: (b, 0, 0, 0))] + past_specs,
        out_specs=[out, out],
        out_shape=[jax.ShapeDtypeStruct((batch, n_tok, C_HEADS, C_HD), F32)] * 2,
        compiler_params=_cparams("parallel"),
        name=f"step_attn_d{dil}",
    )(qkv, *([past] * n_res))


def _rope_tables(pos):
    half = C_HD // 2
    inv = ROPE_THETA ** (-jnp.arange(half, dtype=F32) / half)
    ang = pos.astype(F32)[:, None] * inv[None, :]
    cos, sin = jnp.cos(ang), jnp.sin(ang)
    return jnp.concatenate([cos, cos], -1), jnp.concatenate([-sin, sin], -1)


def _row_tile(m, cap):
    return min(m, cap)


def _hgrn_layer(x, batch, seq, s0, w_in, lb, ng, w_out, ln_g, ln_b):
    m = batch * seq
    bn = 1024
    p = _proj(x, w_in, (), (), functools.partial(_hgrn_proj_epilogue, bn=bn), m, _row_tile(m, PROJ_ROWS), bn,
              "hgrn_proj")
    if s0 is None:
        o, s_new = _hgrn_scan(p, lb, ng, batch, seq, min(seq, 512))
    else:
        rows = 8
        pp = jnp.pad(p.reshape(batch, seq, -1), ((0, 0), (0, rows - seq), (0, 0))).reshape(batch * rows, -1)
        o, s_new = _hgrn_step(pp, s0, lb, ng, batch, seq)
        o = o.reshape(batch, rows, -1)[:, :seq].reshape(m, -1)
    bm = _row_tile(m, 512)
    x = _out_ln((o,), (pl.BlockSpec((bm, o.shape[1]), lambda i: (i, 0)),), _cast_prologue,
                w_out, x, ln_g, ln_b, bm, "hgrn_out")
    return x, s_new


def _gmlp_layer(x, batch, seq, w_in, b_in, g1, b1, ws, bs, w_out, ln_g, ln_b):
    m = batch * seq
    bm = _row_tile(m, 512)
    vec = pl.BlockSpec((1, D_MODEL), lambda i, j: (0, 0))
    uv = _proj(x, w_in, (b_in, g1, b1),
               (pl.BlockSpec((1, D_MODEL), lambda i, j: (0, j)), vec, vec),
               _gmlp_proj_epilogue, m, bm, D_MODEL, "gmlp_proj")
    if seq % B_CHUNK == 0:
        chunk, ws_c, bs_t = B_CHUNK, ws, bs.T
    else:
        chunk = m
        eye = jnp.eye(batch, dtype=ws.dtype)
        ws_c = jnp.einsum("ab,hts->hatbs", eye, ws[:, :seq, :seq]).reshape(B_HEADS, m, m)
        bs_t = jnp.tile(bs[:, :seq].T, (batch, 1))
    x = _gmlp_out(uv, ws_c, bs_t, w_out, x, ln_g, ln_b, _row_tile(m, 256), chunk)
    return x, uv


def _attn_layer(x, batch, seq, caches, pos0, w_in, w_out, ln_g, ln_b):
    m = batch * seq
    bn = 1024
    bm = _row_tile(m, PROJ_ROWS)
    cos, sin = _rope_tables(pos0 + jnp.arange(seq, dtype=jnp.int32))
    cos, sin = jnp.tile(cos, (batch, 1)), jnp.tile(sin, (batch, 1))
    tab = pl.BlockSpec((bm, C_HD), lambda i, j: (i, 0))
    qkv = _proj(x, w_in, (cos, sin), (tab, tab), functools.partial(_attn_proj_epilogue, bn=bn), m, bm, bn,
                "attn_proj")
    qkv3 = qkv.reshape(batch, seq, C_QKV)
    outs, lses = [], []
    for g, (window, dil) in enumerate(C_GROUPS):
        if caches is None:
            o, lse = _band_attn(qkv, g, dil, batch, seq)
        else:
            o, lse = _step_attn(qkv.reshape(batch, seq, C_QKV // C_HD, C_HD), caches[g], g, window, dil,
                                batch, seq)
            o, lse = o.reshape(m, -1), lse.reshape(m, -1)
        outs.append(o)
        lses.append(lse)
    bm2 = _row_tile(m, 256)
    spec = pl.BlockSpec((bm2, C_HEADS * C_HD), lambda i: (i, 0))
    x = _out_ln(tuple(outs) + tuple(lses), (spec,) * 6, _merge_prologue, w_out, x, ln_g, ln_b, bm2,
                "attn_out")
    hw = C_HEADS * C_HD
    ng = len(C_GROUPS)
    kv = []
    for g, (window, _) in enumerate(C_GROUPS):
        tail = qkv3[:, seq - min(window, seq):]
        k = tail[:, :, (ng + g) * hw:(ng + g + 1) * hw].reshape(batch, -1, C_HEADS, C_HD)
        v = tail[:, :, (2 * ng + g) * hw:(2 * ng + g + 1) * hw].reshape(batch, -1, C_HEADS, C_HD)
        kv.append(jnp.stack([k, v], axis=2))
    return x, kv


def kernel(x_prompt, x_sample, state_hgrn, cache_c_kv_w128, cache_c_kv_w512, cache_c_kv_w2048, ln_g, ln_b, a_w_in, a_lb_logits, a_norm_g, a_w_out, b_w_in, b_b_in, b_ln_g, b_ln_b, b_w_s, b_b_s, b_w_out, c_w_in, c_w_out, moe_w_group, moe_w_expert, moe_w1, moe_w3, moe_w2):
    bp, tp, _ = x_prompt.shape
    bs, ts, _ = x_sample.shape
    assert tp % 512 == 0 and tp // C_GROUPS[-1][1] >= 2 * C_QBLOCK
    assert ts <= 8 and ts <= C_GROUPS[1][1] and (bp * tp) % MOE_TILE == 0

    lb_p = jax.nn.softmax(a_lb_logits.astype(F32), axis=0)
    lb_all = jnp.clip(jnp.cumsum(lb_p, axis=0) - lb_p[0:1], 0.0, 1.0 - 1e-6)
    caches = (cache_c_kv_w128, cache_c_kv_w512, cache_c_kv_w2048)

    xp = x_prompt.reshape(bp * tp, D_MODEL)
    xs = x_sample.reshape(bs * ts, D_MODEL)
    hgrn_p, hgrn_s, chunk_v_s = [], [], []
    kv_p = [[] for _ in C_GROUPS]
    kv_s = [[] for _ in C_GROUPS]

    for i in range(DEPTH):
        kind, j = i % 3, i // 3
        g0, b0 = ln_g[i, 0][None], ln_b[i, 0][None]
        if kind == 0:
            w_in, w_out = a_w_in[j].astype(BF16), a_w_out[j].astype(BF16)
            lb, ng = lb_all[j][None], a_norm_g[j][None]
            xp, sp = _hgrn_layer(xp, bp, tp, None, w_in, lb, ng, w_out, g0, b0)
            xs, ss = _hgrn_layer(xs, bs, ts, state_hgrn[j].astype(F32), w_in, lb, ng, w_out, g0, b0)
            hgrn_p.append(sp)
            hgrn_s.append(ss)
        elif kind == 1:
            w_in, w_out = b_w_in[j].astype(BF16), b_w_out[j].astype(BF16)
            args = (w_in, b_b_in[j][None], b_ln_g[j][None], b_ln_b[j][None], b_w_s[j], b_b_s[j], w_out, g0, b0)
            xp, _ = _gmlp_layer(xp, bp, tp, *args)
            xs, uvs = _gmlp_layer(xs, bs, ts, *args)
            chunk_v_s.append(uvs[:, D_MODEL:].reshape(bs, ts, D_MODEL))
        else:
            w_in, w_out = c_w_in[j].astype(BF16), c_w_out[j].astype(BF16)
            xp, kvp = _attn_layer(xp, bp, tp, None, 0, w_in, w_out, g0, b0)
            xs, kvs = _attn_layer(xs, bs, ts, tuple(c[j] for c in caches), PAST_LEN, w_in, w_out, g0, b0)
            for g in range(len(C_GROUPS)):
                kv_p[g].append(kvp[g])
                kv_s[g].append(kvs[g])
        wr = jnp.pad(jnp.concatenate([moe_w_expert[i], moe_w_group[i]], axis=1),
                     ((0, 0), (0, LANES - MOE_GE - MOE_GROUPS)))
        w1g, w3g = _group_up_weights(moe_w1[i]), _group_up_weights(moe_w3[i])
        w2g = moe_w2[i].astype(BF16).reshape(MOE_GROUPS, MOE_EXPERTS * MOE_FF, D_MODEL)
        g1, b1 = ln_g[i, 1][None], ln_b[i, 1][None]
        xp = _moe_dispatch(xp, bp * tp, wr, w1g, w3g, w2g, g1, b1, MOE_TILE)
        xs = _moe(xs, wr, w1g, w3g, w2g, g1, b1, bs * ts)

    return (xp[:bp * tp].reshape(bp, tp, D_MODEL), xs.reshape(bs, ts, D_MODEL),
            jnp.stack(hgrn_p), jnp.stack(hgrn_s), jnp.stack(chunk_v_s),
            jnp.stack(kv_p[0]), jnp.stack(kv_s[0]), jnp.stack(kv_p[1]), jnp.stack(kv_s[1]),
            jnp.stack(kv_p[2]), jnp.stack(kv_s[2]))
```

```python
import functools

import jax
import jax.numpy as jnp
from jax import lax
from jax.experimental import pallas as pl
from jax.experimental.pallas import tpu as pltpu

F32 = jnp.float32
BF16 = jnp.bfloat16
HIGHEST = lax.Precision.HIGHEST

D_MODEL = 2048
DEPTH = 4
PAST_LEN = 16384
A_HEADS = 16
A_DK = 128
A_DV = 128
A_WIDTH = A_HEADS * A_DK
HGRN_CHUNK = 64
HGRN_SUB = 8
LOG2E = 1.4426950408889634
HGRN_SUPER = 256
HGRN_ROWS = 1024
B_CHUNK = 128
B_HEADS = 16
B_HD = 128
C_HEADS = 8
C_HD = 128
C_GROUPS = ((128, 1), (512, 4), (2048, 16))
C_KEYS = 129
C_QBLOCK = 128
C_QKV = 3 * len(C_GROUPS) * C_HEADS * C_HD
ROPE_THETA = 10000.0
MOE_GROUPS = 4
MOE_EXPERTS = 4
MOE_GE = MOE_GROUPS * MOE_EXPERTS
MOE_FF = 256
MOE_TILE = 256
LN_EPS = 1e-5
RMS_EPS = 1e-6
ALPHA = (2 * DEPTH) ** 0.25
LANES = 128
VMEM_LIMIT = 56 * 1024 * 1024
PROJ_ROWS = 1024


def _cparams(*sem):
    return pltpu.CompilerParams(dimension_semantics=sem, vmem_limit_bytes=VMEM_LIMIT)


def _sigmoid(x):
    return 1.0 / (1.0 + jnp.exp(-x))


def _ln_rows(y, g, b):
    mu = jnp.mean(y, axis=-1, keepdims=True)
    d = y - mu
    var = jnp.mean(d * d, axis=-1, keepdims=True)
    return d * lax.rsqrt(var + LN_EPS) * g + b


def _nt_dot(a, b):
    return lax.dot_general(a, b, (((1,), (1,)), ((), ())), preferred_element_type=F32)


def _tn_dot(a, b):
    return lax.dot_general(a, b, (((0,), (0,)), ((), ())), preferred_element_type=F32)


def _proj_body(x_ref, w_ref, *rest, epilogue, n_extra):
    extras = rest[:n_extra]
    o_ref = rest[n_extra]
    xb_ref = rest[n_extra + 1]
    j = pl.program_id(1)

    @pl.when(j == 0)
    def _cast():
        xb_ref[...] = x_ref[...].astype(BF16)

    acc = jnp.dot(xb_ref[...], w_ref[...], preferred_element_type=F32)
    epilogue(acc, j, extras, o_ref)


def _proj(x, w, extras, extra_specs, epilogue, m, bm, bn, name):
    k = x.shape[1]
    n = w.shape[1]
    return pl.pallas_call(
        functools.partial(_proj_body, epilogue=epilogue, n_extra=len(extras)),
        grid=(m // bm, n // bn),
        in_specs=[pl.BlockSpec((bm, k), lambda i, j: (i, 0)),
                  pl.BlockSpec((k, bn), lambda i, j: (0, j))] + list(extra_specs),
        out_specs=pl.BlockSpec((bm, bn), lambda i, j: (i, j)),
        out_shape=jax.ShapeDtypeStruct((m, n), F32),
        scratch_shapes=[pltpu.VMEM((bm, k), BF16)],
        compiler_params=_cparams("parallel", "arbitrary"),
        name=name,
    )(x, w, *extras)


def _hgrn_proj_epilogue(acc, j, extras, o_ref, *, bn):
    nq = A_WIDTH // bn
    is_silu = jnp.logical_or(j < nq, j >= 3 * nq)

    @pl.when(is_silu)
    def _():
        o_ref[...] = acc * _sigmoid(acc)

    @pl.when(jnp.logical_not(is_silu))
    def _():
        o_ref[...] = acc


def _gelu_tanh(z):
    return 0.5 * z * (1.0 + jnp.tanh(0.7978845608028654 * (z + 0.044715 * (z * z * z))))


def _gmlp_proj_epilogue(acc, j, extras, o_ref):
    bias_ref, g_ref, b_ref = extras
    z = _gelu_tanh(acc + bias_ref[...])

    @pl.when(j == 0)
    def _():
        o_ref[...] = z

    @pl.when(j == 1)
    def _():
        o_ref[...] = _ln_rows(z, g_ref[...], b_ref[...])


def _attn_proj_epilogue(acc, j, extras, o_ref, *, bn):
    cos_ref, sin_ref = extras
    n_rot = 2 * len(C_GROUPS) * C_HEADS * C_HD // bn

    @pl.when(j < n_rot)
    def _():
        cos = cos_ref[...]
        sin = sin_ref[...]
        for h in range(bn // C_HD):
            xh = acc[:, h * C_HD:(h + 1) * C_HD]
            o_ref[:, h * C_HD:(h + 1) * C_HD] = xh * cos + pltpu.roll(xh, C_HD // 2, 1) * sin

    @pl.when(j >= n_rot)
    def _():
        o_ref[...] = acc


def _out_ln_body(*refs, prologue, n_in):
    ins = refs[:n_in]
    w_ref, r_ref, g_ref, b_ref, o_ref = refs[n_in:n_in + 5]
    a = prologue(*ins)
    acc = jnp.dot(a, w_ref[...], preferred_element_type=F32)
    y = ALPHA * r_ref[...] + acc
    o_ref[...] = _ln_rows(y, g_ref[...], b_ref[...])


def _out_ln(ins, in_specs, prologue, w, resid, g, b, bm, name):
    m = ins[0].shape[0]
    k = w.shape[0]
    row = pl.BlockSpec((bm, D_MODEL), lambda i: (i, 0))
    vec = pl.BlockSpec((1, D_MODEL), lambda i: (0, 0))
    return pl.pallas_call(
        functools.partial(_out_ln_body, prologue=prologue, n_in=len(ins)),
        grid=(m // bm,),
        in_specs=list(in_specs) + [pl.BlockSpec((k, D_MODEL), lambda i: (0, 0)), row, vec, vec],
        out_specs=row,
        out_shape=jax.ShapeDtypeStruct((m, D_MODEL), F32),
        compiler_params=_cparams("parallel"),
        name=name,
    )(*ins, w, resid, g, b)


def _cast_prologue(a_ref):
    return a_ref[...].astype(BF16)


def _merge_prologue(o0, o1, o2, l0, l1, l2):
    a0, a1, a2 = l0[...], l1[...], l2[...]
    mx = jnp.maximum(jnp.maximum(a0, a1), a2)
    e0, e1, e2 = jnp.exp(a0 - mx), jnp.exp(a1 - mx), jnp.exp(a2 - mx)
    o = (e0 * o0[...] + e1 * o1[...] + e2 * o2[...]) / (e0 + e1 + e2)
    return o.astype(BF16)


def _moe_gate(logits):
    lane = lax.broadcasted_iota(jnp.int32, logits.shape, 1).astype(F32)
    neg = -jnp.inf
    big = 4.0 * LANES
    gl = jnp.where((lane >= MOE_GE) & (lane < MOE_GE + MOE_GROUPS), logits, neg)
    gmax = jnp.max(gl, axis=-1, keepdims=True)
    g_idx = jnp.min(jnp.where(gl == gmax, lane - MOE_GE, big), axis=-1, keepdims=True)
    g_top = 1.0 / jnp.sum(jnp.exp(gl - gmax), axis=-1, keepdims=True)
    lo = g_idx * MOE_EXPERTS
    el = jnp.where((lane >= lo) & (lane < lo + MOE_EXPERTS), logits, neg)
    m1 = jnp.max(el, axis=-1, keepdims=True)
    i1 = jnp.min(jnp.where(el == m1, lane, big), axis=-1, keepdims=True)
    el2 = jnp.where(lane == i1, neg, el)
    m2 = jnp.max(el2, axis=-1, keepdims=True)
    i2 = jnp.min(jnp.where(el2 == m2, lane, big), axis=-1, keepdims=True)
    r = jnp.exp(m2 - m1)
    w1 = g_top / (1.0 + r)
    w2 = w1 * r
    return jnp.where(lane == i1, w1, 0.0) + jnp.where(lane == i2, w2, 0.0), g_idx


def _gate_columns(gate, first):
    lane = lax.broadcasted_iota(jnp.int32, gate.shape, 1)
    return jnp.concatenate(
        [jnp.broadcast_to(jnp.sum(jnp.where(lane == first + e, gate, 0.0), axis=-1, keepdims=True),
                          (gate.shape[0], MOE_FF)) for e in range(MOE_EXPERTS)], axis=1)


def _group_ffn(xb, gate, first, w1_ref, w3_ref, w2_ref):
    h1 = jnp.dot(xb, w1_ref[0], preferred_element_type=F32)
    h3 = jnp.dot(xb, w3_ref[0], preferred_element_type=F32)
    hg = (h1 * _sigmoid(h1) * h3 * _gate_columns(gate, first)).astype(BF16)
    return jnp.dot(hg, w2_ref[0], preferred_element_type=F32)


def _moe_body(x_ref, wr_ref, w1_ref, w3_ref, w2_ref, g_ref, b_ref, o_ref, xb_ref, gate_ref, acc_ref):
    gi = pl.program_id(1)

    @pl.when(gi == 0)
    def _route():
        x = x_ref[...]
        xb_ref[...] = x.astype(BF16)
        logits = jnp.dot(x, wr_ref[...], precision=HIGHEST, preferred_element_type=F32)
        gate_ref[...] = _moe_gate(logits)[0]
        acc_ref[...] = jnp.zeros_like(acc_ref)

    acc_ref[...] += _group_ffn(xb_ref[...], gate_ref[...], gi * MOE_EXPERTS, w1_ref, w3_ref, w2_ref)

    @pl.when(gi == MOE_GROUPS - 1)
    def _finish():
        y = ALPHA * x_ref[...] + acc_ref[...]
        o_ref[...] = _ln_rows(y, g_ref[...], b_ref[...])


def _moe(x, wr, w1g, w3g, w2g, g, b, bm):
    m = x.shape[0]
    eff = MOE_EXPERTS * MOE_FF
    row = pl.BlockSpec((bm, D_MODEL), lambda i, e: (i, 0))
    vec = pl.BlockSpec((1, D_MODEL), lambda i, e: (0, 0))
    return pl.pallas_call(
        _moe_body,
        grid=(m // bm, MOE_GROUPS),
        in_specs=[row,
                  pl.BlockSpec((D_MODEL, LANES), lambda i, e: (0, 0)),
                  pl.BlockSpec((1, D_MODEL, eff), lambda i, e: (e, 0, 0)),
                  pl.BlockSpec((1, D_MODEL, eff), lambda i, e: (e, 0, 0)),
                  pl.BlockSpec((1, eff, D_MODEL), lambda i, e: (e, 0, 0)),
                  vec, vec],
        out_specs=row,
        out_shape=jax.ShapeDtypeStruct((m, D_MODEL), F32),
        scratch_shapes=[pltpu.VMEM((bm, D_MODEL), BF16),
                        pltpu.VMEM((bm, LANES), F32),
                        pltpu.VMEM((bm, D_MODEL), F32)],
        compiler_params=_cparams("parallel", "arbitrary"),
        name="moe",
    )(x, wr, w1g, w3g, w2g, g, b)


def _router_body(x_ref, wr_ref, gate_ref, gidx_ref, cnt_ref):
    x = x_ref[...]
    xh = x.astype(BF16)
    xl = (x - xh.astype(F32)).astype(BF16)
    a = jnp.dot(xh, wr_ref[...], preferred_element_type=F32)
    logits = a[:, :LANES] + a[:, LANES:] + jnp.dot(xl, wr_ref[:, :LANES], preferred_element_type=F32)
    gate, g_idx = _moe_gate(logits)
    gate_ref[...] = gate
    bm = x.shape[0]
    t = lax.broadcasted_iota(jnp.int32, (bm, LANES), 0)
    lane = lax.broadcasted_iota(jnp.int32, (bm, LANES), 1)
    spread = jnp.where((t & (LANES - 1)) == lane, g_idx, 0.0).astype(BF16)
    r_i = lax.broadcasted_iota(jnp.int32, (bm // LANES, bm), 0)
    t_i = lax.broadcasted_iota(jnp.int32, (bm // LANES, bm), 1)
    sel = (jnp.right_shift(t_i, LANES.bit_length() - 1) == r_i).astype(BF16)
    gidx_ref[...] = jnp.dot(sel, spread, preferred_element_type=F32)
    onehot = (lane.astype(F32) == g_idx).astype(F32)
    cnt_ref[...] = jnp.broadcast_to(jnp.sum(onehot, axis=0, keepdims=True), cnt_ref.shape)


def _router(x, wr, m, bm):
    assert bm % (8 * LANES) == 0 and m % bm == 0
    rows = bm // LANES
    small = pl.BlockSpec((rows, LANES), lambda i: (i, 0))
    return pl.pallas_call(
        _router_body,
        grid=(m // bm,),
        in_specs=[pl.BlockSpec((bm, D_MODEL), lambda i: (i, 0)),
                  pl.BlockSpec((D_MODEL, 2 * LANES), lambda i: (0, 0))],
        out_specs=[pl.BlockSpec((bm, LANES), lambda i: (i, 0)), small, small],
        out_shape=[jax.ShapeDtypeStruct((m, LANES), F32),
                   jax.ShapeDtypeStruct((m // LANES, LANES), F32),
                   jax.ShapeDtypeStruct((m // LANES, LANES), F32)],
        compiler_params=_cparams("parallel"),
        name="moe_router",
    )(x, wr)


def _moe_sorted_body(tg_ref, grp_ref, start_ref, x_hbm, gate_hbm, w1_ref, w3_ref, w2_ref, g_ref, b_ref, o_hbm,
                     xbuf, gbuf, obuf, src_ref, dst_ref, pos_ref, xsem, gsem, osem, *, bm, n):
    i = pl.program_id(0)
    last = pl.num_programs(0) - 1
    slot = i % 2
    shift = bm.bit_length() - 1

    def plan():
        def init(u, carry):
            src_ref[u] = 0
            dst_ref[u] = n + (((u >> shift) + 1) & 1) * bm + (u & (bm - 1))
            return carry

        def fill(t, carry):
            g = grp_ref[t]
            d = pos_ref[g]
            pos_ref[g] = d + 1
            src_ref[d] = t
            dst_ref[bm + d] = t
            return carry

        for g in range(MOE_GROUPS):
            pos_ref[g] = start_ref[g]
        lax.fori_loop(0, src_ref.shape[0], init, 0, unroll=8)
        lax.fori_loop(0, n, fill, 0, unroll=8)

    def gather(tile, s):
        for j in range(bm):
            r = src_ref[tile * bm + j]
            pltpu.make_async_copy(x_hbm.at[pl.ds(r, 1), :], xbuf.at[s, pl.ds(j, 1), :], xsem.at[s]).start()
            pltpu.make_async_copy(gate_hbm.at[pl.ds(r, 1), :], gbuf.at[s, pl.ds(j, 1), :], gsem.at[s]).start()

    def wait_gather(s):
        pltpu.make_async_copy(x_hbm.at[pl.ds(0, bm), :], xbuf.at[s], xsem.at[s]).wait()
        pltpu.make_async_copy(gate_hbm.at[pl.ds(0, bm), :], gbuf.at[s], gsem.at[s]).wait()

    def scatter(tile, s):
        for j in range(bm):
            r = dst_ref[(tile + 1) * bm + j]
            pltpu.make_async_copy(obuf.at[s, pl.ds(j, 1), :], o_hbm.at[pl.ds(r, 1), :], osem.at[s]).start()

    def wait_scatter(s):
        pltpu.make_async_copy(obuf.at[s], o_hbm.at[pl.ds(0, bm), :], osem.at[s]).wait()

    @pl.when(i == 0)
    def _first():
        plan()
        gather(0, 0)
        obuf[1] = jnp.zeros(obuf.shape[1:], F32)

    wait_gather(slot)

    @pl.when(i >= 1)
    def _reuse():
        wait_scatter(slot)

    gather(i + 1, 1 - slot)
    scatter(i - 1, 1 - slot)
    x = xbuf[slot]
    ffn = _group_ffn(x.astype(BF16), gbuf[slot], tg_ref[i] * MOE_EXPERTS, w1_ref, w3_ref, w2_ref)
    obuf[slot] = _ln_rows(ALPHA * x + ffn, g_ref[...], b_ref[...])

    @pl.when(i == last)
    def _drain():
        scatter(i, slot)
        wait_gather(1 - slot)
        wait_scatter(1 - slot)
        wait_scatter(slot)


def _moe_sorted(tile_group, grp, starts, x, gate, w1g, w3g, w2g, g, b, bm):
    n_tiles = tile_group.shape[0]
    n = grp.shape[0]
    assert bm & (bm - 1) == 0
    eff = MOE_EXPERTS * MOE_FF
    vec = pl.BlockSpec((1, D_MODEL), lambda i, tg, gr, st: (0, 0))
    hbm = pl.BlockSpec(memory_space=pl.ANY)
    return pl.pallas_call(
        functools.partial(_moe_sorted_body, bm=bm, n=n),
        grid_spec=pltpu.PrefetchScalarGridSpec(
            num_scalar_prefetch=3,
            grid=(n_tiles,),
            in_specs=[hbm, hbm,
                      pl.BlockSpec((1, D_MODEL, eff), lambda i, tg, gr, st: (tg[i], 0, 0)),
                      pl.BlockSpec((1, D_MODEL, eff), lambda i, tg, gr, st: (tg[i], 0, 0)),
                      pl.BlockSpec((1, eff, D_MODEL), lambda i, tg, gr, st: (tg[i], 0, 0)),
                      vec, vec],
            out_specs=hbm,
            scratch_shapes=[pltpu.VMEM((2, bm, D_MODEL), F32),
                            pltpu.VMEM((2, bm, LANES), F32),
                            pltpu.VMEM((2, bm, D_MODEL), F32),
                            pltpu.SMEM(((n_tiles + 1) * bm,), jnp.int32),
                            pltpu.SMEM(((n_tiles + 1) * bm,), jnp.int32),
                            pltpu.SMEM((MOE_GROUPS,), jnp.int32),
                            pltpu.SemaphoreType.DMA((2,)),
                            pltpu.SemaphoreType.DMA((2,)),
                            pltpu.SemaphoreType.DMA((2,))]),
        out_shape=jax.ShapeDtypeStruct((n + 2 * bm, D_MODEL), F32),
        compiler_params=_cparams("arbitrary"),
        name="moe_sorted",
    )(tile_group, grp, starts, x, gate, w1g, w3g, w2g, g, b)


def _group_up_weights(w):
    w = w.astype(BF16).reshape(MOE_GROUPS, MOE_EXPERTS, D_MODEL, MOE_FF)
    return w.transpose(0, 2, 1, 3).reshape(MOE_GROUPS, D_MODEL, MOE_EXPERTS * MOE_FF)


def _moe_dispatch(x, n, wr, w1g, w3g, w2g, g, b, bm):
    wr_hi = wr.astype(BF16)
    wr_lo = (wr - wr_hi.astype(F32)).astype(BF16)
    rt = 8 * LANES
    gate, gidx, cnt = _router(x, jnp.concatenate([wr_hi, wr_lo], axis=1), n, rt)
    grp = gidx.astype(jnp.int32).reshape(n)
    counts = jnp.sum(cnt.reshape(n // rt, rt // LANES, LANES)[:, 0, :MOE_GROUPS], axis=0).astype(jnp.int32)
    padded = (counts + bm - 1) // bm * bm
    ends = jnp.cumsum(padded)
    starts = ends - padded
    n_tiles = n // bm + MOE_GROUPS
    tile_start = jnp.arange(n_tiles, dtype=jnp.int32) * bm
    tile_group = jnp.minimum(jnp.sum(tile_start[:, None] >= ends[None, :], axis=1), MOE_GROUPS - 1)
    return _moe_sorted(tile_group.astype(jnp.int32), grp, starts.astype(jnp.int32), x, gate,
                       w1g, w3g, w2g, g, b, bm)


def _hgrn_gates(f, lb):
    log_sig = jnp.minimum(f, 0.0) - jnp.log1p(jnp.exp(-jnp.abs(f)))
    a = jnp.log1p(-lb) + log_sig
    log_lb = jnp.log(lb)
    log_f = jnp.maximum(log_lb, a) + jnp.log1p(jnp.exp(-jnp.abs(log_lb - a)))
    k = (1.0 - lb) / (1.0 + jnp.exp(f))
    return log_f, k


def _hgrn_finish(o, gate, ng):
    o = o * lax.rsqrt(jnp.mean(o * o, axis=-1, keepdims=True) + RMS_EPS) * ng
    return o * gate


def _rows_of(x, idx, n):
    return jnp.concatenate([jnp.broadcast_to(x[i:i + 1, :], (n, x.shape[1])) for i in idx], axis=0)


def _hgrn_scan_body(q_ref, f_ref, v_ref, g_ref, lb_ref, ng_ref, tri_ref, lm_ref, bsel_ref, dm_ref,
                    o_ref, s_ref, st_ref, *, n_super):
    C, c, N = HGRN_CHUNK, HGRN_SUB, HGRN_SUPER
    neg = -1e30
    tb = pl.program_id(2)

    @pl.when(tb == 0)
    def _init():
        st_ref[...] = jnp.zeros_like(st_ref)

    lb = lb_ref[...]
    ng = ng_ref[...]
    row = lax.broadcasted_iota(jnp.int32, (N, 1), 0)
    srow = lax.broadcasted_iota(jnp.int32, (c, 1), 0)
    levels = [(C >> l, C >> (l + 1)) for l in range(lm_ref.shape[0])]

    def super_chunk(si, carry):
        r0 = pl.multiple_of(si * N, N)
        q = q_ref[pl.ds(r0, N), :]
        v = v_ref[pl.ds(r0, N), :].astype(BF16)
        gate = g_ref[pl.ds(r0, N), :]
        log_f, k = _hgrn_gates(f_ref[pl.ds(r0, N), :], lb)

        hi = log_f.astype(BF16)
        r1 = log_f - hi.astype(F32)
        mid = r1.astype(BF16)
        lo = (r1 - mid.astype(F32)).astype(BF16)
        cs = jnp.dot(tri_ref[...], jnp.concatenate([hi, mid, lo], axis=1), preferred_element_type=F32)
        b2 = (cs[:, :A_DK] + cs[:, A_DK:2 * A_DK] + cs[:, 2 * A_DK:]) * LOG2E

        att_t = None
        for li, (blk, half) in enumerate(levels):
            up = (row & (blk - 1)) >= half
            ref = _rows_of(b2, [blk * m + half - 1 for m in range(N // blk)], blk)
            e = jnp.exp2(jnp.where(up, b2 - ref, ref - b2))
            ql = jnp.where(up, q * e, 0.0).astype(BF16)
            kl = jnp.where(up, 0.0, k * e).astype(BF16)
            term = _nt_dot(kl, ql) * lm_ref[li]
            att_t = term if att_t is None else att_t + term

        ys = []
        for i in range(N // c):
            base = c * i
            kb = k[base:base + c, :]
            bb = b2[base:base + c, :]
            units = []
            for tl in range(c):
                d = jnp.where(srow <= tl, b2[base + tl:base + tl + 1, :] - bb, neg)
                units.append(q[base + tl:base + tl + 1, :] * kb * jnp.exp2(d))
            ys.append(jnp.concatenate(units, axis=1))
        y = jnp.concatenate(ys, axis=0).astype(BF16)
        r = jnp.dot(y, bsel_ref[...], preferred_element_type=F32)
        diag = jnp.concatenate([r * dm_ref[0], r * dm_ref[1]], axis=1)
        o_intra = _tn_dot((att_t + diag).astype(BF16), v)

        bl = _rows_of(b2, [C * m + C - 1 for m in range(N // C)], C)
        qe = (q * jnp.exp2(b2)).astype(BF16)
        kd = (k * jnp.exp2(bl - b2)).astype(BF16)
        st = st_ref[...]
        nc = N // C
        states, lhs = [], []
        zero = jnp.zeros((C, A_DK), BF16)
        for m in range(nc):
            rows = slice(C * m, C * m + C)
            states.append(st.astype(BF16))
            lhs.append(jnp.concatenate([qe[rows, :] if j == m else zero for j in range(nc)], axis=1))
            st = st * jnp.exp2(b2[C * m + C - 1:C * m + C, :]) + _tn_dot(v[rows, :], kd[rows, :])
        st_ref[...] = st
        o = o_intra + _nt_dot(jnp.concatenate(lhs, axis=0), jnp.concatenate(states, axis=1))
        o_ref[pl.ds(r0, N), :] = _hgrn_finish(o, gate, ng).astype(o_ref.dtype)
        return carry

    for si in range(n_super):
        super_chunk(si, 0)

    @pl.when(tb == pl.num_programs(2) - 1)
    def _emit():
        s_ref[0, 0] = st_ref[...].T


def _hgrn_scan(p, lb, ng, batch, seq, tb):
    nt = seq // tb
    nh = A_HEADS
    n, c = HGRN_SUPER, HGRN_SUB
    assert n == 2 * LANES and LANES % c == 0
    s_i = jnp.arange(n)[:, None]
    t_i = jnp.arange(n)[None, :]
    tri = ((s_i >= t_i) & (s_i // HGRN_CHUNK == t_i // HGRN_CHUNK)).astype(BF16)
    blocks = []
    blk = HGRN_CHUNK
    while blk > c:
        blocks.append(blk)
        blk //= 2
    lm = jnp.stack([(s_i // bk == t_i // bk) for bk in blocks]).astype(F32)
    lane = jnp.arange(LANES)[None, :]
    bsel = (jnp.arange(c * A_DK)[:, None] // A_DK == lane % c).astype(BF16)
    dm = jnp.stack([(s_i // c == g * (LANES // c) + lane // c) for g in range(2)]).astype(F32)

    def col(off):
        return pl.BlockSpec((tb, A_DK), lambda b, h, t: (b * nt + t, off * nh + h))

    def const(a):
        return pl.BlockSpec(a.shape, lambda b, h, t: (0,) * a.ndim)

    return pl.pallas_call(
        functools.partial(_hgrn_scan_body, n_super=tb // n),
        grid=(batch, nh, nt),
        in_specs=[col(0), col(1), col(2), col(3),
                  pl.BlockSpec((1, A_DK), lambda b, h, t: (0, h)),
                  pl.BlockSpec((1, A_DV), lambda b, h, t: (0, 0)),
                  const(tri), const(lm), const(bsel), const(dm)],
        out_specs=[pl.BlockSpec((tb, A_DV), lambda b, h, t: (b * nt + t, h)),
                   pl.BlockSpec((1, 1, A_DK, A_DV), lambda b, h, t: (b, h, 0, 0))],
        out_shape=[jax.ShapeDtypeStruct((batch * seq, nh * A_DV), BF16),
                   jax.ShapeDtypeStruct((batch, nh, A_DK, A_DV), F32)],
        scratch_shapes=[pltpu.VMEM((A_DV, A_DK), F32)],
        compiler_params=_cparams("parallel", "parallel", "arbitrary"),
        name="hgrn_scan",
    )(p, p, p, p, lb, ng, tri, lm, bsel, dm)


def _hgrn_step_body(p_ref, s0_ref, lb_ref, ng_ref, o_ref, s_ref, *, n_tok):
    R = p_ref.shape[0]
    row = lax.broadcasted_iota(jnp.int32, (R, 1), 0)
    valid = row < n_tok
    r2 = lax.broadcasted_iota(jnp.int32, (R, R), 0)
    c2 = lax.broadcasted_iota(jnp.int32, (R, R), 1)
    tril = (r2 >= c2).astype(F32)
    ng = ng_ref[...]
    for h in range(A_HEADS):
        sl = slice(h * A_DK, (h + 1) * A_DK)
        q = p_ref[:, sl]
        v = p_ref[:, 2 * A_WIDTH + h * A_DV:2 * A_WIDTH + (h + 1) * A_DV]
        gate = p_ref[:, 3 * A_WIDTH + h * A_DV:3 * A_WIDTH + (h + 1) * A_DV]
        log_f, k = _hgrn_gates(p_ref[:, A_WIDTH + h * A_DK:A_WIDTH + (h + 1) * A_DK], lb_ref[:, sl])
        b = jnp.dot(tril, log_f, precision=HIGHEST, preferred_element_type=F32)
        st = s0_ref[0, h].T
        o = _nt_dot((q * jnp.exp(b)).astype(BF16), st.astype(BF16))
        for s in range(n_tok):
            m = row >= s
            w = jnp.where(m, q * k[s:s + 1, :] * jnp.exp(jnp.where(m, b - b[s:s + 1, :], 0.0)), 0.0)
            o = o + jnp.sum(w, axis=-1, keepdims=True) * v[s:s + 1, :]
        o_ref[:, h * A_DV:(h + 1) * A_DV] = _hgrn_finish(o, gate, ng)
        bl = b[n_tok - 1:n_tok, :]
        kd = jnp.where(valid, k * jnp.exp(jnp.where(valid, bl - b, 0.0)), 0.0)
        st_new = st * jnp.exp(bl) + _tn_dot(v.astype(BF16), kd.astype(BF16))
        s_ref[0, h] = st_new.T


def _hgrn_step(p, s0, lb, ng, batch, n_tok):
    rows = p.shape[0] // batch
    return pl.pallas_call(
        functools.partial(_hgrn_step_body, n_tok=n_tok),
        grid=(batch,),
        in_specs=[pl.BlockSpec((rows, 4 * A_WIDTH), lambda b: (b, 0)),
                  pl.BlockSpec((1, A_HEADS, A_DK, A_DV), lambda b: (b, 0, 0, 0)),
                  pl.BlockSpec((1, A_WIDTH), lambda b: (0, 0)),
                  pl.BlockSpec((1, A_DV), lambda b: (0, 0))],
        out_specs=[pl.BlockSpec((rows, A_HEADS * A_DV), lambda b: (b, 0)),
                   pl.BlockSpec((1, A_HEADS, A_DK, A_DV), lambda b: (b, 0, 0, 0))],
        out_shape=[jax.ShapeDtypeStruct((batch * rows, A_HEADS * A_DV), F32),
                   jax.ShapeDtypeStruct((batch, A_HEADS, A_DK, A_DV), F32)],
        compiler_params=_cparams("parallel"),
        name="hgrn_step",
    )(p, s0, lb, ng)


def _gmlp_prologue(u_ref, v_ref, ws_ref, bs_ref, gated_ref, *, chunk):
    bm = u_ref.shape[0]
    r2 = lax.broadcasted_iota(jnp.int32, (chunk, chunk), 0)
    c2 = lax.broadcasted_iota(jnp.int32, (chunk, chunk), 1)
    causal = r2 >= c2
    for h in range(B_HEADS):
        wc = jnp.where(causal, ws_ref[h], 0.0).astype(BF16)
        bias = bs_ref[:, h:h + 1]
        cols = slice(h * B_HD, (h + 1) * B_HD)
        for n in range(bm // chunk):
            rows = slice(n * chunk, (n + 1) * chunk)
            mixed = jnp.dot(wc, v_ref[rows, cols].astype(BF16), preferred_element_type=F32) + bias
            gated_ref[rows, cols] = (u_ref[rows, cols] * mixed).astype(BF16)
    return gated_ref[...]


def _gmlp_out_body(u_ref, v_ref, ws_ref, bs_ref, w_ref, r_ref, g_ref, b_ref, o_ref, gated_ref, *, chunk):
    a = _gmlp_prologue(u_ref, v_ref, ws_ref, bs_ref, gated_ref, chunk=chunk)
    acc = jnp.dot(a, w_ref[...], preferred_element_type=F32)
    y = ALPHA * r_ref[...] + acc
    o_ref[...] = _ln_rows(y, g_ref[...], b_ref[...])


def _gmlp_out(uv, ws, bs_t, w, resid, g, b, bm, chunk):
    m = uv.shape[0]
    row = pl.BlockSpec((bm, D_MODEL), lambda i: (i, 0))
    vec = pl.BlockSpec((1, D_MODEL), lambda i: (0, 0))
    return pl.pallas_call(
        functools.partial(_gmlp_out_body, chunk=chunk),
        grid=(m // bm,),
        in_specs=[pl.BlockSpec((bm, D_MODEL), lambda i: (i, 0)),
                  pl.BlockSpec((bm, D_MODEL), lambda i: (i, 1)),
                  pl.BlockSpec(ws.shape, lambda i: (0, 0, 0)),
                  pl.BlockSpec(bs_t.shape, lambda i: (0, 0)),
                  pl.BlockSpec((D_MODEL, D_MODEL), lambda i: (0, 0)),
                  row, vec, vec],
        out_specs=row,
        out_shape=jax.ShapeDtypeStruct((m, D_MODEL), F32),
        scratch_shapes=[pltpu.VMEM((bm, D_MODEL), BF16)],
        compiler_params=_cparams("parallel"),
        name="gmlp_out",
    )(uv, uv, ws, bs_t, w, resid, g, b)


def _band_attn_body(q_ref, k_ref, v_ref, o_ref, l_ref, *, dil, seq):
    bq = C_QBLOCK
    span = C_KEYS - 1
    scale = C_HD ** -0.5
    n_blocks = seq // dil // bq
    qi = lax.broadcasted_iota(jnp.int32, (bq, 2 * bq), 0)
    ki = lax.broadcasted_iota(jnp.int32, (bq, 2 * bq), 1)

    def rows(first, n):
        return pl.ds(first, n) if dil == 1 else pl.ds(first, n, stride=dil)

    for r in range(dil):
        for i in range(n_blocks):
            w = max(i - 1, 0)
            qs = rows(r + dil * bq * i, bq)
            ws = rows(r + dil * bq * w, 2 * bq)
            q = q_ref[qs, :].astype(BF16)
            kw = k_ref[ws, :].astype(BF16)
            vw = v_ref[ws, :].astype(BF16)
            s = _nt_dot(q, kw) * scale
            rel = bq * (i - w) + qi - ki
            s = jnp.where((rel >= 0) & (rel <= span), s, -jnp.inf)
            mx = jnp.max(s, axis=-1, keepdims=True)
            p = jnp.exp(s - mx)
            den = jnp.sum(p, axis=-1, keepdims=True)
            o_ref[qs, :] = jnp.dot(p.astype(BF16), vw, preferred_element_type=F32) / den
            l_ref[qs, :] = jnp.broadcast_to(mx + jnp.log(den), (bq, C_HD))


def _band_attn(qkv, g, dil, batch, seq):
    ng = len(C_GROUPS)

    def col(part):
        return pl.BlockSpec((seq, C_HD), lambda b, h: (b, (part * ng + g) * C_HEADS + h))

    out = pl.BlockSpec((seq, C_HD), lambda b, h: (b, h))
    return pl.pallas_call(
        functools.partial(_band_attn_body, dil=dil, seq=seq),
        grid=(batch, C_HEADS),
        in_specs=[col(0), col(1), col(2)],
        out_specs=[out, out],
        out_shape=[jax.ShapeDtypeStruct((batch * seq, C_HEADS * C_HD), F32)] * 2,
        compiler_params=_cparams("parallel", "parallel"),
        name=f"band_attn_d{dil}",
    )(qkv, qkv, qkv)


def _step_attn_body(qkv_ref, *refs, g, dil, n_tok):
    past_refs, (o_ref, l_ref) = refs[:-2], refs[-2:]
    scale = C_HD ** -0.5
    ng = len(C_GROUPS)
    hs = C_HEADS
    tok = lax.broadcasted_iota(jnp.int32, (n_tok, 1, 1), 0)
    prow = lax.broadcasted_iota(jnp.int32, (past_refs[0].shape[1], 1, 1), 0)
    kn = qkv_ref[0, :, (ng + g) * hs:(ng + g + 1) * hs, :]
    vn = qkv_ref[0, :, (2 * ng + g) * hs:(2 * ng + g + 1) * hs, :]
    for t in range(n_tok):
        past_ref = past_refs[t % dil]
        new_ok = (tok <= t) & (((t - tok) % dil) == 0)
        past_ok = prow >= (t if dil == 1 else 0)
        q = qkv_ref[0, t, g * hs:(g + 1) * hs, :][None]
        kp = past_ref[0, :, 0:hs, :]
        vp = past_ref[0, :, hs:2 * hs, :]
        sp = jnp.where(past_ok, jnp.sum(kp * q, axis=-1, keepdims=True) * scale, -jnp.inf)
        sn = jnp.where(new_ok, jnp.sum(kn * q, axis=-1, keepdims=True) * scale, -jnp.inf)
        mx = jnp.maximum(jnp.max(sp, axis=0, keepdims=True), jnp.max(sn, axis=0, keepdims=True))
        pp = jnp.exp(sp - mx)
        pn = jnp.exp(sn - mx)
        den = jnp.sum(pp, axis=0, keepdims=True) + jnp.sum(pn, axis=0, keepdims=True)
        o = (jnp.sum(pp * vp, axis=0, keepdims=True) + jnp.sum(pn * vn, axis=0, keepdims=True)) / den
        o_ref[0, t] = o[0]
        l_ref[0, t] = jnp.broadcast_to(mx + jnp.log(den), (1, hs, C_HD))[0]


def _step_attn(qkv, cache, g, window, dil, batch, n_tok):
    past = cache.reshape(batch, window // dil, dil, 2 * C_HEADS, C_HD)
    n_res = min(dil, n_tok)
    out = pl.BlockSpec((1, n_tok, C_HEADS, C_HD), lambda b: (b, 0, 0, 0))
    past_specs = [pl.BlockSpec((1, window // dil, None, 2 * C_HEADS, C_HD),
                               functools.partial(lambda b, r: (b, 0, r, 0, 0), r=r)) for r in range(n_res)]
    return pl.pallas_call(
        functools.partial(_step_attn_body, g=g, dil=dil, n_tok=n_tok),
        grid=(batch,),
        in_specs=[pl.BlockSpec((1, n_tok) + qkv.shape[2:], lambda b: (b, 0, 0, 0))] + past_specs,
        out_specs=[out, out],
        out_shape=[jax.ShapeDtypeStruct((batch, n_tok, C_HEADS, C_HD), F32)] * 2,
        compiler_params=_cparams("parallel"),
        name=f"step_attn_d{dil}",
    )(qkv, *([past] * n_res))


def _rope_tables(pos):
    half = C_HD // 2
    inv = ROPE_THETA ** (-jnp.arange(half, dtype=F32) / half)
    ang = pos.astype(F32)[:, None] * inv[None, :]
    cos, sin = jnp.cos(ang), jnp.sin(ang)
    return jnp.concatenate([cos, cos], -1), jnp.concatenate([-sin, sin], -1)


def _row_tile(m, cap):
    return min(m, cap)


def _hgrn_layer(x, batch, seq, s0, w_in, lb, ng, w_out, ln_g, ln_b):
    m = batch * seq
    bn = 1024
    p = _proj(x, w_in, (), (), functools.partial(_hgrn_proj_epilogue, bn=bn), m, _row_tile(m, PROJ_ROWS), bn,
              "hgrn_proj")
    if s0 is None:
        o, s_new = _hgrn_scan(p, lb, ng, batch, seq, HGRN_ROWS if seq % HGRN_ROWS == 0 else HGRN_SUPER)
    else:
        rows = 8
        pp = jnp.pad(p.reshape(batch, seq, -1), ((0, 0), (0, rows - seq), (0, 0))).reshape(batch * rows, -1)
        o, s_new = _hgrn_step(pp, s0, lb, ng, batch, seq)
        o = o.reshape(batch, rows, -1)[:, :seq].reshape(m, -1)
    bm = _row_tile(m, 512)
    x = _out_ln((o,), (pl.BlockSpec((bm, o.shape[1]), lambda i: (i, 0)),), _cast_prologue,
                w_out, x, ln_g, ln_b, bm, "hgrn_out")
    return x, s_new


def _gmlp_layer(x, batch, seq, w_in, b_in, g1, b1, ws, bs, w_out, ln_g, ln_b):
    m = batch * seq
    bm = _row_tile(m, 512)
    vec = pl.BlockSpec((1, D_MODEL), lambda i, j: (0, 0))
    uv = _proj(x, w_in, (b_in, g1, b1),
               (pl.BlockSpec((1, D_MODEL), lambda i, j: (0, j)), vec, vec),
               _gmlp_proj_epilogue, m, bm, D_MODEL, "gmlp_proj")
    if seq % B_CHUNK == 0:
        chunk, ws_c, bs_t = B_CHUNK, ws, bs.T
    else:
        chunk = m
        eye = jnp.eye(batch, dtype=ws.dtype)
        ws_c = jnp.einsum("ab,hts->hatbs", eye, ws[:, :seq, :seq]).reshape(B_HEADS, m, m)
        bs_t = jnp.tile(bs[:, :seq].T, (batch, 1))
    x = _gmlp_out(uv, ws_c, bs_t, w_out, x, ln_g, ln_b, _row_tile(m, 256), chunk)
    return x, uv


def _attn_layer(x, batch, seq, caches, pos0, w_in, w_out, ln_g, ln_b):
    m = batch * seq
    bn = 1024
    bm = _row_tile(m, PROJ_ROWS)
    cos, sin = _rope_tables(pos0 + jnp.arange(seq, dtype=jnp.int32))
    cos, sin = jnp.tile(cos, (batch, 1)), jnp.tile(sin, (batch, 1))
    tab = pl.BlockSpec((bm, C_HD), lambda i, j: (i, 0))
    qkv = _proj(x, w_in, (cos, sin), (tab, tab), functools.partial(_attn_proj_epilogue, bn=bn), m, bm, bn,
                "attn_proj")
    qkv3 = qkv.reshape(batch, seq, C_QKV)
    outs, lses = [], []
    for g, (window, dil) in enumerate(C_GROUPS):
        if caches is None:
            o, lse = _band_attn(qkv, g, dil, batch, seq)
        else:
            o, lse = _step_attn(qkv.reshape(batch, seq, C_QKV // C_HD, C_HD), caches[g], g, window, dil,
                                batch, seq)
            o, lse = o.reshape(m, -1), lse.reshape(m, -1)
        outs.append(o)
        lses.append(lse)
    bm2 = _row_tile(m, 256)
    spec = pl.BlockSpec((bm2, C_HEADS * C_HD), lambda i: (i, 0))
    x = _out_ln(tuple(outs) + tuple(lses), (spec,) * 6, _merge_prologue, w_out, x, ln_g, ln_b, bm2,
                "attn_out")
    hw = C_HEADS * C_HD
    ng = len(C_GROUPS)
    kv = []
    for g, (window, _) in enumerate(C_GROUPS):
        tail = qkv3[:, seq - min(window, seq):]
        k = tail[:, :, (ng + g) * hw:(ng + g + 1) * hw].reshape(batch, -1, C_HEADS, C_HD)
        v = tail[:, :, (2 * ng + g) * hw:(2 * ng + g + 1) * hw].reshape(batch, -1, C_HEADS, C_HD)
        kv.append(jnp.stack([k, v], axis=2))
    return x, kv


def kernel(x_prompt, x_sample, state_hgrn, cache_c_kv_w128, cache_c_kv_w512, cache_c_kv_w2048, ln_g, ln_b, a_w_in, a_lb_logits, a_norm_g, a_w_out, b_w_in, b_b_in, b_ln_g, b_ln_b, b_w_s, b_b_s, b_w_out, c_w_in, c_w_out, moe_w_group, moe_w_expert, moe_w1, moe_w3, moe_w2):
    bp, tp, _ = x_prompt.shape
    bs, ts, _ = x_sample.shape
    assert tp % HGRN_SUPER == 0 and tp % B_CHUNK == 0 and tp // C_GROUPS[-1][1] >= 2 * C_QBLOCK
    assert ts <= 8 and ts <= C_GROUPS[1][1] and (bp * tp) % MOE_TILE == 0

    lb_p = jax.nn.softmax(a_lb_logits.astype(F32), axis=0)
    lb_all = jnp.clip(jnp.cumsum(lb_p, axis=0) - lb_p[0:1], 0.0, 1.0 - 1e-6)
    caches = (cache_c_kv_w128, cache_c_kv_w512, cache_c_kv_w2048)

    xp = x_prompt.reshape(bp * tp, D_MODEL)
    xs = x_sample.reshape(bs * ts, D_MODEL)
    hgrn_p, hgrn_s, chunk_v_s = [], [], []
    kv_p = [[] for _ in C_GROUPS]
    kv_s = [[] for _ in C_GROUPS]

    for i in range(DEPTH):
        kind, j = i % 3, i // 3
        g0, b0 = ln_g[i, 0][None], ln_b[i, 0][None]
        if kind == 0:
            w_in, w_out = a_w_in[j].astype(BF16), a_w_out[j].astype(BF16)
            lb, ng = lb_all[j][None], a_norm_g[j][None]
            xp, sp = _hgrn_layer(xp, bp, tp, None, w_in, lb, ng, w_out, g0, b0)
            xs, ss = _hgrn_layer(xs, bs, ts, state_hgrn[j].astype(F32), w_in, lb, ng, w_out, g0, b0)
            hgrn_p.append(sp)
            hgrn_s.append(ss)
        elif kind == 1:
            w_in, w_out = b_w_in[j].astype(BF16), b_w_out[j].astype(BF16)
            args = (w_in, b_b_in[j][None], b_ln_g[j][None], b_ln_b[j][None], b_w_s[j], b_b_s[j], w_out, g0, b0)
            xp, _ = _gmlp_layer(xp, bp, tp, *args)
            xs, uvs = _gmlp_layer(xs, bs, ts, *args)
            chunk_v_s.append(uvs[:, D_MODEL:].reshape(bs, ts, D_MODEL))
        else:
            w_in, w_out = c_w_in[j].astype(BF16), c_w_out[j].astype(BF16)
            xp, kvp = _attn_layer(xp, bp, tp, None, 0, w_in, w_out, g0, b0)
            xs, kvs = _attn_layer(xs, bs, ts, tuple(c[j] for c in caches), PAST_LEN, w_in, w_out, g0, b0)
            for g in range(len(C_GROUPS)):
                kv_p[g].append(kvp[g])
                kv_s[g].append(kvs[g])
        wr = jnp.pad(jnp.concatenate([moe_w_expert[i], moe_w_group[i]], axis=1),
                     ((0, 0), (0, LANES - MOE_GE - MOE_GROUPS)))
        w1g, w3g = _group_up_weights(moe_w1[i]), _group_up_weights(moe_w3[i])
        w2g = moe_w2[i].astype(BF16).reshape(MOE_GROUPS, MOE_EXPERTS * MOE_FF, D_MODEL)
        g1, b1 = ln_g[i, 1][None], ln_b[i, 1][None]
        xp = _moe_dispatch(xp, bp * tp, wr, w1g, w3g, w2g, g1, b1, MOE_TILE)
        xs = _moe(xs, wr, w1g, w3g, w2g, g1, b1, bs * ts)

    return (xp[:bp * tp].reshape(bp, tp, D_MODEL), xs.reshape(bs, ts, D_MODEL),
            jnp.stack(hgrn_p), jnp.stack(hgrn_s), jnp.stack(chunk_v_s),
            jnp.stack(kv_p[0]), jnp.stack(kv_s[0]), jnp.stack(kv_p[1]), jnp.stack(kv_s[1]),
            jnp.stack(kv_p[2]), jnp.stack(kv_s[2]))
```

```python
import functools

import jax
import jax.numpy as jnp
from jax import lax
from jax.experimental import pallas as pl
from jax.experimental.pallas import tpu as pltpu

F32 = jnp.float32
BF16 = jnp.bfloat16
HIGHEST = lax.Precision.HIGHEST

D_MODEL = 2048
DEPTH = 4
PAST_LEN = 16384
A_HEADS = 16
A_DK = 128
A_DV = 128
A_WIDTH = A_HEADS * A_DK
HGRN_CHUNK = 64
HGRN_SUB = 8
LOG2E = 1.4426950408889634
HGRN_SUPER = 256
HGRN_ROWS = 1024
B_CHUNK = 128
B_HEADS = 16
B_HD = 128
C_HEADS = 8
C_HD = 128
C_GROUPS = ((128, 1), (512, 4), (2048, 16))
C_KEYS = 129
C_QBLOCK = 128
C_QKV = 3 * len(C_GROUPS) * C_HEADS * C_HD
ROPE_THETA = 10000.0
MOE_GROUPS = 4
MOE_EXPERTS = 4
MOE_GE = MOE_GROUPS * MOE_EXPERTS
MOE_FF = 256
MOE_TILE = 256
LN_EPS = 1e-5
RMS_EPS = 1e-6
ALPHA = (2 * DEPTH) ** 0.25
LANES = 128
VMEM_LIMIT = 56 * 1024 * 1024
PROJ_ROWS = 1024


def _cparams(*sem):
    return pltpu.CompilerParams(dimension_semantics=sem, vmem_limit_bytes=VMEM_LIMIT)


def _sigmoid(x):
    return 1.0 / (1.0 + jnp.exp(-x))


def _ln_rows(y, g, b):
    mu = jnp.mean(y, axis=-1, keepdims=True)
    d = y - mu
    var = jnp.mean(d * d, axis=-1, keepdims=True)
    return d * lax.rsqrt(var + LN_EPS) * g + b


def _nt_dot(a, b):
    return lax.dot_general(a, b, (((1,), (1,)), ((), ())), preferred_element_type=F32)


def _tn_dot(a, b):
    return lax.dot_general(a, b, (((0,), (0,)), ((), ())), preferred_element_type=F32)


def _proj_body(x_ref, w_ref, *rest, epilogue, n_extra):
    extras = rest[:n_extra]
    o_ref = rest[n_extra]
    xb_ref = rest[n_extra + 1]
    j = pl.program_id(1)

    @pl.when(j == 0)
    def _cast():
        xb_ref[...] = x_ref[...].astype(BF16)

    acc = jnp.dot(xb_ref[...], w_ref[...], preferred_element_type=F32)
    epilogue(acc, j, extras, o_ref)


def _proj(x, w, extras, extra_specs, epilogue, m, bm, bn, name):
    k = x.shape[1]
    n = w.shape[1]
    return pl.pallas_call(
        functools.partial(_proj_body, epilogue=epilogue, n_extra=len(extras)),
        grid=(m // bm, n // bn),
        in_specs=[pl.BlockSpec((bm, k), lambda i, j: (i, 0)),
                  pl.BlockSpec((k, bn), lambda i, j: (0, j))] + list(extra_specs),
        out_specs=pl.BlockSpec((bm, bn), lambda i, j: (i, j)),
        out_shape=jax.ShapeDtypeStruct((m, n), F32),
        scratch_shapes=[pltpu.VMEM((bm, k), BF16)],
        compiler_params=_cparams("parallel", "arbitrary"),
        name=name,
    )(x, w, *extras)


def _hgrn_proj_epilogue(acc, j, extras, o_ref, *, bn):
    nq = A_WIDTH // bn
    is_silu = jnp.logical_or(j < nq, j >= 3 * nq)

    @pl.when(is_silu)
    def _():
        o_ref[...] = acc * _sigmoid(acc)

    @pl.when(jnp.logical_not(is_silu))
    def _():
        o_ref[...] = acc


def _gelu_tanh(z):
    return 0.5 * z * (1.0 + jnp.tanh(0.7978845608028654 * (z + 0.044715 * (z * z * z))))


def _gmlp_proj_epilogue(acc, j, extras, o_ref):
    bias_ref, g_ref, b_ref = extras
    z = _gelu_tanh(acc + bias_ref[...])

    @pl.when(j == 0)
    def _():
        o_ref[...] = z

    @pl.when(j == 1)
    def _():
        o_ref[...] = _ln_rows(z, g_ref[...], b_ref[...])


def _attn_proj_epilogue(acc, j, extras, o_ref, *, bn):
    cos_ref, sin_ref = extras
    n_rot = 2 * len(C_GROUPS) * C_HEADS * C_HD // bn

    @pl.when(j < n_rot)
    def _():
        cos = cos_ref[...]
        sin = sin_ref[...]
        for h in range(bn // C_HD):
            xh = acc[:, h * C_HD:(h + 1) * C_HD]
            o_ref[:, h * C_HD:(h + 1) * C_HD] = xh * cos + pltpu.roll(xh, C_HD // 2, 1) * sin

    @pl.when(j >= n_rot)
    def _():
        o_ref[...] = acc


def _out_ln_body(*refs, prologue, n_in):
    ins = refs[:n_in]
    w_ref, r_ref, g_ref, b_ref, o_ref = refs[n_in:n_in + 5]
    a = prologue(*ins)
    acc = jnp.dot(a, w_ref[...], preferred_element_type=F32)
    y = ALPHA * r_ref[...] + acc
    o_ref[...] = _ln_rows(y, g_ref[...], b_ref[...])


def _out_ln(ins, in_specs, prologue, w, resid, g, b, bm, name):
    m = ins[0].shape[0]
    k = w.shape[0]
    row = pl.BlockSpec((bm, D_MODEL), lambda i: (i, 0))
    vec = pl.BlockSpec((1, D_MODEL), lambda i: (0, 0))
    return pl.pallas_call(
        functools.partial(_out_ln_body, prologue=prologue, n_in=len(ins)),
        grid=(m // bm,),
        in_specs=list(in_specs) + [pl.BlockSpec((k, D_MODEL), lambda i: (0, 0)), row, vec, vec],
        out_specs=row,
        out_shape=jax.ShapeDtypeStruct((m, D_MODEL), F32),
        compiler_params=_cparams("parallel"),
        name=name,
    )(*ins, w, resid, g, b)


def _cast_prologue(a_ref):
    return a_ref[...].astype(BF16)


def _merge_prologue(o0, o1, o2, l0, l1, l2):
    a0, a1, a2 = l0[...], l1[...], l2[...]
    mx = jnp.maximum(jnp.maximum(a0, a1), a2)
    e0, e1, e2 = jnp.exp(a0 - mx), jnp.exp(a1 - mx), jnp.exp(a2 - mx)
    o = (e0 * o0[...] + e1 * o1[...] + e2 * o2[...]) / (e0 + e1 + e2)
    return o.astype(BF16)


def _moe_gate(logits):
    lane = lax.broadcasted_iota(jnp.int32, logits.shape, 1).astype(F32)
    neg = -jnp.inf
    big = 4.0 * LANES
    gl = jnp.where((lane >= MOE_GE) & (lane < MOE_GE + MOE_GROUPS), logits, neg)
    gmax = jnp.max(gl, axis=-1, keepdims=True)
    g_idx = jnp.min(jnp.where(gl == gmax, lane - MOE_GE, big), axis=-1, keepdims=True)
    g_top = 1.0 / jnp.sum(jnp.exp(gl - gmax), axis=-1, keepdims=True)
    lo = g_idx * MOE_EXPERTS
    el = jnp.where((lane >= lo) & (lane < lo + MOE_EXPERTS), logits, neg)
    m1 = jnp.max(el, axis=-1, keepdims=True)
    i1 = jnp.min(jnp.where(el == m1, lane, big), axis=-1, keepdims=True)
    el2 = jnp.where(lane == i1, neg, el)
    m2 = jnp.max(el2, axis=-1, keepdims=True)
    i2 = jnp.min(jnp.where(el2 == m2, lane, big), axis=-1, keepdims=True)
    r = jnp.exp(m2 - m1)
    w1 = g_top / (1.0 + r)
    w2 = w1 * r
    return jnp.where(lane == i1, w1, 0.0) + jnp.where(lane == i2, w2, 0.0), g_idx


def _gate_columns(gate, first):
    lane = lax.broadcasted_iota(jnp.int32, gate.shape, 1)
    return jnp.concatenate(
        [jnp.broadcast_to(jnp.sum(jnp.where(lane == first + e, gate, 0.0), axis=-1, keepdims=True),
                          (gate.shape[0], MOE_FF)) for e in range(MOE_EXPERTS)], axis=1)


def _group_ffn(xb, gate, first, w1_ref, w3_ref, w2_ref):
    h1 = jnp.concatenate([jnp.dot(xb, w1_ref[e], preferred_element_type=F32) for e in range(MOE_EXPERTS)], axis=1)
    h3 = jnp.concatenate([jnp.dot(xb, w3_ref[e], preferred_element_type=F32) for e in range(MOE_EXPERTS)], axis=1)
    hg = (h1 * _sigmoid(h1) * h3 * _gate_columns(gate, first)).astype(BF16)
    return jnp.dot(hg, w2_ref[0], preferred_element_type=F32)


def _moe_body(x_ref, wr_ref, w1_ref, w3_ref, w2_ref, g_ref, b_ref, o_ref, xb_ref, gate_ref, acc_ref):
    gi = pl.program_id(1)

    @pl.when(gi == 0)
    def _route():
        x = x_ref[...]
        xb_ref[...] = x.astype(BF16)
        logits = jnp.dot(x, wr_ref[...], precision=HIGHEST, preferred_element_type=F32)
        gate_ref[...] = _moe_gate(logits)[0]
        acc_ref[...] = jnp.zeros_like(acc_ref)

    acc_ref[...] += _group_ffn(xb_ref[...], gate_ref[...], gi * MOE_EXPERTS, w1_ref, w3_ref, w2_ref)

    @pl.when(gi == MOE_GROUPS - 1)
    def _finish():
        y = ALPHA * x_ref[...] + acc_ref[...]
        o_ref[...] = _ln_rows(y, g_ref[...], b_ref[...])


def _moe(x, wr, w1g, w3g, w2g, g, b, bm):
    m = x.shape[0]
    eff = MOE_EXPERTS * MOE_FF
    row = pl.BlockSpec((bm, D_MODEL), lambda i, e: (i, 0))
    vec = pl.BlockSpec((1, D_MODEL), lambda i, e: (0, 0))
    return pl.pallas_call(
        _moe_body,
        grid=(m // bm, MOE_GROUPS),
        in_specs=[row,
                  pl.BlockSpec((D_MODEL, LANES), lambda i, e: (0, 0)),
                  pl.BlockSpec((MOE_EXPERTS, D_MODEL, MOE_FF), lambda i, e: (e, 0, 0)),
                  pl.BlockSpec((MOE_EXPERTS, D_MODEL, MOE_FF), lambda i, e: (e, 0, 0)),
                  pl.BlockSpec((1, eff, D_MODEL), lambda i, e: (e, 0, 0)),
                  vec, vec],
        out_specs=row,
        out_shape=jax.ShapeDtypeStruct((m, D_MODEL), F32),
        scratch_shapes=[pltpu.VMEM((bm, D_MODEL), BF16),
                        pltpu.VMEM((bm, LANES), F32),
                        pltpu.VMEM((bm, D_MODEL), F32)],
        compiler_params=_cparams("parallel", "arbitrary"),
        name="moe",
    )(x, wr, w1g, w3g, w2g, g, b)


def _router_body(x_ref, wr_ref, gate_ref, gidx_ref, cnt_ref):
    x = x_ref[...]
    xh = x.astype(BF16)
    xl = (x - xh.astype(F32)).astype(BF16)
    a = jnp.dot(xh, wr_ref[...], preferred_element_type=F32)
    logits = a[:, :LANES] + a[:, LANES:] + jnp.dot(xl, wr_ref[:, :LANES], preferred_element_type=F32)
    gate, g_idx = _moe_gate(logits)
    gate_ref[...] = gate
    bm = x.shape[0]
    t = lax.broadcasted_iota(jnp.int32, (bm, LANES), 0)
    lane = lax.broadcasted_iota(jnp.int32, (bm, LANES), 1)
    spread = jnp.where((t & (LANES - 1)) == lane, g_idx, 0.0).astype(BF16)
    r_i = lax.broadcasted_iota(jnp.int32, (bm // LANES, bm), 0)
    t_i = lax.broadcasted_iota(jnp.int32, (bm // LANES, bm), 1)
    sel = (jnp.right_shift(t_i, LANES.bit_length() - 1) == r_i).astype(BF16)
    gidx_ref[...] = jnp.dot(sel, spread, preferred_element_type=F32)
    onehot = (lane.astype(F32) == g_idx).astype(F32)
    cnt_ref[...] = jnp.broadcast_to(jnp.sum(onehot, axis=0, keepdims=True), cnt_ref.shape)


def _router(x, wr, m, bm):
    assert bm % (8 * LANES) == 0 and m % bm == 0
    rows = bm // LANES
    small = pl.BlockSpec((rows, LANES), lambda i: (i, 0))
    return pl.pallas_call(
        _router_body,
        grid=(m // bm,),
        in_specs=[pl.BlockSpec((bm, D_MODEL), lambda i: (i, 0)),
                  pl.BlockSpec((D_MODEL, 2 * LANES), lambda i: (0, 0))],
        out_specs=[pl.BlockSpec((bm, LANES), lambda i: (i, 0)), small, small],
        out_shape=[jax.ShapeDtypeStruct((m, LANES), F32),
                   jax.ShapeDtypeStruct((m // LANES, LANES), F32),
                   jax.ShapeDtypeStruct((m // LANES, LANES), F32)],
        compiler_params=_cparams("parallel"),
        name="moe_router",
    )(x, wr)


def _moe_sorted_body(tg_ref, grp_ref, start_ref, x_hbm, gate_hbm, w1_ref, w3_ref, w2_ref, g_ref, b_ref, o_hbm,
                     xbuf, gbuf, obuf, src_ref, dst_ref, pos_ref, xsem, gsem, osem, *, bm, n):
    i = pl.program_id(0)
    last = pl.num_programs(0) - 1
    slot = i % 2
    shift = bm.bit_length() - 1

    def plan():
        def init(u, carry):
            src_ref[u] = 0
            dst_ref[u] = n + (((u >> shift) + 1) & 1) * bm + (u & (bm - 1))
            return carry

        def fill(t, carry):
            g = grp_ref[t]
            d = pos_ref[g]
            pos_ref[g] = d + 1
            src_ref[d] = t
            dst_ref[bm + d] = t
            return carry

        for g in range(MOE_GROUPS):
            pos_ref[g] = start_ref[g]
        lax.fori_loop(0, src_ref.shape[0], init, 0, unroll=8)
        lax.fori_loop(0, n, fill, 0, unroll=8)

    def gather(tile, s):
        for j in range(bm):
            r = src_ref[tile * bm + j]
            pltpu.make_async_copy(x_hbm.at[pl.ds(r, 1), :], xbuf.at[s, pl.ds(j, 1), :], xsem.at[s]).start()
            pltpu.make_async_copy(gate_hbm.at[pl.ds(r, 1), :], gbuf.at[s, pl.ds(j, 1), :], gsem.at[s]).start()

    def wait_gather(s):
        pltpu.make_async_copy(x_hbm.at[pl.ds(0, bm), :], xbuf.at[s], xsem.at[s]).wait()
        pltpu.make_async_copy(gate_hbm.at[pl.ds(0, bm), :], gbuf.at[s], gsem.at[s]).wait()

    def scatter(tile, s):
        for j in range(bm):
            r = dst_ref[(tile + 1) * bm + j]
            pltpu.make_async_copy(obuf.at[s, pl.ds(j, 1), :], o_hbm.at[pl.ds(r, 1), :], osem.at[s]).start()

    def wait_scatter(s):
        pltpu.make_async_copy(obuf.at[s], o_hbm.at[pl.ds(0, bm), :], osem.at[s]).wait()

    @pl.when(i == 0)
    def _first():
        plan()
        gather(0, 0)
        obuf[1] = jnp.zeros(obuf.shape[1:], F32)

    wait_gather(slot)

    @pl.when(i >= 1)
    def _reuse():
        wait_scatter(slot)

    gather(i + 1, 1 - slot)
    scatter(i - 1, 1 - slot)
    x = xbuf[slot]
    ffn = _group_ffn(x.astype(BF16), gbuf[slot], tg_ref[i] * MOE_EXPERTS, w1_ref, w3_ref, w2_ref)
    obuf[slot] = _ln_rows(ALPHA * x + ffn, g_ref[...], b_ref[...])

    @pl.when(i == last)
    def _drain():
        scatter(i, slot)
        wait_gather(1 - slot)
        wait_scatter(1 - slot)
        wait_scatter(slot)


def _moe_sorted(tile_group, grp, starts, x, gate, w1g, w3g, w2g, g, b, bm):
    n_tiles = tile_group.shape[0]
    n = grp.shape[0]
    assert bm & (bm - 1) == 0
    eff = MOE_EXPERTS * MOE_FF
    vec = pl.BlockSpec((1, D_MODEL), lambda i, tg, gr, st: (0, 0))
    hbm = pl.BlockSpec(memory_space=pl.ANY)
    return pl.pallas_call(
        functools.partial(_moe_sorted_body, bm=bm, n=n),
        grid_spec=pltpu.PrefetchScalarGridSpec(
            num_scalar_prefetch=3,
            grid=(n_tiles,),
            in_specs=[hbm, hbm,
                      pl.BlockSpec((MOE_EXPERTS, D_MODEL, MOE_FF), lambda i, tg, gr, st: (tg[i], 0, 0)),
                      pl.BlockSpec((MOE_EXPERTS, D_MODEL, MOE_FF), lambda i, tg, gr, st: (tg[i], 0, 0)),
                      pl.BlockSpec((1, eff, D_MODEL), lambda i, tg, gr, st: (tg[i], 0, 0)),
                      vec, vec],
            out_specs=hbm,
            scratch_shapes=[pltpu.VMEM((2, bm, D_MODEL), F32),
                            pltpu.VMEM((2, bm, LANES), F32),
                            pltpu.VMEM((2, bm, D_MODEL), F32),
                            pltpu.SMEM(((n_tiles + 1) * bm,), jnp.int32),
                            pltpu.SMEM(((n_tiles + 1) * bm,), jnp.int32),
                            pltpu.SMEM((MOE_GROUPS,), jnp.int32),
                            pltpu.SemaphoreType.DMA((2,)),
                            pltpu.SemaphoreType.DMA((2,)),
                            pltpu.SemaphoreType.DMA((2,))]),
        out_shape=jax.ShapeDtypeStruct((n + 2 * bm, D_MODEL), F32),
        compiler_params=_cparams("arbitrary"),
        name="moe_sorted",
    )(tile_group, grp, starts, x, gate, w1g, w3g, w2g, g, b)


def _moe_dispatch(x, n, wr, w1g, w3g, w2g, g, b, bm):
    wr_hi = wr.astype(BF16)
    wr_lo = (wr - wr_hi.astype(F32)).astype(BF16)
    rt = 8 * LANES
    gate, gidx, cnt = _router(x, jnp.concatenate([wr_hi, wr_lo], axis=1), n, rt)
    grp = gidx.astype(jnp.int32).reshape(n)
    counts = jnp.sum(cnt.reshape(n // rt, rt // LANES, LANES)[:, 0, :MOE_GROUPS], axis=0).astype(jnp.int32)
    padded = (counts + bm - 1) // bm * bm
    ends = jnp.cumsum(padded)
    starts = ends - padded
    n_tiles = n // bm + MOE_GROUPS
    tile_start = jnp.arange(n_tiles, dtype=jnp.int32) * bm
    tile_group = jnp.minimum(jnp.sum(tile_start[:, None] >= ends[None, :], axis=1), MOE_GROUPS - 1)
    return _moe_sorted(tile_group.astype(jnp.int32), grp, starts.astype(jnp.int32), x, gate,
                       w1g, w3g, w2g, g, b, bm)


def _hgrn_gates(f, lb):
    log_sig = jnp.minimum(f, 0.0) - jnp.log1p(jnp.exp(-jnp.abs(f)))
    a = jnp.log1p(-lb) + log_sig
    log_lb = jnp.log(lb)
    log_f = jnp.maximum(log_lb, a) + jnp.log1p(jnp.exp(-jnp.abs(log_lb - a)))
    k = (1.0 - lb) / (1.0 + jnp.exp(f))
    return log_f, k


def _hgrn_finish(o, gate, ng):
    o = o * lax.rsqrt(jnp.mean(o * o, axis=-1, keepdims=True) + RMS_EPS) * ng
    return o * gate


def _rows_of(x, idx, n):
    return jnp.concatenate([jnp.broadcast_to(x[i:i + 1, :], (n, x.shape[1])) for i in idx], axis=0)


def _hgrn_scan_body(q_ref, f_ref, v_ref, g_ref, lb_ref, ng_ref, tri_ref, lm_ref, bsel_ref, dm_ref,
                    o_ref, s_ref, st_ref, *, n_super):
    C, c, N = HGRN_CHUNK, HGRN_SUB, HGRN_SUPER
    neg = -1e30
    tb = pl.program_id(2)

    @pl.when(tb == 0)
    def _init():
        st_ref[...] = jnp.zeros_like(st_ref)

    lb = lb_ref[...]
    ng = ng_ref[...]
    row = lax.broadcasted_iota(jnp.int32, (N, 1), 0)
    srow = lax.broadcasted_iota(jnp.int32, (c, 1), 0)
    levels = [(C >> l, C >> (l + 1)) for l in range(lm_ref.shape[0])]

    def super_chunk(si, carry):
        r0 = pl.multiple_of(si * N, N)
        q = q_ref[pl.ds(r0, N), :]
        v = v_ref[pl.ds(r0, N), :].astype(BF16)
        gate = g_ref[pl.ds(r0, N), :]
        log_f, k = _hgrn_gates(f_ref[pl.ds(r0, N), :], lb)

        hi = log_f.astype(BF16)
        r1 = log_f - hi.astype(F32)
        mid = r1.astype(BF16)
        lo = (r1 - mid.astype(F32)).astype(BF16)
        cs = jnp.dot(tri_ref[...], jnp.concatenate([hi, mid, lo], axis=1), preferred_element_type=F32)
        b2 = (cs[:, :A_DK] + cs[:, A_DK:2 * A_DK] + cs[:, 2 * A_DK:]) * LOG2E

        att_t = None
        for li, (blk, half) in enumerate(levels):
            up = (row & (blk - 1)) >= half
            ref = _rows_of(b2, [blk * m + half - 1 for m in range(N // blk)], blk)
            e = jnp.exp2(jnp.where(up, b2 - ref, ref - b2))
            ql = jnp.where(up, q * e, 0.0).astype(BF16)
            kl = jnp.where(up, 0.0, k * e).astype(BF16)
            term = _nt_dot(kl, ql) * lm_ref[li]
            att_t = term if att_t is None else att_t + term

        ys = []
        for i in range(N // c):
            base = c * i
            kb = k[base:base + c, :]
            bb = b2[base:base + c, :]
            units = []
            for tl in range(c):
                d = jnp.where(srow <= tl, b2[base + tl:base + tl + 1, :] - bb, neg)
                units.append(q[base + tl:base + tl + 1, :] * kb * jnp.exp2(d))
            ys.append(jnp.concatenate(units, axis=1))
        y = jnp.concatenate(ys, axis=0).astype(BF16)
        r = jnp.dot(y, bsel_ref[...], preferred_element_type=F32)
        diag = jnp.concatenate([r * dm_ref[0], r * dm_ref[1]], axis=1)
        o_intra = _tn_dot((att_t + diag).astype(BF16), v)

        bl = _rows_of(b2, [C * m + C - 1 for m in range(N // C)], C)
        qe = (q * jnp.exp2(b2)).astype(BF16)
        kd = (k * jnp.exp2(bl - b2)).astype(BF16)
        st = st_ref[...]
        nc = N // C
        states, lhs = [], []
        zero = jnp.zeros((C, A_DK), BF16)
        for m in range(nc):
            rows = slice(C * m, C * m + C)
            states.append(st.astype(BF16))
            lhs.append(jnp.concatenate([qe[rows, :] if j == m else zero for j in range(nc)], axis=1))
            st = st * jnp.exp2(b2[C * m + C - 1:C * m + C, :]) + _tn_dot(v[rows, :], kd[rows, :])
        st_ref[...] = st
        o = o_intra + _nt_dot(jnp.concatenate(lhs, axis=0), jnp.concatenate(states, axis=1))
        o_ref[pl.ds(r0, N), :] = _hgrn_finish(o, gate, ng).astype(o_ref.dtype)
        return carry

    for si in range(n_super):
        super_chunk(si, 0)

    @pl.when(tb == pl.num_programs(2) - 1)
    def _emit():
        s_ref[0, 0] = st_ref[...].T


def _hgrn_scan(p, lb, ng, batch, seq, tb):
    nt = seq // tb
    nh = A_HEADS
    n, c = HGRN_SUPER, HGRN_SUB
    assert n == 2 * LANES and LANES % c == 0
    s_i = jnp.arange(n)[:, None]
    t_i = jnp.arange(n)[None, :]
    tri = ((s_i >= t_i) & (s_i // HGRN_CHUNK == t_i // HGRN_CHUNK)).astype(BF16)
    blocks = []
    blk = HGRN_CHUNK
    while blk > c:
        blocks.append(blk)
        blk //= 2
    lm = jnp.stack([(s_i // bk == t_i // bk) for bk in blocks]).astype(F32)
    lane = jnp.arange(LANES)[None, :]
    bsel = (jnp.arange(c * A_DK)[:, None] // A_DK == lane % c).astype(BF16)
    dm = jnp.stack([(s_i // c == g * (LANES // c) + lane // c) for g in range(2)]).astype(F32)

    def col(off):
        return pl.BlockSpec((tb, A_DK), lambda b, h, t: (b * nt + t, off * nh + h))

    def const(a):
        return pl.BlockSpec(a.shape, lambda b, h, t: (0,) * a.ndim)

    return pl.pallas_call(
        functools.partial(_hgrn_scan_body, n_super=tb // n),
        grid=(batch, nh, nt),
        in_specs=[col(0), col(1), col(2), col(3),
                  pl.BlockSpec((1, A_DK), lambda b, h, t: (0, h)),
                  pl.BlockSpec((1, A_DV), lambda b, h, t: (0, 0)),
                  const(tri), const(lm), const(bsel), const(dm)],
        out_specs=[pl.BlockSpec((tb, A_DV), lambda b, h, t: (b * nt + t, h)),
                   pl.BlockSpec((1, 1, A_DK, A_DV), lambda b, h, t: (b, h, 0, 0))],
        out_shape=[jax.ShapeDtypeStruct((batch * seq, nh * A_DV), BF16),
                   jax.ShapeDtypeStruct((batch, nh, A_DK, A_DV), F32)],
        scratch_shapes=[pltpu.VMEM((A_DV, A_DK), F32)],
        compiler_params=_cparams("parallel", "parallel", "arbitrary"),
        name="hgrn_scan",
    )(p, p, p, p, lb, ng, tri, lm, bsel, dm)


def _hgrn_step_body(p_ref, s0_ref, lb_ref, ng_ref, o_ref, s_ref, *, n_tok):
    R = p_ref.shape[0]
    row = lax.broadcasted_iota(jnp.int32, (R, 1), 0)
    valid = row < n_tok
    r2 = lax.broadcasted_iota(jnp.int32, (R, R), 0)
    c2 = lax.broadcasted_iota(jnp.int32, (R, R), 1)
    tril = (r2 >= c2).astype(F32)
    ng = ng_ref[...]
    for h in range(A_HEADS):
        sl = slice(h * A_DK, (h + 1) * A_DK)
        q = p_ref[:, sl]
        v = p_ref[:, 2 * A_WIDTH + h * A_DV:2 * A_WIDTH + (h + 1) * A_DV]
        gate = p_ref[:, 3 * A_WIDTH + h * A_DV:3 * A_WIDTH + (h + 1) * A_DV]
        log_f, k = _hgrn_gates(p_ref[:, A_WIDTH + h * A_DK:A_WIDTH + (h + 1) * A_DK], lb_ref[:, sl])
        b = jnp.dot(tril, log_f, precision=HIGHEST, preferred_element_type=F32)
        st = s0_ref[0, h].T
        o = _nt_dot((q * jnp.exp(b)).astype(BF16), st.astype(BF16))
        for s in range(n_tok):
            m = row >= s
            w = jnp.where(m, q * k[s:s + 1, :] * jnp.exp(jnp.where(m, b - b[s:s + 1, :], 0.0)), 0.0)
            o = o + jnp.sum(w, axis=-1, keepdims=True) * v[s:s + 1, :]
        o_ref[:, h * A_DV:(h + 1) * A_DV] = _hgrn_finish(o, gate, ng)
        bl = b[n_tok - 1:n_tok, :]
        kd = jnp.where(valid, k * jnp.exp(jnp.where(valid, bl - b, 0.0)), 0.0)
        st_new = st * jnp.exp(bl) + _tn_dot(v.astype(BF16), kd.astype(BF16))
        s_ref[0, h] = st_new.T


def _hgrn_step(p, s0, lb, ng, batch, n_tok):
    rows = p.shape[0] // batch
    return pl.pallas_call(
        functools.partial(_hgrn_step_body, n_tok=n_tok),
        grid=(batch,),
        in_specs=[pl.BlockSpec((rows, 4 * A_WIDTH), lambda b: (b, 0)),
                  pl.BlockSpec((1, A_HEADS, A_DK, A_DV), lambda b: (b, 0, 0, 0)),
                  pl.BlockSpec((1, A_WIDTH), lambda b: (0, 0)),
                  pl.BlockSpec((1, A_DV), lambda b: (0, 0))],
        out_specs=[pl.BlockSpec((rows, A_HEADS * A_DV), lambda b: (b, 0)),
                   pl.BlockSpec((1, A_HEADS, A_DK, A_DV), lambda b: (b, 0, 0, 0))],
        out_shape=[jax.ShapeDtypeStruct((batch * rows, A_HEADS * A_DV), F32),
                   jax.ShapeDtypeStruct((batch, A_HEADS, A_DK, A_DV), F32)],
        compiler_params=_cparams("parallel"),
        name="hgrn_step",
    )(p, s0, lb, ng)


def _gmlp_prologue(u_ref, v_ref, ws_ref, bs_ref, gated_ref, *, chunk):
    bm = u_ref.shape[0]
    r2 = lax.broadcasted_iota(jnp.int32, (chunk, chunk), 0)
    c2 = lax.broadcasted_iota(jnp.int32, (chunk, chunk), 1)
    causal = r2 >= c2
    for h in range(B_HEADS):
        wc = jnp.where(causal, ws_ref[h], 0.0).astype(BF16)
        bias = bs_ref[:, h:h + 1]
        cols = slice(h * B_HD, (h + 1) * B_HD)
        for n in range(bm // chunk):
            rows = slice(n * chunk, (n + 1) * chunk)
            mixed = jnp.dot(wc, v_ref[rows, cols].astype(BF16), preferred_element_type=F32) + bias
            gated_ref[rows, cols] = (u_ref[rows, cols] * mixed).astype(BF16)
    return gated_ref[...]


def _gmlp_out_body(u_ref, v_ref, ws_ref, bs_ref, w_ref, r_ref, g_ref, b_ref, o_ref, gated_ref, *, chunk):
    a = _gmlp_prologue(u_ref, v_ref, ws_ref, bs_ref, gated_ref, chunk=chunk)
    acc = jnp.dot(a, w_ref[...], preferred_element_type=F32)
    y = ALPHA * r_ref[...] + acc
    o_ref[...] = _ln_rows(y, g_ref[...], b_ref[...])


def _gmlp_out(uv, ws, bs_t, w, resid, g, b, bm, chunk):
    m = uv.shape[0]
    row = pl.BlockSpec((bm, D_MODEL), lambda i: (i, 0))
    vec = pl.BlockSpec((1, D_MODEL), lambda i: (0, 0))
    return pl.pallas_call(
        functools.partial(_gmlp_out_body, chunk=chunk),
        grid=(m // bm,),
        in_specs=[pl.BlockSpec((bm, D_MODEL), lambda i: (i, 0)),
                  pl.BlockSpec((bm, D_MODEL), lambda i: (i, 1)),
                  pl.BlockSpec(ws.shape, lambda i: (0, 0, 0)),
                  pl.BlockSpec(bs_t.shape, lambda i: (0, 0)),
                  pl.BlockSpec((D_MODEL, D_MODEL), lambda i: (0, 0)),
                  row, vec, vec],
        out_specs=row,
        out_shape=jax.ShapeDtypeStruct((m, D_MODEL), F32),
        scratch_shapes=[pltpu.VMEM((bm, D_MODEL), BF16)],
        compiler_params=_cparams("parallel"),
        name="gmlp_out",
    )(uv, uv, ws, bs_t, w, resid, g, b)


def _band_attn_body(q_ref, k_ref, v_ref, o_ref, l_ref, *, dil, seq):
    bq = C_QBLOCK
    span = C_KEYS - 1
    scale = C_HD ** -0.5
    n_blocks = seq // dil // bq
    qi = lax.broadcasted_iota(jnp.int32, (bq, 2 * bq), 0)
    ki = lax.broadcasted_iota(jnp.int32, (bq, 2 * bq), 1)

    def rows(first, n):
        return pl.ds(first, n) if dil == 1 else pl.ds(first, n, stride=dil)

    for r in range(dil):
        for i in range(n_blocks):
            w = max(i - 1, 0)
            qs = rows(r + dil * bq * i, bq)
            ws = rows(r + dil * bq * w, 2 * bq)
            q = q_ref[qs, :].astype(BF16)
            kw = k_ref[ws, :].astype(BF16)
            vw = v_ref[ws, :].astype(BF16)
            s = _nt_dot(q, kw) * scale
            rel = bq * (i - w) + qi - ki
            s = jnp.where((rel >= 0) & (rel <= span), s, -jnp.inf)
            mx = jnp.max(s, axis=-1, keepdims=True)
            p = jnp.exp(s - mx)
            den = jnp.sum(p, axis=-1, keepdims=True)
            o_ref[qs, :] = jnp.dot(p.astype(BF16), vw, preferred_element_type=F32) / den
            l_ref[qs, :] = jnp.broadcast_to(mx + jnp.log(den), (bq, C_HD))


def _band_attn(qkv, g, dil, batch, seq):
    ng = len(C_GROUPS)

    def col(part):
        return pl.BlockSpec((seq, C_HD), lambda b, h: (b, (part * ng + g) * C_HEADS + h))

    out = pl.BlockSpec((seq, C_HD), lambda b, h: (b, h))
    return pl.pallas_call(
        functools.partial(_band_attn_body, dil=dil, seq=seq),
        grid=(batch, C_HEADS),
        in_specs=[col(0), col(1), col(2)],
        out_specs=[out, out],
        out_shape=[jax.ShapeDtypeStruct((batch * seq, C_HEADS * C_HD), F32)] * 2,
        compiler_params=_cparams("parallel", "parallel"),
        name=f"band_attn_d{dil}",
    )(qkv, qkv, qkv)


def _step_attn_body(qkv_ref, *refs, g, dil, n_tok):
    past_refs, (o_ref, l_ref) = refs[:-2], refs[-2:]
    scale = C_HD ** -0.5
    ng = len(C_GROUPS)
    hs = C_HEADS
    tok = lax.broadcasted_iota(jnp.int32, (n_tok, 1, 1), 0)
    prow = lax.broadcasted_iota(jnp.int32, (past_refs[0].shape[1], 1, 1), 0)
    kn = qkv_ref[0, :, (ng + g) * hs:(ng + g + 1) * hs, :]
    vn = qkv_ref[0, :, (2 * ng + g) * hs:(2 * ng + g + 1) * hs, :]
    for t in range(n_tok):
        past_ref = past_refs[t % dil]
        new_ok = (tok <= t) & (((t - tok) % dil) == 0)
        past_ok = prow >= (t if dil == 1 else 0)
        q = qkv_ref[0, t, g * hs:(g + 1) * hs, :][None]
        kp = past_ref[0, :, 0:hs, :]
        vp = past_ref[0, :, hs:2 * hs, :]
        sp = jnp.where(past_ok, jnp.sum(kp * q, axis=-1, keepdims=True) * scale, -jnp.inf)
        sn = jnp.where(new_ok, jnp.sum(kn * q, axis=-1, keepdims=True) * scale, -jnp.inf)
        mx = jnp.maximum(jnp.max(sp, axis=0, keepdims=True), jnp.max(sn, axis=0, keepdims=True))
        pp = jnp.exp(sp - mx)
        pn = jnp.exp(sn - mx)
        den = jnp.sum(pp, axis=0, keepdims=True) + jnp.sum(pn, axis=0, keepdims=True)
        o = (jnp.sum(pp * vp, axis=0, keepdims=True) + jnp.sum(pn * vn, axis=0, keepdims=True)) / den
        o_ref[0, t] = o[0]
        l_ref[0, t] = jnp.broadcast_to(mx + jnp.log(den), (1, hs, C_HD))[0]


def _step_attn(qkv, cache, g, window, dil, batch, n_tok):
    past = cache.reshape(batch, window // dil, dil, 2 * C_HEADS, C_HD)
    n_res = min(dil, n_tok)
    out = pl.BlockSpec((1, n_tok, C_HEADS, C_HD), lambda b: (b, 0, 0, 0))
    past_specs = [pl.BlockSpec((1, window // dil, None, 2 * C_HEADS, C_HD),
                               functools.partial(lambda b, r: (b, 0, r, 0, 0), r=r)) for r in range(n_res)]
    return pl.pallas_call(
        functools.partial(_step_attn_body, g=g, dil=dil, n_tok=n_tok),
        grid=(batch,),
        in_specs=[pl.BlockSpec((1, n_tok) + qkv.shape[2:], lambda b: (b, 0, 0, 0))] + past_specs,
        out_specs=[out, out],
        out_shape=[jax.ShapeDtypeStruct((batch, n_tok, C_HEADS, C_HD), F32)] * 2,
        compiler_params=_cparams("parallel"),
        name=f"step_attn_d{dil}",
    )(qkv, *([past] * n_res))


def _rope_tables(pos):
    half = C_HD // 2
    inv = ROPE_THETA ** (-jnp.arange(half, dtype=F32) / half)
    ang = pos.astype(F32)[:, None] * inv[None, :]
    cos, sin = jnp.cos(ang), jnp.sin(ang)
    return jnp.concatenate([cos, cos], -1), jnp.concatenate([-sin, sin], -1)


def _row_tile(m, cap):
    return min(m, cap)


def _hgrn_layer(x, batch, seq, s0, w_in, lb, ng, w_out, ln_g, ln_b):
    m = batch * seq
    bn = 1024
    p = _proj(x, w_in, (), (), functools.partial(_hgrn_proj_epilogue, bn=bn), m, _row_tile(m, PROJ_ROWS), bn,
              "hgrn_proj")
    if s0 is None:
        o, s_new = _hgrn_scan(p, lb, ng, batch, seq, HGRN_ROWS if seq % HGRN_ROWS == 0 else HGRN_SUPER)
    else:
        rows = 8
        pp = jnp.pad(p.reshape(batch, seq, -1), ((0, 0), (0, rows - seq), (0, 0))).reshape(batch * rows, -1)
        o, s_new = _hgrn_step(pp, s0, lb, ng, batch, seq)
        o = o.reshape(batch, rows, -1)[:, :seq].reshape(m, -1)
    bm = _row_tile(m, 512)
    x = _out_ln((o,), (pl.BlockSpec((bm, o.shape[1]), lambda i: (i, 0)),), _cast_prologue,
                w_out, x, ln_g, ln_b, bm, "hgrn_out")
    return x, s_new


def _gmlp_layer(x, batch, seq, w_in, b_in, g1, b1, ws, bs, w_out, ln_g, ln_b):
    m = batch * seq
    bm = _row_tile(m, 512)
    vec = pl.BlockSpec((1, D_MODEL), lambda i, j: (0, 0))
    uv = _proj(x, w_in, (b_in, g1, b1),
               (pl.BlockSpec((1, D_MODEL), lambda i, j: (0, j)), vec, vec),
               _gmlp_proj_epilogue, m, bm, D_MODEL, "gmlp_proj")
    if seq % B_CHUNK == 0:
        chunk, ws_c, bs_t = B_CHUNK, ws, bs.T
    else:
        chunk = m
        eye = jnp.eye(batch, dtype=ws.dtype)
        ws_c = jnp.einsum("ab,hts->hatbs", eye, ws[:, :seq, :seq]).reshape(B_HEADS, m, m)
        bs_t = jnp.tile(bs[:, :seq].T, (batch, 1))
    x = _gmlp_out(uv, ws_c, bs_t, w_out, x, ln_g, ln_b, _row_tile(m, 256), chunk)
    return x, uv


def _attn_layer(x, batch, seq, caches, pos0, w_in, w_out, ln_g, ln_b):
    m = batch * seq
    bn = 1024
    bm = _row_tile(m, PROJ_ROWS)
    cos, sin = _rope_tables(pos0 + jnp.arange(seq, dtype=jnp.int32))
    cos, sin = jnp.tile(cos, (batch, 1)), jnp.tile(sin, (batch, 1))
    tab = pl.BlockSpec((bm, C_HD), lambda i, j: (i, 0))
    qkv = _proj(x, w_in, (cos, sin), (tab, tab), functools.partial(_attn_proj_epilogue, bn=bn), m, bm, bn,
                "attn_proj")
    qkv3 = qkv.reshape(batch, seq, C_QKV)
    outs, lses = [], []
    for g, (window, dil) in enumerate(C_GROUPS):
        if caches is None:
            o, lse = _band_attn(qkv, g, dil, batch, seq)
        else:
            o, lse = _step_attn(qkv.reshape(batch, seq, C_QKV // C_HD, C_HD), caches[g], g, window, dil,
                                batch, seq)
            o, lse = o.reshape(m, -1), lse.reshape(m, -1)
        outs.append(o)
        lses.append(lse)
    bm2 = _row_tile(m, 256)
    spec = pl.BlockSpec((bm2, C_HEADS * C_HD), lambda i: (i, 0))
    x = _out_ln(tuple(outs) + tuple(lses), (spec,) * 6, _merge_prologue, w_out, x, ln_g, ln_b, bm2,
                "attn_out")
    hw = C_HEADS * C_HD
    ng = len(C_GROUPS)
    kv = []
    for g, (window, _) in enumerate(C_GROUPS):
        tail = qkv3[:, seq - min(window, seq):]
        k = tail[:, :, (ng + g) * hw:(ng + g + 1) * hw].reshape(batch, -1, C_HEADS, C_HD)
        v = tail[:, :, (2 * ng + g) * hw:(2 * ng + g + 1) * hw].reshape(batch, -1, C_HEADS, C_HD)
        kv.append(jnp.stack([k, v], axis=2))
    return x, kv


def kernel(x_prompt, x_sample, state_hgrn, cache_c_kv_w128, cache_c_kv_w512, cache_c_kv_w2048, ln_g, ln_b, a_w_in, a_lb_logits, a_norm_g, a_w_out, b_w_in, b_b_in, b_ln_g, b_ln_b, b_w_s, b_b_s, b_w_out, c_w_in, c_w_out, moe_w_group, moe_w_expert, moe_w1, moe_w3, moe_w2):
    bp, tp, _ = x_prompt.shape
    bs, ts, _ = x_sample.shape
    assert tp % HGRN_SUPER == 0 and tp % B_CHUNK == 0 and tp // C_GROUPS[-1][1] >= 2 * C_QBLOCK
    assert ts <= 8 and ts <= C_GROUPS[1][1] and (bp * tp) % MOE_TILE == 0

    lb_p = jax.nn.softmax(a_lb_logits.astype(F32), axis=0)
    lb_all = jnp.clip(jnp.cumsum(lb_p, axis=0) - lb_p[0:1], 0.0, 1.0 - 1e-6)
    caches = (cache_c_kv_w128, cache_c_kv_w512, cache_c_kv_w2048)

    xp = x_prompt.reshape(bp * tp, D_MODEL)
    xs = x_sample.reshape(bs * ts, D_MODEL)
    hgrn_p, hgrn_s, chunk_v_s = [], [], []
    kv_p = [[] for _ in C_GROUPS]
    kv_s = [[] for _ in C_GROUPS]

    for i in range(DEPTH):
        kind, j = i % 3, i // 3
        g0, b0 = ln_g[i, 0][None], ln_b[i, 0][None]
        if kind == 0:
            w_in, w_out = a_w_in[j].astype(BF16), a_w_out[j].astype(BF16)
            lb, ng = lb_all[j][None], a_norm_g[j][None]
            xp, sp = _hgrn_layer(xp, bp, tp, None, w_in, lb, ng, w_out, g0, b0)
            xs, ss = _hgrn_layer(xs, bs, ts, state_hgrn[j].astype(F32), w_in, lb, ng, w_out, g0, b0)
            hgrn_p.append(sp)
            hgrn_s.append(ss)
        elif kind == 1:
            w_in, w_out = b_w_in[j].astype(BF16), b_w_out[j].astype(BF16)
            args = (w_in, b_b_in[j][None], b_ln_g[j][None], b_ln_b[j][None], b_w_s[j], b_b_s[j], w_out, g0, b0)
            xp, _ = _gmlp_layer(xp, bp, tp, *args)
            xs, uvs = _gmlp_layer(xs, bs, ts, *args)
            chunk_v_s.append(uvs[:, D_MODEL:].reshape(bs, ts, D_MODEL))
        else:
            w_in, w_out = c_w_in[j].astype(BF16), c_w_out[j].astype(BF16)
            xp, kvp = _attn_layer(xp, bp, tp, None, 0, w_in, w_out, g0, b0)
            xs, kvs = _attn_layer(xs, bs, ts, tuple(c[j] for c in caches), PAST_LEN, w_in, w_out, g0, b0)
            for g in range(len(C_GROUPS)):
                kv_p[g].append(kvp[g])
                kv_s[g].append(kvs[g])
        wr = jnp.pad(jnp.concatenate([moe_w_expert[i], moe_w_group[i]], axis=1),
                     ((0, 0), (0, LANES - MOE_GE - MOE_GROUPS)))
        w1g, w3g = moe_w1[i].astype(BF16), moe_w3[i].astype(BF16)
        w2g = moe_w2[i].astype(BF16).reshape(MOE_GROUPS, MOE_EXPERTS * MOE_FF, D_MODEL)
        g1, b1 = ln_g[i, 1][None], ln_b[i, 1][None]
        xp = _moe_dispatch(xp, bp * tp, wr, w1g, w3g, w2g, g1, b1, MOE_TILE)
        xs = _moe(xs, wr, w1g, w3g, w2g, g1, b1, bs * ts)

    return (xp[:bp * tp].reshape(bp, tp, D_MODEL), xs.reshape(bs, ts, D_MODEL),
            jnp.stack(hgrn_p), jnp.stack(hgrn_s), jnp.stack(chunk_v_s),
            jnp.stack(kv_p[0]), jnp.stack(kv_s[0]), jnp.stack(kv_p[1]), jnp.stack(kv_s[1]),
            jnp.stack(kv_p[2]), jnp.stack(kv_s[2]))
```

```python
import functools

import jax
import jax.numpy as jnp
from jax import lax
from jax.experimental import pallas as pl
from jax.experimental.pallas import tpu as pltpu

F32 = jnp.float32
BF16 = jnp.bfloat16
HIGHEST = lax.Precision.HIGHEST

D_MODEL = 2048
DEPTH = 4
PAST_LEN = 16384
A_HEADS = 16
A_DK = 128
A_DV = 128
A_WIDTH = A_HEADS * A_DK
HGRN_CHUNK = 64
HGRN_SUB = 8
LOG2E = 1.4426950408889634
HGRN_SUPER = 256
HGRN_ROWS = 1024
B_CHUNK = 128
B_HEADS = 16
B_HD = 128
C_HEADS = 8
C_HD = 128
C_GROUPS = ((128, 1), (512, 4), (2048, 16))
C_KEYS = 129
C_QBLOCK = 128
C_QKV = 3 * len(C_GROUPS) * C_HEADS * C_HD
ROPE_THETA = 10000.0
MOE_GROUPS = 4
MOE_EXPERTS = 4
MOE_GE = MOE_GROUPS * MOE_EXPERTS
MOE_FF = 256
MOE_TILE = 256
LN_EPS = 1e-5
RMS_EPS = 1e-6
ALPHA = (2 * DEPTH) ** 0.25
LANES = 128
VMEM_LIMIT = 56 * 1024 * 1024
PROJ_ROWS = 1024


def _cparams(*sem):
    return pltpu.CompilerParams(dimension_semantics=sem, vmem_limit_bytes=VMEM_LIMIT)


def _sigmoid(x):
    return 1.0 / (1.0 + jnp.exp(-x))


def _ln_rows(y, g, b):
    mu = jnp.mean(y, axis=-1, keepdims=True)
    d = y - mu
    var = jnp.mean(d * d, axis=-1, keepdims=True)
    return d * lax.rsqrt(var + LN_EPS) * g + b


def _nt_dot(a, b):
    return lax.dot_general(a, b, (((1,), (1,)), ((), ())), preferred_element_type=F32)


def _tn_dot(a, b):
    return lax.dot_general(a, b, (((0,), (0,)), ((), ())), preferred_element_type=F32)


def _proj_body(x_ref, w_ref, *rest, epilogue, n_extra):
    extras = rest[:n_extra]
    o_ref = rest[n_extra]
    xb_ref = rest[n_extra + 1]
    j = pl.program_id(1)

    @pl.when(j == 0)
    def _cast():
        xb_ref[...] = x_ref[...].astype(BF16)

    acc = jnp.dot(xb_ref[...], w_ref[...], preferred_element_type=F32)
    epilogue(acc, j, extras, o_ref)


def _proj(x, w, extras, extra_specs, epilogue, m, bm, bn, name):
    k = x.shape[1]
    n = w.shape[1]
    return pl.pallas_call(
        functools.partial(_proj_body, epilogue=epilogue, n_extra=len(extras)),
        grid=(m // bm, n // bn),
        in_specs=[pl.BlockSpec((bm, k), lambda i, j: (i, 0)),
                  pl.BlockSpec((k, bn), lambda i, j: (0, j))] + list(extra_specs),
        out_specs=pl.BlockSpec((bm, bn), lambda i, j: (i, j)),
        out_shape=jax.ShapeDtypeStruct((m, n), F32),
        scratch_shapes=[pltpu.VMEM((bm, k), BF16)],
        compiler_params=_cparams("parallel", "arbitrary"),
        name=name,
    )(x, w, *extras)


def _hgrn_proj_epilogue(acc, j, extras, o_ref, *, bn):
    nq = A_WIDTH // bn
    is_silu = jnp.logical_or(j < nq, j >= 3 * nq)

    @pl.when(is_silu)
    def _():
        o_ref[...] = acc * _sigmoid(acc)

    @pl.when(jnp.logical_not(is_silu))
    def _():
        o_ref[...] = acc


def _gelu_tanh(z):
    return 0.5 * z * (1.0 + jnp.tanh(0.7978845608028654 * (z + 0.044715 * (z * z * z))))


def _gmlp_proj_epilogue(acc, j, extras, o_ref):
    bias_ref, g_ref, b_ref = extras
    z = _gelu_tanh(acc + bias_ref[...])

    @pl.when(j == 0)
    def _():
        o_ref[...] = z

    @pl.when(j == 1)
    def _():
        o_ref[...] = _ln_rows(z, g_ref[...], b_ref[...])


def _attn_proj_epilogue(acc, j, extras, o_ref, *, bn):
    cos_ref, sin_ref = extras
    n_rot = 2 * len(C_GROUPS) * C_HEADS * C_HD // bn

    @pl.when(j < n_rot)
    def _():
        cos = cos_ref[...]
        sin = sin_ref[...]
        for h in range(bn // C_HD):
            xh = acc[:, h * C_HD:(h + 1) * C_HD]
            o_ref[:, h * C_HD:(h + 1) * C_HD] = xh * cos + pltpu.roll(xh, C_HD // 2, 1) * sin

    @pl.when(j >= n_rot)
    def _():
        o_ref[...] = acc


def _out_ln_body(*refs, prologue, n_in):
    ins = refs[:n_in]
    w_ref, r_ref, g_ref, b_ref, o_ref = refs[n_in:n_in + 5]
    a = prologue(*ins)
    acc = jnp.dot(a, w_ref[...], preferred_element_type=F32)
    y = ALPHA * r_ref[...] + acc
    o_ref[...] = _ln_rows(y, g_ref[...], b_ref[...])


def _out_ln(ins, in_specs, prologue, w, resid, g, b, bm, name):
    m = ins[0].shape[0]
    k = w.shape[0]
    row = pl.BlockSpec((bm, D_MODEL), lambda i: (i, 0))
    vec = pl.BlockSpec((1, D_MODEL), lambda i: (0, 0))
    return pl.pallas_call(
        functools.partial(_out_ln_body, prologue=prologue, n_in=len(ins)),
        grid=(m // bm,),
        in_specs=list(in_specs) + [pl.BlockSpec((k, D_MODEL), lambda i: (0, 0)), row, vec, vec],
        out_specs=row,
        out_shape=jax.ShapeDtypeStruct((m, D_MODEL), F32),
        compiler_params=_cparams("parallel"),
        name=name,
    )(*ins, w, resid, g, b)


def _cast_prologue(a_ref):
    return a_ref[...].astype(BF16)


def _merge_prologue(o0, o1, o2, l0, l1, l2):
    a0, a1, a2 = l0[...], l1[...], l2[...]
    mx = jnp.maximum(jnp.maximum(a0, a1), a2)
    e0, e1, e2 = jnp.exp(a0 - mx), jnp.exp(a1 - mx), jnp.exp(a2 - mx)
    o = (e0 * o0[...] + e1 * o1[...] + e2 * o2[...]) / (e0 + e1 + e2)
    return o.astype(BF16)


def _moe_gate(logits, group=None):
    lane = lax.broadcasted_iota(jnp.int32, logits.shape, 1).astype(F32)
    neg = -jnp.inf
    big = 4.0 * LANES
    gl = jnp.where((lane >= MOE_GE) & (lane < MOE_GE + MOE_GROUPS), logits, neg)
    gmax = jnp.max(gl, axis=-1, keepdims=True)
    den = jnp.sum(jnp.exp(gl - gmax), axis=-1, keepdims=True)
    if group is None:
        g_idx = jnp.min(jnp.where(gl == gmax, lane - MOE_GE, big), axis=-1, keepdims=True)
        g_top = 1.0 / den
    else:
        g_idx = group
        g_sel = jnp.sum(jnp.where(lane == MOE_GE + group, logits, 0.0), axis=-1, keepdims=True)
        g_top = jnp.exp(g_sel - gmax) / den
    lo = g_idx * MOE_EXPERTS
    el = jnp.where((lane >= lo) & (lane < lo + MOE_EXPERTS), logits, neg)
    m1 = jnp.max(el, axis=-1, keepdims=True)
    i1 = jnp.min(jnp.where(el == m1, lane, big), axis=-1, keepdims=True)
    el2 = jnp.where(lane == i1, neg, el)
    m2 = jnp.max(el2, axis=-1, keepdims=True)
    i2 = jnp.min(jnp.where(el2 == m2, lane, big), axis=-1, keepdims=True)
    r = jnp.exp(m2 - m1)
    w1 = g_top / (1.0 + r)
    w2 = w1 * r
    return jnp.where(lane == i1, w1, 0.0) + jnp.where(lane == i2, w2, 0.0), g_idx


def _gate_columns(gate, first):
    lane = lax.broadcasted_iota(jnp.int32, gate.shape, 1)
    return jnp.concatenate(
        [jnp.broadcast_to(jnp.sum(jnp.where(lane == first + e, gate, 0.0), axis=-1, keepdims=True),
                          (gate.shape[0], MOE_FF)) for e in range(MOE_EXPERTS)], axis=1)


def _group_ffn(xb, gate, first, w1_ref, w3_ref, w2_ref):
    h1 = jnp.concatenate([jnp.dot(xb, w1_ref[e], preferred_element_type=F32) for e in range(MOE_EXPERTS)], axis=1)
    h3 = jnp.concatenate([jnp.dot(xb, w3_ref[e], preferred_element_type=F32) for e in range(MOE_EXPERTS)], axis=1)
    hg = (h1 * _sigmoid(h1) * h3 * _gate_columns(gate, first)).astype(BF16)
    return jnp.dot(hg, w2_ref[0], preferred_element_type=F32)


def _moe_body(x_ref, wr_ref, w1_ref, w3_ref, w2_ref, g_ref, b_ref, o_ref, xb_ref, gate_ref, acc_ref):
    gi = pl.program_id(1)

    @pl.when(gi == 0)
    def _route():
        x = x_ref[...]
        xb_ref[...] = x.astype(BF16)
        logits = jnp.dot(x, wr_ref[...], precision=HIGHEST, preferred_element_type=F32)
        gate_ref[...] = _moe_gate(logits)[0]
        acc_ref[...] = jnp.zeros_like(acc_ref)

    acc_ref[...] += _group_ffn(xb_ref[...], gate_ref[...], gi * MOE_EXPERTS, w1_ref, w3_ref, w2_ref)

    @pl.when(gi == MOE_GROUPS - 1)
    def _finish():
        y = ALPHA * x_ref[...] + acc_ref[...]
        o_ref[...] = _ln_rows(y, g_ref[...], b_ref[...])


def _moe(x, wr, w1g, w3g, w2g, g, b, bm):
    m = x.shape[0]
    eff = MOE_EXPERTS * MOE_FF
    row = pl.BlockSpec((bm, D_MODEL), lambda i, e: (i, 0))
    vec = pl.BlockSpec((1, D_MODEL), lambda i, e: (0, 0))
    return pl.pallas_call(
        _moe_body,
        grid=(m // bm, MOE_GROUPS),
        in_specs=[row,
                  pl.BlockSpec((D_MODEL, LANES), lambda i, e: (0, 0)),
                  pl.BlockSpec((MOE_EXPERTS, D_MODEL, MOE_FF), lambda i, e: (e, 0, 0)),
                  pl.BlockSpec((MOE_EXPERTS, D_MODEL, MOE_FF), lambda i, e: (e, 0, 0)),
                  pl.BlockSpec((1, eff, D_MODEL), lambda i, e: (e, 0, 0)),
                  vec, vec],
        out_specs=row,
        out_shape=jax.ShapeDtypeStruct((m, D_MODEL), F32),
        scratch_shapes=[pltpu.VMEM((bm, D_MODEL), BF16),
                        pltpu.VMEM((bm, LANES), F32),
                        pltpu.VMEM((bm, D_MODEL), F32)],
        compiler_params=_cparams("parallel", "arbitrary"),
        name="moe",
    )(x, wr, w1g, w3g, w2g, g, b)


def _router_logits(x, xh, wr_ref):
    xl = (x - xh.astype(F32)).astype(BF16)
    a = jnp.dot(xh, wr_ref[...], preferred_element_type=F32)
    return a[:, :LANES] + a[:, LANES:] + jnp.dot(xl, wr_ref[:, :LANES], preferred_element_type=F32)


def _router_body(x_ref, wr_ref, gidx_ref, cnt_ref):
    x = x_ref[...]
    _, g_idx = _moe_gate(_router_logits(x, x.astype(BF16), wr_ref))
    bm = x.shape[0]
    t = lax.broadcasted_iota(jnp.int32, (bm, LANES), 0)
    lane = lax.broadcasted_iota(jnp.int32, (bm, LANES), 1)
    spread = jnp.where((t & (LANES - 1)) == lane, g_idx, 0.0).astype(BF16)
    r_i = lax.broadcasted_iota(jnp.int32, (bm // LANES, bm), 0)
    t_i = lax.broadcasted_iota(jnp.int32, (bm // LANES, bm), 1)
    sel = (jnp.right_shift(t_i, LANES.bit_length() - 1) == r_i).astype(BF16)
    gidx_ref[...] = jnp.dot(sel, spread, preferred_element_type=F32)
    onehot = (lane.astype(F32) == g_idx).astype(F32)
    cnt_ref[...] = jnp.broadcast_to(jnp.sum(onehot, axis=0, keepdims=True), cnt_ref.shape)


def _router(x, wr, m, bm):
    assert bm % (8 * LANES) == 0 and m % bm == 0
    rows = bm // LANES
    small = pl.BlockSpec((rows, LANES), lambda i: (i, 0))
    return pl.pallas_call(
        _router_body,
        grid=(m // bm,),
        in_specs=[pl.BlockSpec((bm, D_MODEL), lambda i: (i, 0)),
                  pl.BlockSpec((D_MODEL, 2 * LANES), lambda i: (0, 0))],
        out_specs=[small, small],
        out_shape=[jax.ShapeDtypeStruct((m // LANES, LANES), F32),
                   jax.ShapeDtypeStruct((m // LANES, LANES), F32)],
        compiler_params=_cparams("parallel"),
        name="moe_router",
    )(x, wr)


def _moe_sorted_body(tg_ref, grp_ref, start_ref, x_hbm, wr_ref, w1_ref, w3_ref, w2_ref, g_ref, b_ref, o_hbm,
                     xbuf, obuf, src_ref, dst_ref, pos_ref, xsem, osem, *, bm, n):
    i = pl.program_id(0)
    last = pl.num_programs(0) - 1
    slot = i % 2
    shift = bm.bit_length() - 1

    def plan():
        def init(u, carry):
            src_ref[u] = 0
            dst_ref[u] = n + (((u >> shift) + 1) & 1) * bm + (u & (bm - 1))
            return carry

        def fill(t, carry):
            g = grp_ref[t]
            d = pos_ref[g]
            pos_ref[g] = d + 1
            src_ref[d] = t
            dst_ref[bm + d] = t
            return carry

        for g in range(MOE_GROUPS):
            pos_ref[g] = start_ref[g]
        lax.fori_loop(0, src_ref.shape[0], init, 0, unroll=8)
        lax.fori_loop(0, n, fill, 0, unroll=8)

    def gather(tile, s):
        for j in range(bm):
            r = src_ref[tile * bm + j]
            pltpu.make_async_copy(x_hbm.at[pl.ds(r, 1), :], xbuf.at[s, pl.ds(j, 1), :], xsem.at[s]).start()

    def wait_gather(s):
        pltpu.make_async_copy(x_hbm.at[pl.ds(0, bm), :], xbuf.at[s], xsem.at[s]).wait()

    def scatter(tile, s):
        for j in range(bm):
            r = dst_ref[(tile + 1) * bm + j]
            pltpu.make_async_copy(obuf.at[s, pl.ds(j, 1), :], o_hbm.at[pl.ds(r, 1), :], osem.at[s]).start()

    def wait_scatter(s):
        pltpu.make_async_copy(obuf.at[s], o_hbm.at[pl.ds(0, bm), :], osem.at[s]).wait()

    @pl.when(i == 0)
    def _first():
        plan()
        gather(0, 0)
        obuf[1] = jnp.zeros(obuf.shape[1:], F32)

    wait_gather(slot)

    @pl.when(i >= 1)
    def _reuse():
        wait_scatter(slot)

    gather(i + 1, 1 - slot)
    scatter(i - 1, 1 - slot)
    x = xbuf[slot]
    xb = x.astype(BF16)
    gate, _ = _moe_gate(_router_logits(x, xb, wr_ref), tg_ref[i].astype(F32))
    ffn = _group_ffn(xb, gate, tg_ref[i] * MOE_EXPERTS, w1_ref, w3_ref, w2_ref)
    obuf[slot] = _ln_rows(ALPHA * x + ffn, g_ref[...], b_ref[...])

    @pl.when(i == last)
    def _drain():
        scatter(i, slot)
        wait_gather(1 - slot)
        wait_scatter(1 - slot)
        wait_scatter(slot)


def _moe_sorted(tile_group, grp, starts, x, wr2, w1g, w3g, w2g, g, b, bm):
    n_tiles = tile_group.shape[0]
    n = grp.shape[0]
    assert bm & (bm - 1) == 0
    eff = MOE_EXPERTS * MOE_FF
    vec = pl.BlockSpec((1, D_MODEL), lambda i, tg, gr, st: (0, 0))
    hbm = pl.BlockSpec(memory_space=pl.ANY)
    return pl.pallas_call(
        functools.partial(_moe_sorted_body, bm=bm, n=n),
        grid_spec=pltpu.PrefetchScalarGridSpec(
            num_scalar_prefetch=3,
            grid=(n_tiles,),
            in_specs=[hbm,
                      pl.BlockSpec((D_MODEL, 2 * LANES), lambda i, tg, gr, st: (0, 0)),
                      pl.BlockSpec((MOE_EXPERTS, D_MODEL, MOE_FF), lambda i, tg, gr, st: (tg[i], 0, 0)),
                      pl.BlockSpec((MOE_EXPERTS, D_MODEL, MOE_FF), lambda i, tg, gr, st: (tg[i], 0, 0)),
                      pl.BlockSpec((1, eff, D_MODEL), lambda i, tg, gr, st: (tg[i], 0, 0)),
                      vec, vec],
            out_specs=hbm,
            scratch_shapes=[pltpu.VMEM((2, bm, D_MODEL), F32),
                            pltpu.VMEM((2, bm, D_MODEL), F32),
                            pltpu.SMEM(((n_tiles + 1) * bm,), jnp.int32),
                            pltpu.SMEM(((n_tiles + 1) * bm,), jnp.int32),
                            pltpu.SMEM((MOE_GROUPS,), jnp.int32),
                            pltpu.SemaphoreType.DMA((2,)),
                            pltpu.SemaphoreType.DMA((2,))]),
        out_shape=jax.ShapeDtypeStruct((n + 2 * bm, D_MODEL), F32),
        compiler_params=_cparams("arbitrary"),
        name="moe_sorted",
    )(tile_group, grp, starts, x, wr2, w1g, w3g, w2g, g, b)


def _moe_dispatch(x, n, wr, w1g, w3g, w2g, g, b, bm):
    wr_hi = wr.astype(BF16)
    wr_lo = (wr - wr_hi.astype(F32)).astype(BF16)
    wr2 = jnp.concatenate([wr_hi, wr_lo], axis=1)
    rt = 8 * LANES
    gidx, cnt = _router(x, wr2, n, rt)
    grp = gidx.astype(jnp.int32).reshape(n)
    counts = jnp.sum(cnt.reshape(n // rt, rt // LANES, LANES)[:, 0, :MOE_GROUPS], axis=0).astype(jnp.int32)
    padded = (counts + bm - 1) // bm * bm
    ends = jnp.cumsum(padded)
    starts = ends - padded
    n_tiles = n // bm + MOE_GROUPS
    tile_start = jnp.arange(n_tiles, dtype=jnp.int32) * bm
    tile_group = jnp.minimum(jnp.sum(tile_start[:, None] >= ends[None, :], axis=1), MOE_GROUPS - 1)
    return _moe_sorted(tile_group.astype(jnp.int32), grp, starts.astype(jnp.int32), x, wr2,
                       w1g, w3g, w2g, g, b, bm)


def _hgrn_gates(f, lb):
    log_sig = jnp.minimum(f, 0.0) - jnp.log1p(jnp.exp(-jnp.abs(f)))
    a = jnp.log1p(-lb) + log_sig
    log_lb = jnp.log(lb)
    log_f = jnp.maximum(log_lb, a) + jnp.log1p(jnp.exp(-jnp.abs(log_lb - a)))
    k = (1.0 - lb) / (1.0 + jnp.exp(f))
    return log_f, k


def _hgrn_finish(o, gate, ng):
    o = o * lax.rsqrt(jnp.mean(o * o, axis=-1, keepdims=True) + RMS_EPS) * ng
    return o * gate


def _rows_of(x, idx, n):
    return jnp.concatenate([jnp.broadcast_to(x[i:i + 1, :], (n, x.shape[1])) for i in idx], axis=0)


def _hgrn_scan_body(q_ref, f_ref, v_ref, g_ref, lb_ref, ng_ref, tri_ref, lm_ref, bsel_ref, dm_ref,
                    o_ref, s_ref, st_ref, *, n_super):
    C, c, N = HGRN_CHUNK, HGRN_SUB, HGRN_SUPER
    neg = -1e30
    tb = pl.program_id(2)

    @pl.when(tb == 0)
    def _init():
        st_ref[...] = jnp.zeros_like(st_ref)

    lb = lb_ref[...]
    ng = ng_ref[...]
    row = lax.broadcasted_iota(jnp.int32, (N, 1), 0)
    srow = lax.broadcasted_iota(jnp.int32, (c, 1), 0)
    levels = [(C >> l, C >> (l + 1)) for l in range(lm_ref.shape[0])]

    def super_chunk(si, carry):
        r0 = pl.multiple_of(si * N, N)
        q = q_ref[pl.ds(r0, N), :]
        v = v_ref[pl.ds(r0, N), :].astype(BF16)
        gate = g_ref[pl.ds(r0, N), :]
        log_f, k = _hgrn_gates(f_ref[pl.ds(r0, N), :], lb)

        hi = log_f.astype(BF16)
        r1 = log_f - hi.astype(F32)
        mid = r1.astype(BF16)
        lo = (r1 - mid.astype(F32)).astype(BF16)
        cs = jnp.dot(tri_ref[...], jnp.concatenate([hi, mid, lo], axis=1), preferred_element_type=F32)
        b2 = (cs[:, :A_DK] + cs[:, A_DK:2 * A_DK] + cs[:, 2 * A_DK:]) * LOG2E

        att_t = None
        for li, (blk, half) in enumerate(levels):
            up = (row & (blk - 1)) >= half
            ref = _rows_of(b2, [blk * m + half - 1 for m in range(N // blk)], blk)
            e = jnp.exp2(jnp.where(up, b2 - ref, ref - b2))
            ql = jnp.where(up, q * e, 0.0).astype(BF16)
            kl = jnp.where(up, 0.0, k * e).astype(BF16)
            term = _nt_dot(kl, ql) * lm_ref[li]
            att_t = term if att_t is None else att_t + term

        ys = []
        for i in range(N // c):
            base = c * i
            kb = k[base:base + c, :]
            bb = b2[base:base + c, :]
            units = []
            for tl in range(c):
                d = jnp.where(srow <= tl, b2[base + tl:base + tl + 1, :] - bb, neg)
                units.append(q[base + tl:base + tl + 1, :] * kb * jnp.exp2(d))
            ys.append(jnp.concatenate(units, axis=1))
        y = jnp.concatenate(ys, axis=0).astype(BF16)
        r = jnp.dot(y, bsel_ref[...], preferred_element_type=F32)
        diag = jnp.concatenate([r * dm_ref[0], r * dm_ref[1]], axis=1)
        o_intra = _tn_dot((att_t + diag).astype(BF16), v)

        bl = _rows_of(b2, [C * m + C - 1 for m in range(N // C)], C)
        qe = (q * jnp.exp2(b2)).astype(BF16)
        kd = (k * jnp.exp2(bl - b2)).astype(BF16)
        st = st_ref[...]
        nc = N // C
        states, lhs = [], []
        zero = jnp.zeros((C, A_DK), BF16)
        for m in range(nc):
            rows = slice(C * m, C * m + C)
            states.append(st.astype(BF16))
            lhs.append(jnp.concatenate([qe[rows, :] if j == m else zero for j in range(nc)], axis=1))
            st = st * jnp.exp2(b2[C * m + C - 1:C * m + C, :]) + _tn_dot(v[rows, :], kd[rows, :])
        st_ref[...] = st
        o = o_intra + _nt_dot(jnp.concatenate(lhs, axis=0), jnp.concatenate(states, axis=1))
        o_ref[pl.ds(r0, N), :] = _hgrn_finish(o, gate, ng).astype(o_ref.dtype)
        return carry

    for si in range(n_super):
        super_chunk(si, 0)

    @pl.when(tb == pl.num_programs(2) - 1)
    def _emit():
        s_ref[0, 0] = st_ref[...].T


def _hgrn_scan(p, lb, ng, batch, seq, tb):
    nt = seq // tb
    nh = A_HEADS
    n, c = HGRN_SUPER, HGRN_SUB
    assert n == 2 * LANES and LANES % c == 0
    s_i = jnp.arange(n)[:, None]
    t_i = jnp.arange(n)[None, :]
    tri = ((s_i >= t_i) & (s_i // HGRN_CHUNK == t_i // HGRN_CHUNK)).astype(BF16)
    blocks = []
    blk = HGRN_CHUNK
    while blk > c:
        blocks.append(blk)
        blk //= 2
    lm = jnp.stack([(s_i // bk == t_i // bk) for bk in blocks]).astype(F32)
    lane = jnp.arange(LANES)[None, :]
    bsel = (jnp.arange(c * A_DK)[:, None] // A_DK == lane % c).astype(BF16)
    dm = jnp.stack([(s_i // c == g * (LANES // c) + lane // c) for g in range(2)]).astype(F32)

    def col(off):
        return pl.BlockSpec((tb, A_DK), lambda b, h, t: (b * nt + t, off * nh + h))

    def const(a):
        return pl.BlockSpec(a.shape, lambda b, h, t: (0,) * a.ndim)

    return pl.pallas_call(
        functools.partial(_hgrn_scan_body, n_super=tb // n),
        grid=(batch, nh, nt),
        in_specs=[col(0), col(1), col(2), col(3),
                  pl.BlockSpec((1, A_DK), lambda b, h, t: (0, h)),
                  pl.BlockSpec((1, A_DV), lambda b, h, t: (0, 0)),
                  const(tri), const(lm), const(bsel), const(dm)],
        out_specs=[pl.BlockSpec((tb, A_DV), lambda b, h, t: (b * nt + t, h)),
                   pl.BlockSpec((1, 1, A_DK, A_DV), lambda b, h, t: (b, h, 0, 0))],
        out_shape=[jax.ShapeDtypeStruct((batch * seq, nh * A_DV), BF16),
                   jax.ShapeDtypeStruct((batch, nh, A_DK, A_DV), F32)],
        scratch_shapes=[pltpu.VMEM((A_DV, A_DK), F32)],
        compiler_params=_cparams("parallel", "parallel", "arbitrary"),
        name="hgrn_scan",
    )(p, p, p, p, lb, ng, tri, lm, bsel, dm)


def _hgrn_step_body(p_ref, s0_ref, lb_ref, ng_ref, o_ref, s_ref, *, n_tok):
    R = p_ref.shape[0]
    row = lax.broadcasted_iota(jnp.int32, (R, 1), 0)
    valid = row < n_tok
    r2 = lax.broadcasted_iota(jnp.int32, (R, R), 0)
    c2 = lax.broadcasted_iota(jnp.int32, (R, R), 1)
    tril = (r2 >= c2).astype(F32)
    ng = ng_ref[...]
    for h in range(A_HEADS):
        sl = slice(h * A_DK, (h + 1) * A_DK)
        q = p_ref[:, sl]
        v = p_ref[:, 2 * A_WIDTH + h * A_DV:2 * A_WIDTH + (h + 1) * A_DV]
        gate = p_ref[:, 3 * A_WIDTH + h * A_DV:3 * A_WIDTH + (h + 1) * A_DV]
        log_f, k = _hgrn_gates(p_ref[:, A_WIDTH + h * A_DK:A_WIDTH + (h + 1) * A_DK], lb_ref[:, sl])
        b = jnp.dot(tril, log_f, precision=HIGHEST, preferred_element_type=F32)
        st = s0_ref[0, h].T
        o = _nt_dot((q * jnp.exp(b)).astype(BF16), st.astype(BF16))
        for s in range(n_tok):
            m = row >= s
            w = jnp.where(m, q * k[s:s + 1, :] * jnp.exp(jnp.where(m, b - b[s:s + 1, :], 0.0)), 0.0)
            o = o + jnp.sum(w, axis=-1, keepdims=True) * v[s:s + 1, :]
        o_ref[:, h * A_DV:(h + 1) * A_DV] = _hgrn_finish(o, gate, ng)
        bl = b[n_tok - 1:n_tok, :]
        kd = jnp.where(valid, k * jnp.exp(jnp.where(valid, bl - b, 0.0)), 0.0)
        st_new = st * jnp.exp(bl) + _tn_dot(v.astype(BF16), kd.astype(BF16))
        s_ref[0, h] = st_new.T


def _hgrn_step(p, s0, lb, ng, batch, n_tok):
    rows = p.shape[0] // batch
    return pl.pallas_call(
        functools.partial(_hgrn_step_body, n_tok=n_tok),
        grid=(batch,),
        in_specs=[pl.BlockSpec((rows, 4 * A_WIDTH), lambda b: (b, 0)),
                  pl.BlockSpec((1, A_HEADS, A_DK, A_DV), lambda b: (b, 0, 0, 0)),
                  pl.BlockSpec((1, A_WIDTH), lambda b: (0, 0)),
                  pl.BlockSpec((1, A_DV), lambda b: (0, 0))],
        out_specs=[pl.BlockSpec((rows, A_HEADS * A_DV), lambda b: (b, 0)),
                   pl.BlockSpec((1, A_HEADS, A_DK, A_DV), lambda b: (b, 0, 0, 0))],
        out_shape=[jax.ShapeDtypeStruct((batch * rows, A_HEADS * A_DV), F32),
                   jax.ShapeDtypeStruct((batch, A_HEADS, A_DK, A_DV), F32)],
        compiler_params=_cparams("parallel"),
        name="hgrn_step",
    )(p, s0, lb, ng)


def _gmlp_prologue(u_ref, v_ref, ws_ref, bs_ref, gated_ref, *, chunk):
    bm = u_ref.shape[0]
    r2 = lax.broadcasted_iota(jnp.int32, (chunk, chunk), 0)
    c2 = lax.broadcasted_iota(jnp.int32, (chunk, chunk), 1)
    causal = r2 >= c2
    for h in range(B_HEADS):
        wc = jnp.where(causal, ws_ref[h], 0.0).astype(BF16)
        bias = bs_ref[:, h:h + 1]
        cols = slice(h * B_HD, (h + 1) * B_HD)
        for n in range(bm // chunk):
            rows = slice(n * chunk, (n + 1) * chunk)
            mixed = jnp.dot(wc, v_ref[rows, cols].astype(BF16), preferred_element_type=F32) + bias
            gated_ref[rows, cols] = (u_ref[rows, cols] * mixed).astype(BF16)
    return gated_ref[...]


def _gmlp_out_body(u_ref, v_ref, ws_ref, bs_ref, w_ref, r_ref, g_ref, b_ref, o_ref, gated_ref, *, chunk):
    a = _gmlp_prologue(u_ref, v_ref, ws_ref, bs_ref, gated_ref, chunk=chunk)
    acc = jnp.dot(a, w_ref[...], preferred_element_type=F32)
    y = ALPHA * r_ref[...] + acc
    o_ref[...] = _ln_rows(y, g_ref[...], b_ref[...])


def _gmlp_out(uv, ws, bs_t, w, resid, g, b, bm, chunk):
    m = uv.shape[0]
    row = pl.BlockSpec((bm, D_MODEL), lambda i: (i, 0))
    vec = pl.BlockSpec((1, D_MODEL), lambda i: (0, 0))
    return pl.pallas_call(
        functools.partial(_gmlp_out_body, chunk=chunk),
        grid=(m // bm,),
        in_specs=[pl.BlockSpec((bm, D_MODEL), lambda i: (i, 0)),
                  pl.BlockSpec((bm, D_MODEL), lambda i: (i, 1)),
                  pl.BlockSpec(ws.shape, lambda i: (0, 0, 0)),
                  pl.BlockSpec(bs_t.shape, lambda i: (0, 0)),
                  pl.BlockSpec((D_MODEL, D_MODEL), lambda i: (0, 0)),
                  row, vec, vec],
        out_specs=row,
        out_shape=jax.ShapeDtypeStruct((m, D_MODEL), F32),
        scratch_shapes=[pltpu.VMEM((bm, D_MODEL), BF16)],
        compiler_params=_cparams("parallel"),
        name="gmlp_out",
    )(uv, uv, ws, bs_t, w, resid, g, b)


def _band_attn_body(q_ref, k_ref, v_ref, o_ref, l_ref, *, dil, seq):
    bq = C_QBLOCK
    span = C_KEYS - 1
    scale = C_HD ** -0.5
    n_blocks = seq // dil // bq
    qi = lax.broadcasted_iota(jnp.int32, (bq, 2 * bq), 0)
    ki = lax.broadcasted_iota(jnp.int32, (bq, 2 * bq), 1)

    def rows(first, n):
        return pl.ds(first, n) if dil == 1 else pl.ds(first, n, stride=dil)

    for r in range(dil):
        for i in range(n_blocks):
            w = max(i - 1, 0)
            qs = rows(r + dil * bq * i, bq)
            ws = rows(r + dil * bq * w, 2 * bq)
            q = q_ref[qs, :].astype(BF16)
            kw = k_ref[ws, :].astype(BF16)
            vw = v_ref[ws, :].astype(BF16)
            s = _nt_dot(q, kw) * scale
            rel = bq * (i - w) + qi - ki
            s = jnp.where((rel >= 0) & (rel <= span), s, -jnp.inf)
            mx = jnp.max(s, axis=-1, keepdims=True)
            p = jnp.exp(s - mx)
            den = jnp.sum(p, axis=-1, keepdims=True)
            o_ref[qs, :] = jnp.dot(p.astype(BF16), vw, preferred_element_type=F32) / den
            l_ref[qs, :] = jnp.broadcast_to(mx + jnp.log(den), (bq, C_HD))


def _band_attn(qkv, g, dil, batch, seq):
    ng = len(C_GROUPS)

    def col(part):
        return pl.BlockSpec((seq, C_HD), lambda b, h: (b, (part * ng + g) * C_HEADS + h))

    out = pl.BlockSpec((seq, C_HD), lambda b, h: (b, h))
    return pl.pallas_call(
        functools.partial(_band_attn_body, dil=dil, seq=seq),
        grid=(batch, C_HEADS),
        in_specs=[col(0), col(1), col(2)],
        out_specs=[out, out],
        out_shape=[jax.ShapeDtypeStruct((batch * seq, C_HEADS * C_HD), F32)] * 2,
        compiler_params=_cparams("parallel", "parallel"),
        name=f"band_attn_d{dil}",
    )(qkv, qkv, qkv)


def _step_attn_body(qkv_ref, *refs, g, dil, n_tok):
    past_refs, (o_ref, l_ref) = refs[:-2], refs[-2:]
    scale = C_HD ** -0.5
    ng = len(C_GROUPS)
    hs = C_HEADS
    tok = lax.broadcasted_iota(jnp.int32, (n_tok, 1, 1), 0)
    prow = lax.broadcasted_iota(jnp.int32, (past_refs[0].shape[1], 1, 1), 0)
    kn = qkv_ref[0, :, (ng + g) * hs:(ng + g + 1) * hs, :]
    vn = qkv_ref[0, :, (2 * ng + g) * hs:(2 * ng + g + 1) * hs, :]
    for t in range(n_tok):
        past_ref = past_refs[t % dil]
        new_ok = (tok <= t) & (((t - tok) % dil) == 0)
        past_ok = prow >= (t if dil == 1 else 0)
        q = qkv_ref[0, t, g * hs:(g + 1) * hs, :][None]
        kp = past_ref[0, :, 0:hs, :]
        vp = past_ref[0, :, hs:2 * hs, :]
        sp = jnp.where(past_ok, jnp.sum(kp * q, axis=-1, keepdims=True) * scale, -jnp.inf)
        sn = jnp.where(new_ok, jnp.sum(kn * q, axis=-1, keepdims=True) * scale, -jnp.inf)
        mx = jnp.maximum(jnp.max(sp, axis=0, keepdims=True), jnp.max(sn, axis=0, keepdims=True))
        pp = jnp.exp(sp - mx)
        pn = jnp.exp(sn - mx)
        den = jnp.sum(pp, axis=0, keepdims=True) + jnp.sum(pn, axis=0, keepdims=True)
        o = (jnp.sum(pp * vp, axis=0, keepdims=True) + jnp.sum(pn * vn, axis=0, keepdims=True)) / den
        o_ref[0, t] = o[0]
        l_ref[0, t] = jnp.broadcast_to(mx + jnp.log(den), (1, hs, C_HD))[0]


def _step_attn(qkv, cache, g, window, dil, batch, n_tok):
    past = cache.reshape(batch, window // dil, dil, 2 * C_HEADS, C_HD)
    n_res = min(dil, n_tok)
    out = pl.BlockSpec((1, n_tok, C_HEADS, C_HD), lambda b: (b, 0, 0, 0))
    past_specs = [pl.BlockSpec((1, window // dil, None, 2 * C_HEADS, C_HD),
                               functools.partial(lambda b, r: (b, 0, r, 0, 0), r=r)) for r in range(n_res)]
    return pl.pallas_call(
        functools.partial(_step_attn_body, g=g, dil=dil, n_tok=n_tok),
        grid=(batch,),
        in_specs=[pl.BlockSpec((1, n_tok) + qkv.shape[2:], lambda b: (b, 0, 0, 0))] + past_specs,
        out_specs=[out, out],
        out_shape=[jax.ShapeDtypeStruct((batch, n_tok, C_HEADS, C_HD), F32)] * 2,
        compiler_params=_cparams("parallel"),
        name=f"step_attn_d{dil}",
    )(qkv, *([past] * n_res))


def _rope_tables(pos):
    half = C_HD // 2
    inv = ROPE_THETA ** (-jnp.arange(half, dtype=F32) / half)
    ang = pos.astype(F32)[:, None] * inv[None, :]
    cos, sin = jnp.cos(ang), jnp.sin(ang)
    return jnp.concatenate([cos, cos], -1), jnp.concatenate([-sin, sin], -1)


def _row_tile(m, cap):
    return min(m, cap)


def _hgrn_layer(x, batch, seq, s0, w_in, lb, ng, w_out, ln_g, ln_b):
    m = batch * seq
    bn = 1024
    p = _proj(x, w_in, (), (), functools.partial(_hgrn_proj_epilogue, bn=bn), m, _row_tile(m, PROJ_ROWS), bn,
              "hgrn_proj")
    if s0 is None:
        o, s_new = _hgrn_scan(p, lb, ng, batch, seq, HGRN_ROWS if seq % HGRN_ROWS == 0 else HGRN_SUPER)
    else:
        rows = 8
        pp = jnp.pad(p.reshape(batch, seq, -1), ((0, 0), (0, rows - seq), (0, 0))).reshape(batch * rows, -1)
        o, s_new = _hgrn_step(pp, s0, lb, ng, batch, seq)
        o = o.reshape(batch, rows, -1)[:, :seq].reshape(m, -1)
    bm = _row_tile(m, 512)
    x = _out_ln((o,), (pl.BlockSpec((bm, o.shape[1]), lambda i: (i, 0)),), _cast_prologue,
                w_out, x, ln_g, ln_b, bm, "hgrn_out")
    return x, s_new


def _gmlp_layer(x, batch, seq, w_in, b_in, g1, b1, ws, bs, w_out, ln_g, ln_b):
    m = batch * seq
    bm = _row_tile(m, 512)
    vec = pl.BlockSpec((1, D_MODEL), lambda i, j: (0, 0))
    uv = _proj(x, w_in, (b_in, g1, b1),
               (pl.BlockSpec((1, D_MODEL), lambda i, j: (0, j)), vec, vec),
               _gmlp_proj_epilogue, m, bm, D_MODEL, "gmlp_proj")
    if seq % B_CHUNK == 0:
        chunk, ws_c, bs_t = B_CHUNK, ws, bs.T
    else:
        chunk = m
        eye = jnp.eye(batch, dtype=ws.dtype)
        ws_c = jnp.einsum("ab,hts->hatbs", eye, ws[:, :seq, :seq]).reshape(B_HEADS, m, m)
        bs_t = jnp.tile(bs[:, :seq].T, (batch, 1))
    x = _gmlp_out(uv, ws_c, bs_t, w_out, x, ln_g, ln_b, _row_tile(m, 256), chunk)
    return x, uv


def _attn_layer(x, batch, seq, caches, pos0, w_in, w_out, ln_g, ln_b):
    m = batch * seq
    bn = 1024
    bm = _row_tile(m, PROJ_ROWS)
    cos, sin = _rope_tables(pos0 + jnp.arange(seq, dtype=jnp.int32))
    cos, sin = jnp.tile(cos, (batch, 1)), jnp.tile(sin, (batch, 1))
    tab = pl.BlockSpec((bm, C_HD), lambda i, j: (i, 0))
    qkv = _proj(x, w_in, (cos, sin), (tab, tab), functools.partial(_attn_proj_epilogue, bn=bn), m, bm, bn,
                "attn_proj")
    qkv3 = qkv.reshape(batch, seq, C_QKV)
    outs, lses = [], []
    for g, (window, dil) in enumerate(C_GROUPS):
        if caches is None:
            o, lse = _band_attn(qkv, g, dil, batch, seq)
        else:
            o, lse = _step_attn(qkv.reshape(batch, seq, C_QKV // C_HD, C_HD), caches[g], g, window, dil,
                                batch, seq)
            o, lse = o.reshape(m, -1), lse.reshape(m, -1)
        outs.append(o)
        lses.append(lse)
    bm2 = _row_tile(m, 256)
    spec = pl.BlockSpec((bm2, C_HEADS * C_HD), lambda i: (i, 0))
    x = _out_ln(tuple(outs) + tuple(lses), (spec,) * 6, _merge_prologue, w_out, x, ln_g, ln_b, bm2,
                "attn_out")
    hw = C_HEADS * C_HD
    ng = len(C_GROUPS)
    kv = []
    for g, (window, _) in enumerate(C_GROUPS):
        tail = qkv3[:, seq - min(window, seq):]
        k = tail[:, :, (ng + g) * hw:(ng + g + 1) * hw].reshape(batch, -1, C_HEADS, C_HD)
        v = tail[:, :, (2 * ng + g) * hw:(2 * ng + g + 1) * hw].reshape(batch, -1, C_HEADS, C_HD)
        kv.append(jnp.stack([k, v], axis=2))
    return x, kv


def kernel(x_prompt, x_sample, state_hgrn, cache_c_kv_w128, cache_c_kv_w512, cache_c_kv_w2048, ln_g, ln_b, a_w_in, a_lb_logits, a_norm_g, a_w_out, b_w_in, b_b_in, b_ln_g, b_ln_b, b_w_s, b_b_s, b_w_out, c_w_in, c_w_out, moe_w_group, moe_w_expert, moe_w1, moe_w3, moe_w2):
    bp, tp, _ = x_prompt.shape
    bs, ts, _ = x_sample.shape
    assert tp % HGRN_SUPER == 0 and tp % B_CHUNK == 0 and tp // C_GROUPS[-1][1] >= 2 * C_QBLOCK
    assert ts <= 8 and ts <= C_GROUPS[1][1] and (bp * tp) % MOE_TILE == 0

    lb_p = jax.nn.softmax(a_lb_logits.astype(F32), axis=0)
    lb_all = jnp.clip(jnp.cumsum(lb_p, axis=0) - lb_p[0:1], 0.0, 1.0 - 1e-6)
    caches = (cache_c_kv_w128, cache_c_kv_w512, cache_c_kv_w2048)

    xp = x_prompt.reshape(bp * tp, D_MODEL)
    xs = x_sample.reshape(bs * ts, D_MODEL)
    hgrn_p, hgrn_s, chunk_v_s = [], [], []
    kv_p = [[] for _ in C_GROUPS]
    kv_s = [[] for _ in C_GROUPS]

    for i in range(DEPTH):
        kind, j = i % 3, i // 3
        g0, b0 = ln_g[i, 0][None], ln_b[i, 0][None]
        if kind == 0:
            w_in, w_out = a_w_in[j].astype(BF16), a_w_out[j].astype(BF16)
            lb, ng = lb_all[j][None], a_norm_g[j][None]
            xp, sp = _hgrn_layer(xp, bp, tp, None, w_in, lb, ng, w_out, g0, b0)
            xs, ss = _hgrn_layer(xs, bs, ts, state_hgrn[j].astype(F32), w_in, lb, ng, w_out, g0, b0)
            hgrn_p.append(sp)
            hgrn_s.append(ss)
        elif kind == 1:
            w_in, w_out = b_w_in[j].astype(BF16), b_w_out[j].astype(BF16)
            args = (w_in, b_b_in[j][None], b_ln_g[j][None], b_ln_b[j][None], b_w_s[j], b_b_s[j], w_out, g0, b0)
            xp, _ = _gmlp_layer(xp, bp, tp, *args)
            xs, uvs = _gmlp_layer(xs, bs, ts, *args)
            chunk_v_s.append(uvs[:, D_MODEL:].reshape(bs, ts, D_MODEL))
        else:
            w_in, w_out = c_w_in[j].astype(BF16), c_w_out[j].astype(BF16)
            xp, kvp = _attn_layer(xp, bp, tp, None, 0, w_in, w_out, g0, b0)
            xs, kvs = _attn_layer(xs, bs, ts, tuple(c[j] for c in caches), PAST_LEN, w_in, w_out, g0, b0)
            for g in range(len(C_GROUPS)):
                kv_p[g].append(kvp[g])
                kv_s[g].append(kvs[g])
        wr = jnp.pad(jnp.concatenate([moe_w_expert[i], moe_w_group[i]], axis=1),
                     ((0, 0), (0, LANES - MOE_GE - MOE_GROUPS)))
        w1g, w3g = moe_w1[i].astype(BF16), moe_w3[i].astype(BF16)
        w2g = moe_w2[i].astype(BF16).reshape(MOE_GROUPS, MOE_EXPERTS * MOE_FF, D_MODEL)
        g1, b1 = ln_g[i, 1][None], ln_b[i, 1][None]
        xp = _moe_dispatch(xp, bp * tp, wr, w1g, w3g, w2g, g1, b1, MOE_TILE)
        xs = _moe(xs, wr, w1g, w3g, w2g, g1, b1, bs * ts)

    return (xp[:bp * tp].reshape(bp, tp, D_MODEL), xs.reshape(bs, ts, D_MODEL),
            jnp.stack(hgrn_p), jnp.stack(hgrn_s), jnp.stack(chunk_v_s),
            jnp.stack(kv_p[0]), jnp.stack(kv_s[0]), jnp.stack(kv_p[1]), jnp.stack(kv_s[1]),
            jnp.stack(kv_p[2]), jnp.stack(kv_s[2]))
```

```python
import functools

import jax
import jax.numpy as jnp
from jax import lax
from jax.experimental import pallas as pl
from jax.experimental.pallas import tpu as pltpu

F32 = jnp.float32
BF16 = jnp.bfloat16
HIGHEST = lax.Precision.HIGHEST

D_MODEL = 2048
DEPTH = 4
PAST_LEN = 16384
A_HEADS = 16
A_DK = 128
A_DV = 128
A_WIDTH = A_HEADS * A_DK
HGRN_CHUNK = 64
HGRN_SUB = 8
LOG2E = 1.4426950408889634
HGRN_SUPER = 256
HGRN_ROWS = 1024
B_CHUNK = 128
B_HEADS = 16
B_HD = 128
C_HEADS = 8
C_HD = 128
C_GROUPS = ((128, 1), (512, 4), (2048, 16))
C_KEYS = 129
C_QBLOCK = 128
C_QKV = 3 * len(C_GROUPS) * C_HEADS * C_HD
ROPE_THETA = 10000.0
MOE_GROUPS = 4
MOE_EXPERTS = 4
MOE_GE = MOE_GROUPS * MOE_EXPERTS
MOE_FF = 256
MOE_TILE = 256
LN_EPS = 1e-5
RMS_EPS = 1e-6
ALPHA = (2 * DEPTH) ** 0.25
LANES = 128
VMEM_LIMIT = 56 * 1024 * 1024
PROJ_ROWS = 1024
PROJ_CHUNK = 256


def _cparams(*sem):
    return pltpu.CompilerParams(dimension_semantics=sem, vmem_limit_bytes=VMEM_LIMIT)


def _sigmoid(x):
    return 1.0 / (1.0 + jnp.exp(-x))


def _ln_rows(y, g, b):
    mu = jnp.mean(y, axis=-1, keepdims=True)
    d = y - mu
    var = jnp.mean(d * d, axis=-1, keepdims=True)
    return d * lax.rsqrt(var + LN_EPS) * g + b


def _nt_dot(a, b):
    return lax.dot_general(a, b, (((1,), (1,)), ((), ())), preferred_element_type=F32)


def _tn_dot(a, b):
    return lax.dot_general(a, b, (((0,), (0,)), ((), ())), preferred_element_type=F32)


def _proj_body(x_ref, w_ref, *rest, epilogue, n_extra, chunk):
    extras = rest[:n_extra]
    o_ref = rest[n_extra]
    xb_ref = rest[n_extra + 1]
    j = pl.program_id(1)

    @pl.when(j == 0)
    def _cast():
        xb_ref[...] = x_ref[...].astype(BF16)

    bn = w_ref.shape[1]
    for c in range(bn // chunk):
        cols = slice(c * chunk, (c + 1) * chunk)
        acc = jnp.dot(xb_ref[...], w_ref[:, cols], preferred_element_type=F32)
        epilogue(acc, j, extras, o_ref, cols)


def _proj(x, w, extras, extra_specs, epilogue, m, bm, bn, chunk, name):
    k = x.shape[1]
    n = w.shape[1]
    return pl.pallas_call(
        functools.partial(_proj_body, epilogue=epilogue, n_extra=len(extras), chunk=chunk),
        grid=(m // bm, n // bn),
        in_specs=[pl.BlockSpec((bm, k), lambda i, j: (i, 0)),
                  pl.BlockSpec((k, bn), lambda i, j: (0, j))] + list(extra_specs),
        out_specs=pl.BlockSpec((bm, bn), lambda i, j: (i, j)),
        out_shape=jax.ShapeDtypeStruct((m, n), F32),
        scratch_shapes=[pltpu.VMEM((bm, k), BF16)],
        compiler_params=_cparams("parallel", "arbitrary"),
        name=name,
    )(x, w, *extras)


def _hgrn_proj_epilogue(acc, j, extras, o_ref, cols, *, bn):
    nq = A_WIDTH // bn
    is_silu = jnp.logical_or(j < nq, j >= 3 * nq)
    o_ref[:, cols] = acc * jnp.where(is_silu, _sigmoid(acc), 1.0)


def _gelu_tanh(z):
    return 0.5 * z * (1.0 + jnp.tanh(0.7978845608028654 * (z + 0.044715 * (z * z * z))))


def _gmlp_proj_epilogue(acc, j, extras, o_ref, cols):
    bias_ref, g_ref, b_ref = extras
    z = _gelu_tanh(acc + bias_ref[...])

    @pl.when(j == 0)
    def _():
        o_ref[...] = z

    @pl.when(j == 1)
    def _():
        o_ref[...] = _ln_rows(z, g_ref[...], b_ref[...])


def _attn_proj_epilogue(acc, j, extras, o_ref, cols, *, bn):
    cos_ref, sin_ref = extras
    is_rot = j < 2 * len(C_GROUPS) * C_HEADS * C_HD // bn
    cos = cos_ref[...]
    sin = sin_ref[...]
    for h in range(acc.shape[1] // C_HD):
        xh = acc[:, h * C_HD:(h + 1) * C_HD]
        rot = xh * cos + pltpu.roll(xh, C_HD // 2, 1) * sin
        o_ref[:, cols.start + h * C_HD:cols.start + (h + 1) * C_HD] = jnp.where(is_rot, rot, xh)


def _out_ln_body(*refs, prologue, n_in):
    ins = refs[:n_in]
    w_ref, r_ref, g_ref, b_ref, o_ref = refs[n_in:n_in + 5]
    a = prologue(*ins)
    acc = jnp.dot(a, w_ref[...], preferred_element_type=F32)
    y = ALPHA * r_ref[...] + acc
    o_ref[...] = _ln_rows(y, g_ref[...], b_ref[...])


def _out_ln(ins, in_specs, prologue, w, resid, g, b, bm, name):
    m = ins[0].shape[0]
    k = w.shape[0]
    row = pl.BlockSpec((bm, D_MODEL), lambda i: (i, 0))
    vec = pl.BlockSpec((1, D_MODEL), lambda i: (0, 0))
    return pl.pallas_call(
        functools.partial(_out_ln_body, prologue=prologue, n_in=len(ins)),
        grid=(m // bm,),
        in_specs=list(in_specs) + [pl.BlockSpec((k, D_MODEL), lambda i: (0, 0)), row, vec, vec],
        out_specs=row,
        out_shape=jax.ShapeDtypeStruct((m, D_MODEL), F32),
        compiler_params=_cparams("parallel"),
        name=name,
    )(*ins, w, resid, g, b)


def _cast_prologue(a_ref):
    return a_ref[...].astype(BF16)


def _merge_prologue(o0, o1, o2, l0, l1, l2):
    a0, a1, a2 = l0[...], l1[...], l2[...]
    mx = jnp.maximum(jnp.maximum(a0, a1), a2)
    e0, e1, e2 = jnp.exp(a0 - mx), jnp.exp(a1 - mx), jnp.exp(a2 - mx)
    o = (e0 * o0[...] + e1 * o1[...] + e2 * o2[...]) / (e0 + e1 + e2)
    return o.astype(BF16)


def _moe_gate(logits, group=None):
    lane = lax.broadcasted_iota(jnp.int32, logits.shape, 1).astype(F32)
    neg = -jnp.inf
    big = 4.0 * LANES
    gl = jnp.where((lane >= MOE_GE) & (lane < MOE_GE + MOE_GROUPS), logits, neg)
    gmax = jnp.max(gl, axis=-1, keepdims=True)
    den = jnp.sum(jnp.exp(gl - gmax), axis=-1, keepdims=True)
    if group is None:
        g_idx = jnp.min(jnp.where(gl == gmax, lane - MOE_GE, big), axis=-1, keepdims=True)
        g_top = 1.0 / den
    else:
        g_idx = group
        g_sel = jnp.sum(jnp.where(lane == MOE_GE + group, logits, 0.0), axis=-1, keepdims=True)
        g_top = jnp.exp(g_sel - gmax) / den
    lo = g_idx * MOE_EXPERTS
    el = jnp.where((lane >= lo) & (lane < lo + MOE_EXPERTS), logits, neg)
    m1 = jnp.max(el, axis=-1, keepdims=True)
    i1 = jnp.min(jnp.where(el == m1, lane, big), axis=-1, keepdims=True)
    el2 = jnp.where(lane == i1, neg, el)
    m2 = jnp.max(el2, axis=-1, keepdims=True)
    i2 = jnp.min(jnp.where(el2 == m2, lane, big), axis=-1, keepdims=True)
    r = jnp.exp(m2 - m1)
    w1 = g_top / (1.0 + r)
    w2 = w1 * r
    return jnp.where(lane == i1, w1, 0.0) + jnp.where(lane == i2, w2, 0.0), g_idx


def _gate_columns(gate, first):
    lane = lax.broadcasted_iota(jnp.int32, gate.shape, 1)
    return jnp.concatenate(
        [jnp.broadcast_to(jnp.sum(jnp.where(lane == first + e, gate, 0.0), axis=-1, keepdims=True),
                          (gate.shape[0], MOE_FF)) for e in range(MOE_EXPERTS)], axis=1)


def _group_ffn(xb, gate, first, w1_ref, w3_ref, w2_ref):
    h1 = jnp.concatenate([jnp.dot(xb, w1_ref[e], preferred_element_type=F32) for e in range(MOE_EXPERTS)], axis=1)
    h3 = jnp.concatenate([jnp.dot(xb, w3_ref[e], preferred_element_type=F32) for e in range(MOE_EXPERTS)], axis=1)
    hg = (h1 * _sigmoid(h1) * h3 * _gate_columns(gate, first)).astype(BF16)
    return jnp.dot(hg, w2_ref[0], preferred_element_type=F32)


def _moe_body(x_ref, wr_ref, w1_ref, w3_ref, w2_ref, g_ref, b_ref, o_ref, xb_ref, gate_ref, acc_ref):
    gi = pl.program_id(1)

    @pl.when(gi == 0)
    def _route():
        x = x_ref[...]
        xb_ref[...] = x.astype(BF16)
        logits = jnp.dot(x, wr_ref[...], precision=HIGHEST, preferred_element_type=F32)
        gate_ref[...] = _moe_gate(logits)[0]
        acc_ref[...] = jnp.zeros_like(acc_ref)

    acc_ref[...] += _group_ffn(xb_ref[...], gate_ref[...], gi * MOE_EXPERTS, w1_ref, w3_ref, w2_ref)

    @pl.when(gi == MOE_GROUPS - 1)
    def _finish():
        y = ALPHA * x_ref[...] + acc_ref[...]
        o_ref[...] = _ln_rows(y, g_ref[...], b_ref[...])


def _moe(x, wr, w1g, w3g, w2g, g, b, bm):
    m = x.shape[0]
    eff = MOE_EXPERTS * MOE_FF
    row = pl.BlockSpec((bm, D_MODEL), lambda i, e: (i, 0))
    vec = pl.BlockSpec((1, D_MODEL), lambda i, e: (0, 0))
    return pl.pallas_call(
        _moe_body,
        grid=(m // bm, MOE_GROUPS),
        in_specs=[row,
                  pl.BlockSpec((D_MODEL, LANES), lambda i, e: (0, 0)),
                  pl.BlockSpec((MOE_EXPERTS, D_MODEL, MOE_FF), lambda i, e: (e, 0, 0)),
                  pl.BlockSpec((MOE_EXPERTS, D_MODEL, MOE_FF), lambda i, e: (e, 0, 0)),
                  pl.BlockSpec((1, eff, D_MODEL), lambda i, e: (e, 0, 0)),
                  vec, vec],
        out_specs=row,
        out_shape=jax.ShapeDtypeStruct((m, D_MODEL), F32),
        scratch_shapes=[pltpu.VMEM((bm, D_MODEL), BF16),
                        pltpu.VMEM((bm, LANES), F32),
                        pltpu.VMEM((bm, D_MODEL), F32)],
        compiler_params=_cparams("parallel", "arbitrary"),
        name="moe",
    )(x, wr, w1g, w3g, w2g, g, b)


def _router_logits(x, xh, wr_ref):
    xl = (x - xh.astype(F32)).astype(BF16)
    a = jnp.dot(xh, wr_ref[...], preferred_element_type=F32)
    return a[:, :LANES] + a[:, LANES:] + jnp.dot(xl, wr_ref[:, :LANES], preferred_element_type=F32)


def _router_body(x_ref, wr_ref, gidx_ref, cnt_ref):
    x = x_ref[...]
    _, g_idx = _moe_gate(_router_logits(x, x.astype(BF16), wr_ref))
    bm = x.shape[0]
    t = lax.broadcasted_iota(jnp.int32, (bm, LANES), 0)
    lane = lax.broadcasted_iota(jnp.int32, (bm, LANES), 1)
    spread = jnp.where((t & (LANES - 1)) == lane, g_idx, 0.0).astype(BF16)
    r_i = lax.broadcasted_iota(jnp.int32, (bm // LANES, bm), 0)
    t_i = lax.broadcasted_iota(jnp.int32, (bm // LANES, bm), 1)
    sel = (jnp.right_shift(t_i, LANES.bit_length() - 1) == r_i).astype(BF16)
    gidx_ref[...] = jnp.dot(sel, spread, preferred_element_type=F32)
    onehot = (lane.astype(F32) == g_idx).astype(F32)
    cnt_ref[...] = jnp.broadcast_to(jnp.sum(onehot, axis=0, keepdims=True), cnt_ref.shape)


def _router(x, wr, m, bm):
    assert bm % (8 * LANES) == 0 and m % bm == 0
    rows = bm // LANES
    small = pl.BlockSpec((rows, LANES), lambda i: (i, 0))
    return pl.pallas_call(
        _router_body,
        grid=(m // bm,),
        in_specs=[pl.BlockSpec((bm, D_MODEL), lambda i: (i, 0)),
                  pl.BlockSpec((D_MODEL, 2 * LANES), lambda i: (0, 0))],
        out_specs=[small, small],
        out_shape=[jax.ShapeDtypeStruct((m // LANES, LANES), F32),
                   jax.ShapeDtypeStruct((m // LANES, LANES), F32)],
        compiler_params=_cparams("parallel"),
        name="moe_router",
    )(x, wr)


def _moe_sorted_body(tg_ref, grp_ref, start_ref, x_hbm, wr_ref, w1_ref, w3_ref, w2_ref, g_ref, b_ref, o_hbm,
                     xbuf, obuf, src_ref, dst_ref, xsem, osem, *, bm, n):
    i = pl.program_id(0)
    last = pl.num_programs(0) - 1
    slot = i % 2
    shift = bm.bit_length() - 1

    def plan():
        def init(u, carry):
            src_ref[u] = 0
            dst_ref[u] = n + (((u >> shift) + 1) & 1) * bm + (u & (bm - 1))
            return carry

        def fill(t, pos):
            g = grp_ref[t]
            d = pos[0]
            for k in range(1, MOE_GROUPS):
                d = jnp.where(g == k, pos[k], d)
            src_ref[d] = t
            dst_ref[bm + d] = t
            return tuple(pos[k] + (g == k).astype(jnp.int32) for k in range(MOE_GROUPS))

        lax.fori_loop(0, src_ref.shape[0], init, 0, unroll=8)
        lax.fori_loop(0, n, fill, tuple(start_ref[g] for g in range(MOE_GROUPS)), unroll=8)

    def gather(tile, s):
        for j in range(bm):
            r = src_ref[tile * bm + j]
            pltpu.make_async_copy(x_hbm.at[pl.ds(r, 1), :], xbuf.at[s, pl.ds(j, 1), :], xsem.at[s]).start()

    def wait_gather(s):
        pltpu.make_async_copy(x_hbm.at[pl.ds(0, bm), :], xbuf.at[s], xsem.at[s]).wait()

    def scatter(tile, s):
        for j in range(bm):
            r = dst_ref[(tile + 1) * bm + j]
            pltpu.make_async_copy(obuf.at[s, pl.ds(j, 1), :], o_hbm.at[pl.ds(r, 1), :], osem.at[s]).start()

    def wait_scatter(s):
        pltpu.make_async_copy(obuf.at[s], o_hbm.at[pl.ds(0, bm), :], osem.at[s]).wait()

    @pl.when(i == 0)
    def _first():
        plan()
        gather(0, 0)
        obuf[1] = jnp.zeros(obuf.shape[1:], F32)

    wait_gather(slot)

    @pl.when(i >= 1)
    def _reuse():
        wait_scatter(slot)

    gather(i + 1, 1 - slot)
    scatter(i - 1, 1 - slot)
    x = xbuf[slot]
    xb = x.astype(BF16)
    gate, _ = _moe_gate(_router_logits(x, xb, wr_ref), tg_ref[i].astype(F32))
    ffn = _group_ffn(xb, gate, tg_ref[i] * MOE_EXPERTS, w1_ref, w3_ref, w2_ref)
    obuf[slot] = _ln_rows(ALPHA * x + ffn, g_ref[...], b_ref[...])

    @pl.when(i == last)
    def _drain():
        scatter(i, slot)
        wait_gather(1 - slot)
        wait_scatter(1 - slot)
        wait_scatter(slot)


def _moe_sorted(tile_group, grp, starts, x, wr2, w1g, w3g, w2g, g, b, bm):
    n_tiles = tile_group.shape[0]
    n = grp.shape[0]
    assert bm & (bm - 1) == 0
    eff = MOE_EXPERTS * MOE_FF
    vec = pl.BlockSpec((1, D_MODEL), lambda i, tg, gr, st: (0, 0))
    hbm = pl.BlockSpec(memory_space=pl.ANY)
    return pl.pallas_call(
        functools.partial(_moe_sorted_body, bm=bm, n=n),
        grid_spec=pltpu.PrefetchScalarGridSpec(
            num_scalar_prefetch=3,
            grid=(n_tiles,),
            in_specs=[hbm,
                      pl.BlockSpec((D_MODEL, 2 * LANES), lambda i, tg, gr, st: (0, 0)),
                      pl.BlockSpec((MOE_EXPERTS, D_MODEL, MOE_FF), lambda i, tg, gr, st: (tg[i], 0, 0)),
                      pl.BlockSpec((MOE_EXPERTS, D_MODEL, MOE_FF), lambda i, tg, gr, st: (tg[i], 0, 0)),
                      pl.BlockSpec((1, eff, D_MODEL), lambda i, tg, gr, st: (tg[i], 0, 0)),
                      vec, vec],
            out_specs=hbm,
            scratch_shapes=[pltpu.VMEM((2, bm, D_MODEL), F32),
                            pltpu.VMEM((2, bm, D_MODEL), F32),
                            pltpu.SMEM(((n_tiles + 1) * bm,), jnp.int32),
                            pltpu.SMEM(((n_tiles + 1) * bm,), jnp.int32),
                            pltpu.SemaphoreType.DMA((2,)),
                            pltpu.SemaphoreType.DMA((2,))]),
        out_shape=jax.ShapeDtypeStruct((n + 2 * bm, D_MODEL), F32),
        compiler_params=_cparams("arbitrary"),
        name="moe_sorted",
    )(tile_group, grp, starts, x, wr2, w1g, w3g, w2g, g, b)


def _moe_dispatch(x, n, wr, w1g, w3g, w2g, g, b, bm):
    wr_hi = wr.astype(BF16)
    wr_lo = (wr - wr_hi.astype(F32)).astype(BF16)
    wr2 = jnp.concatenate([wr_hi, wr_lo], axis=1)
    rt = 8 * LANES
    gidx, cnt = _router(x, wr2, n, rt)
    grp = gidx.astype(jnp.int32).reshape(n)
    counts = jnp.sum(cnt.reshape(n // rt, rt // LANES, LANES)[:, 0, :MOE_GROUPS], axis=0).astype(jnp.int32)
    padded = (counts + bm - 1) // bm * bm
    ends = jnp.cumsum(padded)
    starts = ends - padded
    n_tiles = n // bm + MOE_GROUPS
    tile_start = jnp.arange(n_tiles, dtype=jnp.int32) * bm
    tile_group = jnp.minimum(jnp.sum(tile_start[:, None] >= ends[None, :], axis=1), MOE_GROUPS - 1)
    return _moe_sorted(tile_group.astype(jnp.int32), grp, starts.astype(jnp.int32), x, wr2,
                       w1g, w3g, w2g, g, b, bm)


def _hgrn_gates(f, lb):
    log_sig = jnp.minimum(f, 0.0) - jnp.log1p(jnp.exp(-jnp.abs(f)))
    a = jnp.log1p(-lb) + log_sig
    log_lb = jnp.log(lb)
    log_f = jnp.maximum(log_lb, a) + jnp.log1p(jnp.exp(-jnp.abs(log_lb - a)))
    k = (1.0 - lb) / (1.0 + jnp.exp(f))
    return log_f, k


def _hgrn_finish(o, gate, ng):
    o = o * lax.rsqrt(jnp.mean(o * o, axis=-1, keepdims=True) + RMS_EPS) * ng
    return o * gate


def _rows_of(x, idx, n):
    return jnp.concatenate([jnp.broadcast_to(x[i:i + 1, :], (n, x.shape[1])) for i in idx], axis=0)


def _hgrn_scan_body(q_ref, f_ref, v_ref, g_ref, lb_ref, ng_ref, tri_ref, lm_ref, bsel_ref, dm_ref,
                    o_ref, s_ref, st_ref, *, n_super):
    C, c, N = HGRN_CHUNK, HGRN_SUB, HGRN_SUPER
    neg = -1e30
    tb = pl.program_id(2)

    @pl.when(tb == 0)
    def _init():
        st_ref[...] = jnp.zeros_like(st_ref)

    lb = lb_ref[...]
    ng = ng_ref[...]
    row = lax.broadcasted_iota(jnp.int32, (N, 1), 0)
    srow = lax.broadcasted_iota(jnp.int32, (c, 1), 0)
    levels = [(C >> l, C >> (l + 1)) for l in range(lm_ref.shape[0])]

    def super_chunk(si, carry):
        r0 = pl.multiple_of(si * N, N)
        q = q_ref[pl.ds(r0, N), :]
        v = v_ref[pl.ds(r0, N), :].astype(BF16)
        gate = g_ref[pl.ds(r0, N), :]
        log_f, k = _hgrn_gates(f_ref[pl.ds(r0, N), :], lb)

        hi = log_f.astype(BF16)
        r1 = log_f - hi.astype(F32)
        mid = r1.astype(BF16)
        lo = (r1 - mid.astype(F32)).astype(BF16)
        cs = jnp.dot(tri_ref[...], jnp.concatenate([hi, mid, lo], axis=1), preferred_element_type=F32)
        b2 = (cs[:, :A_DK] + cs[:, A_DK:2 * A_DK] + cs[:, 2 * A_DK:]) * LOG2E

        att_t = None
        for li, (blk, half) in enumerate(levels):
            up = (row & (blk - 1)) >= half
            ref = _rows_of(b2, [blk * m + half - 1 for m in range(N // blk)], blk)
            e = jnp.exp2(jnp.where(up, b2 - ref, ref - b2))
            ql = jnp.where(up, q * e, 0.0).astype(BF16)
            kl = jnp.where(up, 0.0, k * e).astype(BF16)
            term = _nt_dot(kl, ql) * lm_ref[li]
            att_t = term if att_t is None else att_t + term

        ys = []
        for i in range(N // c):
            base = c * i
            kb = k[base:base + c, :]
            bb = b2[base:base + c, :]
            units = []
            for tl in range(c):
                d = jnp.where(srow <= tl, b2[base + tl:base + tl + 1, :] - bb, neg)
                units.append(q[base + tl:base + tl + 1, :] * kb * jnp.exp2(d))
            ys.append(jnp.concatenate(units, axis=1))
        y = jnp.concatenate(ys, axis=0).astype(BF16)
        r = jnp.dot(y, bsel_ref[...], preferred_element_type=F32)
        diag = jnp.concatenate([r * dm_ref[0], r * dm_ref[1]], axis=1)
        o_intra = _tn_dot((att_t + diag).astype(BF16), v)

        bl = _rows_of(b2, [C * m + C - 1 for m in range(N // C)], C)
        qe = (q * jnp.exp2(b2)).astype(BF16)
        kd = (k * jnp.exp2(bl - b2)).astype(BF16)
        st = st_ref[...]
        nc = N // C
        states, lhs = [], []
        zero = jnp.zeros((C, A_DK), BF16)
        for m in range(nc):
            rows = slice(C * m, C * m + C)
            states.append(st.astype(BF16))
            lhs.append(jnp.concatenate([qe[rows, :] if j == m else zero for j in range(nc)], axis=1))
            st = st * jnp.exp2(b2[C * m + C - 1:C * m + C, :]) + _tn_dot(v[rows, :], kd[rows, :])
        st_ref[...] = st
        o = o_intra + _nt_dot(jnp.concatenate(lhs, axis=0), jnp.concatenate(states, axis=1))
        o_ref[pl.ds(r0, N), :] = _hgrn_finish(o, gate, ng).astype(o_ref.dtype)
        return carry

    for si in range(n_super):
        super_chunk(si, 0)

    @pl.when(tb == pl.num_programs(2) - 1)
    def _emit():
        s_ref[0, 0] = st_ref[...].T


def _hgrn_scan(p, lb, ng, batch, seq, tb):
    nt = seq // tb
    nh = A_HEADS
    n, c = HGRN_SUPER, HGRN_SUB
    assert n == 2 * LANES and LANES % c == 0
    s_i = jnp.arange(n)[:, None]
    t_i = jnp.arange(n)[None, :]
    tri = ((s_i >= t_i) & (s_i // HGRN_CHUNK == t_i // HGRN_CHUNK)).astype(BF16)
    blocks = []
    blk = HGRN_CHUNK
    while blk > c:
        blocks.append(blk)
        blk //= 2
    lm = jnp.stack([(s_i // bk == t_i // bk) for bk in blocks]).astype(F32)
    lane = jnp.arange(LANES)[None, :]
    bsel = (jnp.arange(c * A_DK)[:, None] // A_DK == lane % c).astype(BF16)
    dm = jnp.stack([(s_i // c == g * (LANES // c) + lane // c) for g in range(2)]).astype(F32)

    def col(off):
        return pl.BlockSpec((tb, A_DK), lambda b, h, t: (b * nt + t, off * nh + h))

    def const(a):
        return pl.BlockSpec(a.shape, lambda b, h, t: (0,) * a.ndim)

    return pl.pallas_call(
        functools.partial(_hgrn_scan_body, n_super=tb // n),
        grid=(batch, nh, nt),
        in_specs=[col(0), col(1), col(2), col(3),
                  pl.BlockSpec((1, A_DK), lambda b, h, t: (0, h)),
                  pl.BlockSpec((1, A_DV), lambda b, h, t: (0, 0)),
                  const(tri), const(lm), const(bsel), const(dm)],
        out_specs=[pl.BlockSpec((tb, A_DV), lambda b, h, t: (b * nt + t, h)),
                   pl.BlockSpec((1, 1, A_DK, A_DV), lambda b, h, t: (b, h, 0, 0))],
        out_shape=[jax.ShapeDtypeStruct((batch * seq, nh * A_DV), BF16),
                   jax.ShapeDtypeStruct((batch, nh, A_DK, A_DV), F32)],
        scratch_shapes=[pltpu.VMEM((A_DV, A_DK), F32)],
        compiler_params=_cparams("parallel", "parallel", "arbitrary"),
        name="hgrn_scan",
    )(p, p, p, p, lb, ng, tri, lm, bsel, dm)


def _hgrn_step_body(p_ref, s0_ref, lb_ref, ng_ref, o_ref, s_ref, *, n_tok):
    R = p_ref.shape[0]
    row = lax.broadcasted_iota(jnp.int32, (R, 1), 0)
    valid = row < n_tok
    r2 = lax.broadcasted_iota(jnp.int32, (R, R), 0)
    c2 = lax.broadcasted_iota(jnp.int32, (R, R), 1)
    tril = (r2 >= c2).astype(F32)
    ng = ng_ref[...]
    for h in range(A_HEADS):
        sl = slice(h * A_DK, (h + 1) * A_DK)
        q = p_ref[:, sl]
        v = p_ref[:, 2 * A_WIDTH + h * A_DV:2 * A_WIDTH + (h + 1) * A_DV]
        gate = p_ref[:, 3 * A_WIDTH + h * A_DV:3 * A_WIDTH + (h + 1) * A_DV]
        log_f, k = _hgrn_gates(p_ref[:, A_WIDTH + h * A_DK:A_WIDTH + (h + 1) * A_DK], lb_ref[:, sl])
        b = jnp.dot(tril, log_f, precision=HIGHEST, preferred_element_type=F32)
        st = s0_ref[0, h].T
        o = _nt_dot((q * jnp.exp(b)).astype(BF16), st.astype(BF16))
        for s in range(n_tok):
            m = row >= s
            w = jnp.where(m, q * k[s:s + 1, :] * jnp.exp(jnp.where(m, b - b[s:s + 1, :], 0.0)), 0.0)
            o = o + jnp.sum(w, axis=-1, keepdims=True) * v[s:s + 1, :]
        o_ref[:, h * A_DV:(h + 1) * A_DV] = _hgrn_finish(o, gate, ng)
        bl = b[n_tok - 1:n_tok, :]
        kd = jnp.where(valid, k * jnp.exp(jnp.where(valid, bl - b, 0.0)), 0.0)
        st_new = st * jnp.exp(bl) + _tn_dot(v.astype(BF16), kd.astype(BF16))
        s_ref[0, h] = st_new.T


def _hgrn_step(p, s0, lb, ng, batch, n_tok):
    rows = p.shape[0] // batch
    return pl.pallas_call(
        functools.partial(_hgrn_step_body, n_tok=n_tok),
        grid=(batch,),
        in_specs=[pl.BlockSpec((rows, 4 * A_WIDTH), lambda b: (b, 0)),
                  pl.BlockSpec((1, A_HEADS, A_DK, A_DV), lambda b: (b, 0, 0, 0)),
                  pl.BlockSpec((1, A_WIDTH), lambda b: (0, 0)),
                  pl.BlockSpec((1, A_DV), lambda b: (0, 0))],
        out_specs=[pl.BlockSpec((rows, A_HEADS * A_DV), lambda b: (b, 0)),
                   pl.BlockSpec((1, A_HEADS, A_DK, A_DV), lambda b: (b, 0, 0, 0))],
        out_shape=[jax.ShapeDtypeStruct((batch * rows, A_HEADS * A_DV), F32),
                   jax.ShapeDtypeStruct((batch, A_HEADS, A_DK, A_DV), F32)],
        compiler_params=_cparams("parallel"),
        name="hgrn_step",
    )(p, s0, lb, ng)


def _gmlp_prologue(u_ref, v_ref, ws_ref, bs_ref, gated_ref, *, chunk):
    bm = u_ref.shape[0]
    r2 = lax.broadcasted_iota(jnp.int32, (chunk, chunk), 0)
    c2 = lax.broadcasted_iota(jnp.int32, (chunk, chunk), 1)
    causal = r2 >= c2
    for h in range(B_HEADS):
        wc = jnp.where(causal, ws_ref[h], 0.0).astype(BF16)
        bias = bs_ref[:, h:h + 1]
        cols = slice(h * B_HD, (h + 1) * B_HD)
        for n in range(bm // chunk):
            rows = slice(n * chunk, (n + 1) * chunk)
            mixed = jnp.dot(wc, v_ref[rows, cols].astype(BF16), preferred_element_type=F32) + bias
            gated_ref[rows, cols] = (u_ref[rows, cols] * mixed).astype(BF16)
    return gated_ref[...]


def _gmlp_out_body(u_ref, v_ref, ws_ref, bs_ref, w_ref, r_ref, g_ref, b_ref, o_ref, gated_ref, *, chunk):
    a = _gmlp_prologue(u_ref, v_ref, ws_ref, bs_ref, gated_ref, chunk=chunk)
    acc = jnp.dot(a, w_ref[...], preferred_element_type=F32)
    y = ALPHA * r_ref[...] + acc
    o_ref[...] = _ln_rows(y, g_ref[...], b_ref[...])


def _gmlp_out(uv, ws, bs_t, w, resid, g, b, bm, chunk):
    m = uv.shape[0]
    row = pl.BlockSpec((bm, D_MODEL), lambda i: (i, 0))
    vec = pl.BlockSpec((1, D_MODEL), lambda i: (0, 0))
    return pl.pallas_call(
        functools.partial(_gmlp_out_body, chunk=chunk),
        grid=(m // bm,),
        in_specs=[pl.BlockSpec((bm, D_MODEL), lambda i: (i, 0)),
                  pl.BlockSpec((bm, D_MODEL), lambda i: (i, 1)),
                  pl.BlockSpec(ws.shape, lambda i: (0, 0, 0)),
                  pl.BlockSpec(bs_t.shape, lambda i: (0, 0)),
                  pl.BlockSpec((D_MODEL, D_MODEL), lambda i: (0, 0)),
                  row, vec, vec],
        out_specs=row,
        out_shape=jax.ShapeDtypeStruct((m, D_MODEL), F32),
        scratch_shapes=[pltpu.VMEM((bm, D_MODEL), BF16)],
        compiler_params=_cparams("parallel"),
        name="gmlp_out",
    )(uv, uv, ws, bs_t, w, resid, g, b)


def _band_attn_body(q_ref, k_ref, v_ref, o_ref, l_ref, *, dil, seq):
    bq = C_QBLOCK
    span = C_KEYS - 1
    scale = C_HD ** -0.5
    n_blocks = seq // dil // bq
    qi = lax.broadcasted_iota(jnp.int32, (bq, 2 * bq), 0)
    ki = lax.broadcasted_iota(jnp.int32, (bq, 2 * bq), 1)

    def rows(first, n):
        return pl.ds(first, n) if dil == 1 else pl.ds(first, n, stride=dil)

    for r in range(dil):
        for i in range(n_blocks):
            w = max(i - 1, 0)
            qs = rows(r + dil * bq * i, bq)
            ws = rows(r + dil * bq * w, 2 * bq)
            q = q_ref[qs, :].astype(BF16)
            kw = k_ref[ws, :].astype(BF16)
            vw = v_ref[ws, :].astype(BF16)
            s = _nt_dot(q, kw) * scale
            rel = bq * (i - w) + qi - ki
            s = jnp.where((rel >= 0) & (rel <= span), s, -jnp.inf)
            mx = jnp.max(s, axis=-1, keepdims=True)
            p = jnp.exp(s - mx)
            den = jnp.sum(p, axis=-1, keepdims=True)
            o_ref[qs, :] = jnp.dot(p.astype(BF16), vw, preferred_element_type=F32) / den
            l_ref[qs, :] = jnp.broadcast_to(mx + jnp.log(den), (bq, C_HD))


def _band_attn(qkv, g, dil, batch, seq):
    ng = len(C_GROUPS)

    def col(part):
        return pl.BlockSpec((seq, C_HD), lambda b, h: (b, (part * ng + g) * C_HEADS + h))

    out = pl.BlockSpec((seq, C_HD), lambda b, h: (b, h))
    return pl.pallas_call(
        functools.partial(_band_attn_body, dil=dil, seq=seq),
        grid=(batch, C_HEADS),
        in_specs=[col(0), col(1), col(2)],
        out_specs=[out, out],
        out_shape=[jax.ShapeDtypeStruct((batch * seq, C_HEADS * C_HD), F32)] * 2,
        compiler_params=_cparams("parallel", "parallel"),
        name=f"band_attn_d{dil}",
    )(qkv, qkv, qkv)


def _step_attn_body(qkv_ref, *refs, g, dil, n_tok):
    past_refs, (o_ref, l_ref) = refs[:-2], refs[-2:]
    scale = C_HD ** -0.5
    ng = len(C_GROUPS)
    hs = C_HEADS
    tok = lax.broadcasted_iota(jnp.int32, (n_tok, 1, 1), 0)
    prow = lax.broadcasted_iota(jnp.int32, (past_refs[0].shape[1], 1, 1), 0)
    kn = qkv_ref[0, :, (ng + g) * hs:(ng + g + 1) * hs, :]
    vn = qkv_ref[0, :, (2 * ng + g) * hs:(2 * ng + g + 1) * hs, :]
    for t in range(n_tok):
        past_ref = past_refs[t % dil]
        new_ok = (tok <= t) & (((t - tok) % dil) == 0)
        past_ok = prow >= (t if dil == 1 else 0)
        q = qkv_ref[0, t, g * hs:(g + 1) * hs, :][None]
        kp = past_ref[0, :, 0:hs, :]
        vp = past_ref[0, :, hs:2 * hs, :]
        sp = jnp.where(past_ok, jnp.sum(kp * q, axis=-1, keepdims=True) * scale, -jnp.inf)
        sn = jnp.where(new_ok, jnp.sum(kn * q, axis=-1, keepdims=True) * scale, -jnp.inf)
        mx = jnp.maximum(jnp.max(sp, axis=0, keepdims=True), jnp.max(sn, axis=0, keepdims=True))
        pp = jnp.exp(sp - mx)
        pn = jnp.exp(sn - mx)
        den = jnp.sum(pp, axis=0, keepdims=True) + jnp.sum(pn, axis=0, keepdims=True)
        o = (jnp.sum(pp * vp, axis=0, keepdims=True) + jnp.sum(pn * vn, axis=0, keepdims=True)) / den
        o_ref[0, t] = o[0]
        l_ref[0, t] = jnp.broadcast_to(mx + jnp.log(den), (1, hs, C_HD))[0]


def _step_attn(qkv, cache, g, window, dil, batch, n_tok):
    past = cache.reshape(batch, window // dil, dil, 2 * C_HEADS, C_HD)
    n_res = min(dil, n_tok)
    out = pl.BlockSpec((1, n_tok, C_HEADS, C_HD), lambda b: (b, 0, 0, 0))
    past_specs = [pl.BlockSpec((1, window // dil, None, 2 * C_HEADS, C_HD),
                               functools.partial(lambda b, r: (b, 0, r, 0, 0), r=r)) for r in range(n_res)]
    return pl.pallas_call(
        functools.partial(_step_attn_body, g=g, dil=dil, n_tok=n_tok),
        grid=(batch,),
        in_specs=[pl.BlockSpec((1, n_tok) + qkv.shape[2:], lambda b: (b, 0, 0, 0))] + past_specs,
        out_specs=[out, out],
        out_shape=[jax.ShapeDtypeStruct((batch, n_tok, C_HEADS, C_HD), F32)] * 2,
        compiler_params=_cparams("parallel"),
        name=f"step_attn_d{dil}",
    )(qkv, *([past] * n_res))


def _rope_tables(pos):
    half = C_HD // 2
    inv = ROPE_THETA ** (-jnp.arange(half, dtype=F32) / half)
    ang = pos.astype(F32)[:, None] * inv[None, :]
    cos, sin = jnp.cos(ang), jnp.sin(ang)
    return jnp.concatenate([cos, cos], -1), jnp.concatenate([-sin, sin], -1)


def _row_tile(m, cap):
    return min(m, cap)


def _hgrn_layer(x, batch, seq, s0, w_in, lb, ng, w_out, ln_g, ln_b):
    m = batch * seq
    bn = 1024
    p = _proj(x, w_in, (), (), functools.partial(_hgrn_proj_epilogue, bn=bn), m, _row_tile(m, PROJ_ROWS), bn,
              PROJ_CHUNK, "hgrn_proj")
    if s0 is None:
        o, s_new = _hgrn_scan(p, lb, ng, batch, seq, HGRN_ROWS if seq % HGRN_ROWS == 0 else HGRN_SUPER)
    else:
        rows = 8
        pp = jnp.pad(p.reshape(batch, seq, -1), ((0, 0), (0, rows - seq), (0, 0))).reshape(batch * rows, -1)
        o, s_new = _hgrn_step(pp, s0, lb, ng, batch, seq)
        o = o.reshape(batch, rows, -1)[:, :seq].reshape(m, -1)
    bm = _row_tile(m, 512)
    x = _out_ln((o,), (pl.BlockSpec((bm, o.shape[1]), lambda i: (i, 0)),), _cast_prologue,
                w_out, x, ln_g, ln_b, bm, "hgrn_out")
    return x, s_new


def _gmlp_layer(x, batch, seq, w_in, b_in, g1, b1, ws, bs, w_out, ln_g, ln_b):
    m = batch * seq
    bm = _row_tile(m, 512)
    vec = pl.BlockSpec((1, D_MODEL), lambda i, j: (0, 0))
    uv = _proj(x, w_in, (b_in, g1, b1),
               (pl.BlockSpec((1, D_MODEL), lambda i, j: (0, j)), vec, vec),
               _gmlp_proj_epilogue, m, bm, D_MODEL, D_MODEL, "gmlp_proj")
    if seq % B_CHUNK == 0:
        chunk, ws_c, bs_t = B_CHUNK, ws, bs.T
    else:
        chunk = m
        eye = jnp.eye(batch, dtype=ws.dtype)
        ws_c = jnp.einsum("ab,hts->hatbs", eye, ws[:, :seq, :seq]).reshape(B_HEADS, m, m)
        bs_t = jnp.tile(bs[:, :seq].T, (batch, 1))
    x = _gmlp_out(uv, ws_c, bs_t, w_out, x, ln_g, ln_b, _row_tile(m, 256), chunk)
    return x, uv


def _attn_layer(x, batch, seq, caches, pos0, w_in, w_out, ln_g, ln_b):
    m = batch * seq
    bn = 1024
    bm = _row_tile(m, PROJ_ROWS)
    cos, sin = _rope_tables(pos0 + jnp.arange(seq, dtype=jnp.int32))
    cos, sin = jnp.tile(cos, (batch, 1)), jnp.tile(sin, (batch, 1))
    tab = pl.BlockSpec((bm, C_HD), lambda i, j: (i, 0))
    qkv = _proj(x, w_in, (cos, sin), (tab, tab), functools.partial(_attn_proj_epilogue, bn=bn), m, bm, bn,
                PROJ_CHUNK, "attn_proj")
    qkv3 = qkv.reshape(batch, seq, C_QKV)
    outs, lses = [], []
    for g, (window, dil) in enumerate(C_GROUPS):
        if caches is None:
            o, lse = _band_attn(qkv, g, dil, batch, seq)
        else:
            o, lse = _step_attn(qkv.reshape(batch, seq, C_QKV // C_HD, C_HD), caches[g], g, window, dil,
                                batch, seq)
            o, lse = o.reshape(m, -1), lse.reshape(m, -1)
        outs.append(o)
        lses.append(lse)
    bm2 = _row_tile(m, 256)
    spec = pl.BlockSpec((bm2, C_HEADS * C_HD), lambda i: (i, 0))
    x = _out_ln(tuple(outs) + tuple(lses), (spec,) * 6, _merge_prologue, w_out, x, ln_g, ln_b, bm2,
                "attn_out")
    hw = C_HEADS * C_HD
    ng = len(C_GROUPS)
    kv = []
    for g, (window, _) in enumerate(C_GROUPS):
        tail = qkv3[:, seq - min(window, seq):]
        k = tail[:, :, (ng + g) * hw:(ng + g + 1) * hw].reshape(batch, -1, C_HEADS, C_HD)
        v = tail[:, :, (2 * ng + g) * hw:(2 * ng + g + 1) * hw].reshape(batch, -1, C_HEADS, C_HD)
        kv.append(jnp.stack([k, v], axis=2))
    return x, kv


def kernel(x_prompt, x_sample, state_hgrn, cache_c_kv_w128, cache_c_kv_w512, cache_c_kv_w2048, ln_g, ln_b, a_w_in, a_lb_logits, a_norm_g, a_w_out, b_w_in, b_b_in, b_ln_g, b_ln_b, b_w_s, b_b_s, b_w_out, c_w_in, c_w_out, moe_w_group, moe_w_expert, moe_w1, moe_w3, moe_w2):
    bp, tp, _ = x_prompt.shape
    bs, ts, _ = x_sample.shape
    assert tp % HGRN_SUPER == 0 and tp % B_CHUNK == 0 and tp // C_GROUPS[-1][1] >= 2 * C_QBLOCK
    assert ts <= 8 and ts <= C_GROUPS[1][1] and (bp * tp) % MOE_TILE == 0

    lb_p = jax.nn.softmax(a_lb_logits.astype(F32), axis=0)
    lb_all = jnp.clip(jnp.cumsum(lb_p, axis=0) - lb_p[0:1], 0.0, 1.0 - 1e-6)
    caches = (cache_c_kv_w128, cache_c_kv_w512, cache_c_kv_w2048)

    xp = x_prompt.reshape(bp * tp, D_MODEL)
    xs = x_sample.reshape(bs * ts, D_MODEL)
    hgrn_p, hgrn_s, chunk_v_s = [], [], []
    kv_p = [[] for _ in C_GROUPS]
    kv_s = [[] for _ in C_GROUPS]

    for i in range(DEPTH):
        kind, j = i % 3, i // 3
        g0, b0 = ln_g[i, 0][None], ln_b[i, 0][None]
        if kind == 0:
            w_in, w_out = a_w_in[j].astype(BF16), a_w_out[j].astype(BF16)
            lb, ng = lb_all[j][None], a_norm_g[j][None]
            xp, sp = _hgrn_layer(xp, bp, tp, None, w_in, lb, ng, w_out, g0, b0)
            xs, ss = _hgrn_layer(xs, bs, ts, state_hgrn[j].astype(F32), w_in, lb, ng, w_out, g0, b0)
            hgrn_p.append(sp)
            hgrn_s.append(ss)
        elif kind == 1:
            w_in, w_out = b_w_in[j].astype(BF16), b_w_out[j].astype(BF16)
            args = (w_in, b_b_in[j][None], b_ln_g[j][None], b_ln_b[j][None], b_w_s[j], b_b_s[j], w_out, g0, b0)
            xp, _ = _gmlp_layer(xp, bp, tp, *args)
            xs, uvs = _gmlp_layer(xs, bs, ts, *args)
            chunk_v_s.append(uvs[:, D_MODEL:].reshape(bs, ts, D_MODEL))
        else:
            w_in, w_out = c_w_in[j].astype(BF16), c_w_out[j].astype(BF16)
            xp, kvp = _attn_layer(xp, bp, tp, None, 0, w_in, w_out, g0, b0)
            xs, kvs = _attn_layer(xs, bs, ts, tuple(c[j] for c in caches), PAST_LEN, w_in, w_out, g0, b0)
            for g in range(len(C_GROUPS)):
                kv_p[g].append(kvp[g])
                kv_s[g].append(kvs[g])
        wr = jnp.pad(jnp.concatenate([moe_w_expert[i], moe_w_group[i]], axis=1),
                     ((0, 0), (0, LANES - MOE_GE - MOE_GROUPS)))
        w1g, w3g = moe_w1[i].astype(BF16), moe_w3[i].astype(BF16)
        w2g = moe_w2[i].astype(BF16).reshape(MOE_GROUPS, MOE_EXPERTS * MOE_FF, D_MODEL)
        g1, b1 = ln_g[i, 1][None], ln_b[i, 1][None]
        xp = _moe_dispatch(xp, bp * tp, wr, w1g, w3g, w2g, g1, b1, MOE_TILE)
        xs = _moe(xs, wr, w1g, w3g, w2g, g1, b1, bs * ts)

    return (xp[:bp * tp].reshape(bp, tp, D_MODEL), xs.reshape(bs, ts, D_MODEL),
            jnp.stack(hgrn_p), jnp.stack(hgrn_s), jnp.stack(chunk_v_s),
            jnp.stack(kv_p[0]), jnp.stack(kv_s[0]), jnp.stack(kv_p[1]), jnp.stack(kv_s[1]),
            jnp.stack(kv_p[2]), jnp.stack(kv_s[2]))
```

```python
import functools

import jax
import jax.numpy as jnp
from jax import lax
from jax.experimental import pallas as pl
from jax.experimental.pallas import tpu as pltpu

F32 = jnp.float32
BF16 = jnp.bfloat16
HIGHEST = lax.Precision.HIGHEST

D_MODEL = 2048
DEPTH = 4
PAST_LEN = 16384
A_HEADS = 16
A_DK = 128
A_DV = 128
A_WIDTH = A_HEADS * A_DK
HGRN_CHUNK = 64
HGRN_SUB = 8
LOG2E = 1.4426950408889634
HGRN_SUPER = 256
HGRN_ROWS = 2048
B_CHUNK = 128
B_HEADS = 16
B_HD = 128
C_HEADS = 8
C_HD = 128
C_GROUPS = ((128, 1), (512, 4), (2048, 16))
C_KEYS = 129
C_QBLOCK = 128
C_QKV = 3 * len(C_GROUPS) * C_HEADS * C_HD
ROPE_THETA = 10000.0
MOE_GROUPS = 4
MOE_EXPERTS = 4
MOE_GE = MOE_GROUPS * MOE_EXPERTS
MOE_FF = 256
MOE_TILE = 256
LN_EPS = 1e-5
RMS_EPS = 1e-6
ALPHA = (2 * DEPTH) ** 0.25
LANES = 128
VMEM_LIMIT = 56 * 1024 * 1024
PROJ_ROWS = 1024
PROJ_CHUNK = 256


def _cparams(*sem):
    return pltpu.CompilerParams(dimension_semantics=sem, vmem_limit_bytes=VMEM_LIMIT)


def _sigmoid(x):
    return 1.0 / (1.0 + jnp.exp(-x))


def _ln_rows(y, g, b):
    mu = jnp.mean(y, axis=-1, keepdims=True)
    d = y - mu
    var = jnp.mean(d * d, axis=-1, keepdims=True)
    return d * lax.rsqrt(var + LN_EPS) * g + b


def _nt_dot(a, b):
    return lax.dot_general(a, b, (((1,), (1,)), ((), ())), preferred_element_type=F32)


def _tn_dot(a, b):
    return lax.dot_general(a, b, (((0,), (0,)), ((), ())), preferred_element_type=F32)


def _proj_body(x_ref, w_ref, *rest, epilogue, n_extra, chunk):
    extras = rest[:n_extra]
    o_ref = rest[n_extra]
    xb_ref = rest[n_extra + 1]
    j = pl.program_id(1)

    @pl.when(j == 0)
    def _cast():
        xb_ref[...] = x_ref[...].astype(BF16)

    bn = w_ref.shape[1]
    for c in range(bn // chunk):
        cols = slice(c * chunk, (c + 1) * chunk)
        acc = jnp.dot(xb_ref[...], w_ref[:, cols], preferred_element_type=F32)
        epilogue(acc, j, extras, o_ref, cols)


def _proj(x, w, extras, extra_specs, epilogue, m, bm, bn, chunk, name):
    k = x.shape[1]
    n = w.shape[1]
    return pl.pallas_call(
        functools.partial(_proj_body, epilogue=epilogue, n_extra=len(extras), chunk=chunk),
        grid=(m // bm, n // bn),
        in_specs=[pl.BlockSpec((bm, k), lambda i, j: (i, 0)),
                  pl.BlockSpec((k, bn), lambda i, j: (0, j))] + list(extra_specs),
        out_specs=pl.BlockSpec((bm, bn), lambda i, j: (i, j)),
        out_shape=jax.ShapeDtypeStruct((m, n), F32),
        scratch_shapes=[pltpu.VMEM((bm, k), BF16)],
        compiler_params=_cparams("parallel", "arbitrary"),
        name=name,
    )(x, w, *extras)


def _hgrn_proj_epilogue(acc, j, extras, o_ref, cols, *, bn):
    nq = A_WIDTH // bn
    is_silu = jnp.logical_or(j < nq, j >= 3 * nq)
    o_ref[:, cols] = acc * jnp.where(is_silu, _sigmoid(acc), 1.0)


def _gelu_tanh(z):
    return 0.5 * z * (1.0 + jnp.tanh(0.7978845608028654 * (z + 0.044715 * (z * z * z))))


def _gmlp_proj_epilogue(acc, j, extras, o_ref, cols):
    bias_ref, g_ref, b_ref = extras
    z = _gelu_tanh(acc + bias_ref[...])

    @pl.when(j == 0)
    def _():
        o_ref[...] = z

    @pl.when(j == 1)
    def _():
        o_ref[...] = _ln_rows(z, g_ref[...], b_ref[...])


def _attn_proj_epilogue(acc, j, extras, o_ref, cols, *, bn):
    cos_ref, sin_ref = extras
    is_rot = j < 2 * len(C_GROUPS) * C_HEADS * C_HD // bn
    cos = cos_ref[...]
    sin = sin_ref[...]
    for h in range(acc.shape[1] // C_HD):
        xh = acc[:, h * C_HD:(h + 1) * C_HD]
        rot = xh * cos + pltpu.roll(xh, C_HD // 2, 1) * sin
        o_ref[:, cols.start + h * C_HD:cols.start + (h + 1) * C_HD] = jnp.where(is_rot, rot, xh)


def _out_ln_body(*refs, prologue, n_in):
    ins = refs[:n_in]
    w_ref, r_ref, g_ref, b_ref, o_ref = refs[n_in:n_in + 5]
    a = prologue(*ins)
    acc = jnp.dot(a, w_ref[...], preferred_element_type=F32)
    y = ALPHA * r_ref[...] + acc
    o_ref[...] = _ln_rows(y, g_ref[...], b_ref[...])


def _out_ln(ins, in_specs, prologue, w, resid, g, b, bm, name):
    m = ins[0].shape[0]
    k = w.shape[0]
    row = pl.BlockSpec((bm, D_MODEL), lambda i: (i, 0))
    vec = pl.BlockSpec((1, D_MODEL), lambda i: (0, 0))
    return pl.pallas_call(
        functools.partial(_out_ln_body, prologue=prologue, n_in=len(ins)),
        grid=(m // bm,),
        in_specs=list(in_specs) + [pl.BlockSpec((k, D_MODEL), lambda i: (0, 0)), row, vec, vec],
        out_specs=row,
        out_shape=jax.ShapeDtypeStruct((m, D_MODEL), F32),
        compiler_params=_cparams("parallel"),
        name=name,
    )(*ins, w, resid, g, b)


def _cast_prologue(a_ref):
    return a_ref[...].astype(BF16)


def _merge_prologue(o0, o1, o2, l0, l1, l2):
    a0, a1, a2 = l0[...], l1[...], l2[...]
    mx = jnp.maximum(jnp.maximum(a0, a1), a2)
    e0, e1, e2 = jnp.exp(a0 - mx), jnp.exp(a1 - mx), jnp.exp(a2 - mx)
    o = (e0 * o0[...] + e1 * o1[...] + e2 * o2[...]) / (e0 + e1 + e2)
    return o.astype(BF16)


def _moe_gate(logits, group=None):
    lane = lax.broadcasted_iota(jnp.int32, logits.shape, 1).astype(F32)
    neg = -jnp.inf
    big = 4.0 * LANES
    gl = jnp.where((lane >= MOE_GE) & (lane < MOE_GE + MOE_GROUPS), logits, neg)
    gmax = jnp.max(gl, axis=-1, keepdims=True)
    den = jnp.sum(jnp.exp(gl - gmax), axis=-1, keepdims=True)
    if group is None:
        g_idx = jnp.min(jnp.where(gl == gmax, lane - MOE_GE, big), axis=-1, keepdims=True)
        g_top = 1.0 / den
    else:
        g_idx = group
        g_sel = jnp.sum(jnp.where(lane == MOE_GE + group, logits, 0.0), axis=-1, keepdims=True)
        g_top = jnp.exp(g_sel - gmax) / den
    lo = g_idx * MOE_EXPERTS
    el = jnp.where((lane >= lo) & (lane < lo + MOE_EXPERTS), logits, neg)
    m1 = jnp.max(el, axis=-1, keepdims=True)
    i1 = jnp.min(jnp.where(el == m1, lane, big), axis=-1, keepdims=True)
    el2 = jnp.where(lane == i1, neg, el)
    m2 = jnp.max(el2, axis=-1, keepdims=True)
    i2 = jnp.min(jnp.where(el2 == m2, lane, big), axis=-1, keepdims=True)
    r = jnp.exp(m2 - m1)
    w1 = g_top / (1.0 + r)
    w2 = w1 * r
    return jnp.where(lane == i1, w1, 0.0) + jnp.where(lane == i2, w2, 0.0), g_idx


def _gate_columns(gate, first):
    lane = lax.broadcasted_iota(jnp.int32, gate.shape, 1)
    return jnp.concatenate(
        [jnp.broadcast_to(jnp.sum(jnp.where(lane == first + e, gate, 0.0), axis=-1, keepdims=True),
                          (gate.shape[0], MOE_FF)) for e in range(MOE_EXPERTS)], axis=1)


def _group_ffn(xb, gate, first, w1_ref, w3_ref, w2_ref):
    h1 = jnp.concatenate([jnp.dot(xb, w1_ref[e], preferred_element_type=F32) for e in range(MOE_EXPERTS)], axis=1)
    h3 = jnp.concatenate([jnp.dot(xb, w3_ref[e], preferred_element_type=F32) for e in range(MOE_EXPERTS)], axis=1)
    hg = (h1 * _sigmoid(h1) * h3 * _gate_columns(gate, first)).astype(BF16)
    return jnp.dot(hg, w2_ref[0], preferred_element_type=F32)


def _moe_body(x_ref, wr_ref, w1_ref, w3_ref, w2_ref, g_ref, b_ref, o_ref, xb_ref, gate_ref, acc_ref):
    gi = pl.program_id(1)

    @pl.when(gi == 0)
    def _route():
        x = x_ref[...]
        xb_ref[...] = x.astype(BF16)
        logits = jnp.dot(x, wr_ref[...], precision=HIGHEST, preferred_element_type=F32)
        gate_ref[...] = _moe_gate(logits)[0]
        acc_ref[...] = jnp.zeros_like(acc_ref)

    acc_ref[...] += _group_ffn(xb_ref[...], gate_ref[...], gi * MOE_EXPERTS, w1_ref, w3_ref, w2_ref)

    @pl.when(gi == MOE_GROUPS - 1)
    def _finish():
        y = ALPHA * x_ref[...] + acc_ref[...]
        o_ref[...] = _ln_rows(y, g_ref[...], b_ref[...])


def _moe(x, wr, w1g, w3g, w2g, g, b, bm):
    m = x.shape[0]
    eff = MOE_EXPERTS * MOE_FF
    row = pl.BlockSpec((bm, D_MODEL), lambda i, e: (i, 0))
    vec = pl.BlockSpec((1, D_MODEL), lambda i, e: (0, 0))
    return pl.pallas_call(
        _moe_body,
        grid=(m // bm, MOE_GROUPS),
        in_specs=[row,
                  pl.BlockSpec((D_MODEL, LANES), lambda i, e: (0, 0)),
                  pl.BlockSpec((MOE_EXPERTS, D_MODEL, MOE_FF), lambda i, e: (e, 0, 0)),
                  pl.BlockSpec((MOE_EXPERTS, D_MODEL, MOE_FF), lambda i, e: (e, 0, 0)),
                  pl.BlockSpec((1, eff, D_MODEL), lambda i, e: (e, 0, 0)),
                  vec, vec],
        out_specs=row,
        out_shape=jax.ShapeDtypeStruct((m, D_MODEL), F32),
        scratch_shapes=[pltpu.VMEM((bm, D_MODEL), BF16),
                        pltpu.VMEM((bm, LANES), F32),
                        pltpu.VMEM((bm, D_MODEL), F32)],
        compiler_params=_cparams("parallel", "arbitrary"),
        name="moe",
    )(x, wr, w1g, w3g, w2g, g, b)


def _router_logits(x, xh, wr_ref):
    xl = (x - xh.astype(F32)).astype(BF16)
    a = jnp.dot(xh, wr_ref[...], preferred_element_type=F32)
    return a[:, :LANES] + a[:, LANES:] + jnp.dot(xl, wr_ref[:, :LANES], preferred_element_type=F32)


def _router_body(x_ref, wr_ref, gidx_ref, cnt_ref):
    x = x_ref[...]
    _, g_idx = _moe_gate(_router_logits(x, x.astype(BF16), wr_ref))
    bm = x.shape[0]
    t = lax.broadcasted_iota(jnp.int32, (bm, LANES), 0)
    lane = lax.broadcasted_iota(jnp.int32, (bm, LANES), 1)
    spread = jnp.where((t & (LANES - 1)) == lane, g_idx, 0.0).astype(BF16)
    r_i = lax.broadcasted_iota(jnp.int32, (bm // LANES, bm), 0)
    t_i = lax.broadcasted_iota(jnp.int32, (bm // LANES, bm), 1)
    sel = (jnp.right_shift(t_i, LANES.bit_length() - 1) == r_i).astype(BF16)
    gidx_ref[...] = jnp.dot(sel, spread, preferred_element_type=F32)
    onehot = (lane.astype(F32) == g_idx).astype(F32)
    cnt_ref[...] = jnp.broadcast_to(jnp.sum(onehot, axis=0, keepdims=True), cnt_ref.shape)


def _router(x, wr, m, bm):
    assert bm % (8 * LANES) == 0 and m % bm == 0
    rows = bm // LANES
    small = pl.BlockSpec((rows, LANES), lambda i: (i, 0))
    return pl.pallas_call(
        _router_body,
        grid=(m // bm,),
        in_specs=[pl.BlockSpec((bm, D_MODEL), lambda i: (i, 0)),
                  pl.BlockSpec((D_MODEL, 2 * LANES), lambda i: (0, 0))],
        out_specs=[small, small],
        out_shape=[jax.ShapeDtypeStruct((m // LANES, LANES), F32),
                   jax.ShapeDtypeStruct((m // LANES, LANES), F32)],
        compiler_params=_cparams("parallel"),
        name="moe_router",
    )(x, wr)


def _moe_sorted_body(tg_ref, grp_ref, lay_ref, x_hbm, wr_ref, w1_ref, w3_ref, w2_ref, g_ref, b_ref, o_hbm,
                     xbuf, obuf, src_ref, dst_ref, pos_ref, xsem, osem, *, bm, n):
    i = pl.program_id(0)
    last = pl.num_programs(0) - 1
    slot = i % 2
    shift = bm.bit_length() - 1

    def plan():
        ng = MOE_GROUPS
        total = src_ref.shape[0] - bm

        def fill(t, carry):
            g = grp_ref[t]
            d = pos_ref[g]
            pos_ref[g] = d + 1
            src_ref[d] = t
            dst_ref[bm + d] = t
            return carry

        def pad(s, carry):
            src_ref[s] = 0
            dst_ref[bm + s] = n + ((s >> shift) & 1) * bm + (s & (bm - 1))
            return carry

        def lead(u, carry):
            dst_ref[u] = n + bm + u
            src_ref[total + u] = 0
            return carry

        for g in range(ng):
            pos_ref[g] = lay_ref[g]
        lax.fori_loop(0, n, fill, 0, unroll=8)
        for g in range(ng):
            lax.fori_loop(lay_ref[ng + g], lay_ref[2 * ng + g], pad, 0)
        lax.fori_loop(lay_ref[3 * ng - 1], total, pad, 0)
        lax.fori_loop(0, bm, lead, 0, unroll=8)

    def gather(tile, s):
        for j in range(bm):
            r = src_ref[tile * bm + j]
            pltpu.make_async_copy(x_hbm.at[pl.ds(r, 1), :], xbuf.at[s, pl.ds(j, 1), :], xsem.at[s]).start()

    def wait_gather(s):
        pltpu.make_async_copy(x_hbm.at[pl.ds(0, bm), :], xbuf.at[s], xsem.at[s]).wait()

    def scatter(tile, s):
        for j in range(bm):
            r = dst_ref[(tile + 1) * bm + j]
            pltpu.make_async_copy(obuf.at[s, pl.ds(j, 1), :], o_hbm.at[pl.ds(r, 1), :], osem.at[s]).start()

    def wait_scatter(s):
        pltpu.make_async_copy(obuf.at[s], o_hbm.at[pl.ds(0, bm), :], osem.at[s]).wait()

    @pl.when(i == 0)
    def _first():
        plan()
        gather(0, 0)
        obuf[1] = jnp.zeros(obuf.shape[1:], F32)

    wait_gather(slot)

    @pl.when(i >= 1)
    def _reuse():
        wait_scatter(slot)

    gather(i + 1, 1 - slot)
    scatter(i - 1, 1 - slot)
    x = xbuf[slot]
    xb = x.astype(BF16)
    gate, _ = _moe_gate(_router_logits(x, xb, wr_ref), tg_ref[i].astype(F32))
    ffn = _group_ffn(xb, gate, tg_ref[i] * MOE_EXPERTS, w1_ref, w3_ref, w2_ref)
    obuf[slot] = _ln_rows(ALPHA * x + ffn, g_ref[...], b_ref[...])

    @pl.when(i == last)
    def _drain():
        scatter(i, slot)
        wait_gather(1 - slot)
        wait_scatter(1 - slot)
        wait_scatter(slot)


def _moe_sorted(tile_group, grp, layout, x, wr2, w1g, w3g, w2g, g, b, bm):
    n_tiles = tile_group.shape[0]
    n = grp.shape[0]
    assert bm & (bm - 1) == 0
    eff = MOE_EXPERTS * MOE_FF
    vec = pl.BlockSpec((1, D_MODEL), lambda i, tg, gr, st: (0, 0))
    hbm = pl.BlockSpec(memory_space=pl.ANY)
    return pl.pallas_call(
        functools.partial(_moe_sorted_body, bm=bm, n=n),
        grid_spec=pltpu.PrefetchScalarGridSpec(
            num_scalar_prefetch=3,
            grid=(n_tiles,),
            in_specs=[hbm,
                      pl.BlockSpec((D_MODEL, 2 * LANES), lambda i, tg, gr, st: (0, 0)),
                      pl.BlockSpec((MOE_EXPERTS, D_MODEL, MOE_FF), lambda i, tg, gr, st: (tg[i], 0, 0)),
                      pl.BlockSpec((MOE_EXPERTS, D_MODEL, MOE_FF), lambda i, tg, gr, st: (tg[i], 0, 0)),
                      pl.BlockSpec((1, eff, D_MODEL), lambda i, tg, gr, st: (tg[i], 0, 0)),
                      vec, vec],
            out_specs=hbm,
            scratch_shapes=[pltpu.VMEM((2, bm, D_MODEL), F32),
                            pltpu.VMEM((2, bm, D_MODEL), F32),
                            pltpu.SMEM(((n_tiles + 1) * bm,), jnp.int32),
                            pltpu.SMEM(((n_tiles + 1) * bm,), jnp.int32),
                            pltpu.SMEM((MOE_GROUPS,), jnp.int32),
                            pltpu.SemaphoreType.DMA((2,)),
                            pltpu.SemaphoreType.DMA((2,))]),
        out_shape=jax.ShapeDtypeStruct((n + 2 * bm, D_MODEL), F32),
        compiler_params=_cparams("arbitrary"),
        name="moe_sorted",
    )(tile_group, grp, layout, x, wr2, w1g, w3g, w2g, g, b)


def _moe_dispatch(x, n, wr, w1g, w3g, w2g, g, b, bm):
    wr_hi = wr.astype(BF16)
    wr_lo = (wr - wr_hi.astype(F32)).astype(BF16)
    wr2 = jnp.concatenate([wr_hi, wr_lo], axis=1)
    rt = 8 * LANES
    gidx, cnt = _router(x, wr2, n, rt)
    grp = gidx.astype(jnp.int32).reshape(n)
    counts = jnp.sum(cnt.reshape(n // rt, rt // LANES, LANES)[:, 0, :MOE_GROUPS], axis=0).astype(jnp.int32)
    padded = (counts + bm - 1) // bm * bm
    ends = jnp.cumsum(padded)
    starts = ends - padded
    n_tiles = n // bm + MOE_GROUPS
    tile_start = jnp.arange(n_tiles, dtype=jnp.int32) * bm
    tile_group = jnp.minimum(jnp.sum(tile_start[:, None] >= ends[None, :], axis=1), MOE_GROUPS - 1)
    layout = jnp.concatenate([starts, starts + counts, ends]).astype(jnp.int32)
    return _moe_sorted(tile_group.astype(jnp.int32), grp, layout, x, wr2, w1g, w3g, w2g, g, b, bm)


def _hgrn_gates(f, lb):
    log_sig = jnp.minimum(f, 0.0) - jnp.log1p(jnp.exp(-jnp.abs(f)))
    a = jnp.log1p(-lb) + log_sig
    log_lb = jnp.log(lb)
    log_f = jnp.maximum(log_lb, a) + jnp.log1p(jnp.exp(-jnp.abs(log_lb - a)))
    k = (1.0 - lb) / (1.0 + jnp.exp(f))
    return log_f, k


def _hgrn_finish(o, gate, ng):
    o = o * lax.rsqrt(jnp.mean(o * o, axis=-1, keepdims=True) + RMS_EPS) * ng
    return o * gate


def _rows_of(x, idx, n):
    return jnp.concatenate([jnp.broadcast_to(x[i:i + 1, :], (n, x.shape[1])) for i in idx], axis=0)


def _hgrn_scan_body(q_ref, f_ref, v_ref, g_ref, lb_ref, ng_ref, tri_ref, lm_ref, bsel_ref, dm_ref,
                    o_ref, s_ref, st_ref, *, n_super):
    C, c, N = HGRN_CHUNK, HGRN_SUB, HGRN_SUPER
    neg = -1e30
    tb = pl.program_id(2)

    @pl.when(tb == 0)
    def _init():
        st_ref[...] = jnp.zeros_like(st_ref)

    lb = lb_ref[...]
    ng = ng_ref[...]
    row = lax.broadcasted_iota(jnp.int32, (N, 1), 0)
    srow = lax.broadcasted_iota(jnp.int32, (c, 1), 0)
    levels = [(C >> l, C >> (l + 1)) for l in range(lm_ref.shape[0])]

    def super_chunk(si, carry):
        r0 = pl.multiple_of(si * N, N)
        q = q_ref[pl.ds(r0, N), :]
        v = v_ref[pl.ds(r0, N), :].astype(BF16)
        gate = g_ref[pl.ds(r0, N), :]
        log_f, k = _hgrn_gates(f_ref[pl.ds(r0, N), :], lb)

        hi = log_f.astype(BF16)
        r1 = log_f - hi.astype(F32)
        mid = r1.astype(BF16)
        lo = (r1 - mid.astype(F32)).astype(BF16)
        cs = jnp.dot(tri_ref[...], jnp.concatenate([hi, mid, lo], axis=1), preferred_element_type=F32)
        b2 = (cs[:, :A_DK] + cs[:, A_DK:2 * A_DK] + cs[:, 2 * A_DK:]) * LOG2E

        att_t = None
        for li, (blk, half) in enumerate(levels):
            up = (row & (blk - 1)) >= half
            ref = _rows_of(b2, [blk * m + half - 1 for m in range(N // blk)], blk)
            e = jnp.exp2(jnp.where(up, b2 - ref, ref - b2))
            ql = jnp.where(up, q * e, 0.0).astype(BF16)
            kl = jnp.where(up, 0.0, k * e).astype(BF16)
            term = _nt_dot(kl, ql) * lm_ref[li]
            att_t = term if att_t is None else att_t + term

        ys = []
        for i in range(N // c):
            base = c * i
            kb = k[base:base + c, :]
            bb = b2[base:base + c, :]
            units = []
            for tl in range(c):
                d = jnp.where(srow <= tl, b2[base + tl:base + tl + 1, :] - bb, neg)
                units.append(q[base + tl:base + tl + 1, :] * kb * jnp.exp2(d))
            ys.append(jnp.concatenate(units, axis=1))
        y = jnp.concatenate(ys, axis=0).astype(BF16)
        r = jnp.dot(y, bsel_ref[...], preferred_element_type=F32)
        diag = jnp.concatenate([r * dm_ref[0], r * dm_ref[1]], axis=1)
        o_intra = _tn_dot((att_t + diag).astype(BF16), v)

        bl = _rows_of(b2, [C * m + C - 1 for m in range(N // C)], C)
        qe = (q * jnp.exp2(b2)).astype(BF16)
        kd = (k * jnp.exp2(bl - b2)).astype(BF16)
        st = st_ref[...]
        nc = N // C
        states, lhs = [], []
        zero = jnp.zeros((C, A_DK), BF16)
        for m in range(nc):
            rows = slice(C * m, C * m + C)
            states.append(st.astype(BF16))
            lhs.append(jnp.concatenate([qe[rows, :] if j == m else zero for j in range(nc)], axis=1))
            st = st * jnp.exp2(b2[C * m + C - 1:C * m + C, :]) + _tn_dot(v[rows, :], kd[rows, :])
        st_ref[...] = st
        o = o_intra + _nt_dot(jnp.concatenate(lhs, axis=0), jnp.concatenate(states, axis=1))
        o_ref[pl.ds(r0, N), :] = _hgrn_finish(o, gate, ng).astype(o_ref.dtype)
        return carry

    for si in range(n_super):
        super_chunk(si, 0)

    @pl.when(tb == pl.num_programs(2) - 1)
    def _emit():
        s_ref[0, 0] = st_ref[...].T


def _hgrn_scan(p, lb, ng, batch, seq, tb):
    nt = seq // tb
    nh = A_HEADS
    n, c = HGRN_SUPER, HGRN_SUB
    assert n == 2 * LANES and LANES % c == 0
    s_i = jnp.arange(n)[:, None]
    t_i = jnp.arange(n)[None, :]
    tri = ((s_i >= t_i) & (s_i // HGRN_CHUNK == t_i // HGRN_CHUNK)).astype(BF16)
    blocks = []
    blk = HGRN_CHUNK
    while blk > c:
        blocks.append(blk)
        blk //= 2
    lm = jnp.stack([(s_i // bk == t_i // bk) for bk in blocks]).astype(F32)
    lane = jnp.arange(LANES)[None, :]
    bsel = (jnp.arange(c * A_DK)[:, None] // A_DK == lane % c).astype(BF16)
    dm = jnp.stack([(s_i // c == g * (LANES // c) + lane // c) for g in range(2)]).astype(F32)

    def col(off):
        return pl.BlockSpec((tb, A_DK), lambda b, h, t: (b * nt + t, off * nh + h))

    def const(a):
        return pl.BlockSpec(a.shape, lambda b, h, t: (0,) * a.ndim)

    return pl.pallas_call(
        functools.partial(_hgrn_scan_body, n_super=tb // n),
        grid=(batch, nh, nt),
        in_specs=[col(0), col(1), col(2), col(3),
                  pl.BlockSpec((1, A_DK), lambda b, h, t: (0, h)),
                  pl.BlockSpec((1, A_DV), lambda b, h, t: (0, 0)),
                  const(tri), const(lm), const(bsel), const(dm)],
        out_specs=[pl.BlockSpec((tb, A_DV), lambda b, h, t: (b * nt + t, h)),
                   pl.BlockSpec((1, 1, A_DK, A_DV), lambda b, h, t: (b, h, 0, 0))],
        out_shape=[jax.ShapeDtypeStruct((batch * seq, nh * A_DV), BF16),
                   jax.ShapeDtypeStruct((batch, nh, A_DK, A_DV), F32)],
        scratch_shapes=[pltpu.VMEM((A_DV, A_DK), F32)],
        compiler_params=_cparams("parallel", "parallel", "arbitrary"),
        name="hgrn_scan",
    )(p, p, p, p, lb, ng, tri, lm, bsel, dm)


def _hgrn_step_body(p_ref, s0_ref, lb_ref, ng_ref, o_ref, s_ref, *, n_tok):
    R = p_ref.shape[0]
    row = lax.broadcasted_iota(jnp.int32, (R, 1), 0)
    valid = row < n_tok
    r2 = lax.broadcasted_iota(jnp.int32, (R, R), 0)
    c2 = lax.broadcasted_iota(jnp.int32, (R, R), 1)
    tril = (r2 >= c2).astype(F32)
    ng = ng_ref[...]
    for h in range(A_HEADS):
        sl = slice(h * A_DK, (h + 1) * A_DK)
        q = p_ref[:, sl]
        v = p_ref[:, 2 * A_WIDTH + h * A_DV:2 * A_WIDTH + (h + 1) * A_DV]
        gate = p_ref[:, 3 * A_WIDTH + h * A_DV:3 * A_WIDTH + (h + 1) * A_DV]
        log_f, k = _hgrn_gates(p_ref[:, A_WIDTH + h * A_DK:A_WIDTH + (h + 1) * A_DK], lb_ref[:, sl])
        b = jnp.dot(tril, log_f, precision=HIGHEST, preferred_element_type=F32)
        st = s0_ref[0, h].T
        o = _nt_dot((q * jnp.exp(b)).astype(BF16), st.astype(BF16))
        for s in range(n_tok):
            m = row >= s
            w = jnp.where(m, q * k[s:s + 1, :] * jnp.exp(jnp.where(m, b - b[s:s + 1, :], 0.0)), 0.0)
            o = o + jnp.sum(w, axis=-1, keepdims=True) * v[s:s + 1, :]
        o_ref[:, h * A_DV:(h + 1) * A_DV] = _hgrn_finish(o, gate, ng)
        bl = b[n_tok - 1:n_tok, :]
        kd = jnp.where(valid, k * jnp.exp(jnp.where(valid, bl - b, 0.0)), 0.0)
        st_new = st * jnp.exp(bl) + _tn_dot(v.astype(BF16), kd.astype(BF16))
        s_ref[0, h] = st_new.T


def _hgrn_step(p, s0, lb, ng, batch, n_tok):
    rows = p.shape[0] // batch
    return pl.pallas_call(
        functools.partial(_hgrn_step_body, n_tok=n_tok),
        grid=(batch,),
        in_specs=[pl.BlockSpec((rows, 4 * A_WIDTH), lambda b: (b, 0)),
                  pl.BlockSpec((1, A_HEADS, A_DK, A_DV), lambda b: (b, 0, 0, 0)),
                  pl.BlockSpec((1, A_WIDTH), lambda b: (0, 0)),
                  pl.BlockSpec((1, A_DV), lambda b: (0, 0))],
        out_specs=[pl.BlockSpec((rows, A_HEADS * A_DV), lambda b: (b, 0)),
                   pl.BlockSpec((1, A_HEADS, A_DK, A_DV), lambda b: (b, 0, 0, 0))],
        out_shape=[jax.ShapeDtypeStruct((batch * rows, A_HEADS * A_DV), F32),
                   jax.ShapeDtypeStruct((batch, A_HEADS, A_DK, A_DV), F32)],
        compiler_params=_cparams("parallel"),
        name="hgrn_step",
    )(p, s0, lb, ng)


def _gmlp_prologue(u_ref, v_ref, ws_ref, bs_ref, gated_ref, *, chunk):
    bm = u_ref.shape[0]
    r2 = lax.broadcasted_iota(jnp.int32, (chunk, chunk), 0)
    c2 = lax.broadcasted_iota(jnp.int32, (chunk, chunk), 1)
    causal = r2 >= c2
    for h in range(B_HEADS):
        wc = jnp.where(causal, ws_ref[h], 0.0).astype(BF16)
        bias = bs_ref[:, h:h + 1]
        cols = slice(h * B_HD, (h + 1) * B_HD)
        for n in range(bm // chunk):
            rows = slice(n * chunk, (n + 1) * chunk)
            mixed = jnp.dot(wc, v_ref[rows, cols].astype(BF16), preferred_element_type=F32) + bias
            gated_ref[rows, cols] = (u_ref[rows, cols] * mixed).astype(BF16)
    return gated_ref[...]


def _gmlp_out_body(u_ref, v_ref, ws_ref, bs_ref, w_ref, r_ref, g_ref, b_ref, o_ref, gated_ref, *, chunk):
    a = _gmlp_prologue(u_ref, v_ref, ws_ref, bs_ref, gated_ref, chunk=chunk)
    acc = jnp.dot(a, w_ref[...], preferred_element_type=F32)
    y = ALPHA * r_ref[...] + acc
    o_ref[...] = _ln_rows(y, g_ref[...], b_ref[...])


def _gmlp_out(uv, ws, bs_t, w, resid, g, b, bm, chunk):
    m = uv.shape[0]
    row = pl.BlockSpec((bm, D_MODEL), lambda i: (i, 0))
    vec = pl.BlockSpec((1, D_MODEL), lambda i: (0, 0))
    return pl.pallas_call(
        functools.partial(_gmlp_out_body, chunk=chunk),
        grid=(m // bm,),
        in_specs=[pl.BlockSpec((bm, D_MODEL), lambda i: (i, 0)),
                  pl.BlockSpec((bm, D_MODEL), lambda i: (i, 1)),
                  pl.BlockSpec(ws.shape, lambda i: (0, 0, 0)),
                  pl.BlockSpec(bs_t.shape, lambda i: (0, 0)),
                  pl.BlockSpec((D_MODEL, D_MODEL), lambda i: (0, 0)),
                  row, vec, vec],
        out_specs=row,
        out_shape=jax.ShapeDtypeStruct((m, D_MODEL), F32),
        scratch_shapes=[pltpu.VMEM((bm, D_MODEL), BF16)],
        compiler_params=_cparams("parallel"),
        name="gmlp_out",
    )(uv, uv, ws, bs_t, w, resid, g, b)


def _band_attn_body(q_ref, k_ref, v_ref, o_ref, l_ref, *, dil, seq):
    bq = C_QBLOCK
    span = C_KEYS - 1
    scale = C_HD ** -0.5
    n_blocks = seq // dil // bq
    qi = lax.broadcasted_iota(jnp.int32, (bq, 2 * bq), 0)
    ki = lax.broadcasted_iota(jnp.int32, (bq, 2 * bq), 1)

    def rows(first, n):
        return pl.ds(first, n) if dil == 1 else pl.ds(first, n, stride=dil)

    for r in range(dil):
        for i in range(n_blocks):
            w = max(i - 1, 0)
            qs = rows(r + dil * bq * i, bq)
            ws = rows(r + dil * bq * w, 2 * bq)
            q = q_ref[qs, :].astype(BF16)
            kw = k_ref[ws, :].astype(BF16)
            vw = v_ref[ws, :].astype(BF16)
            s = _nt_dot(q, kw) * scale
            rel = bq * (i - w) + qi - ki
            s = jnp.where((rel >= 0) & (rel <= span), s, -jnp.inf)
            mx = jnp.max(s, axis=-1, keepdims=True)
            p = jnp.exp(s - mx)
            den = jnp.sum(p, axis=-1, keepdims=True)
            o_ref[qs, :] = jnp.dot(p.astype(BF16), vw, preferred_element_type=F32) / den
            l_ref[qs, :] = jnp.broadcast_to(mx + jnp.log(den), (bq, C_HD))


def _band_attn(qkv, g, dil, batch, seq):
    ng = len(C_GROUPS)

    def col(part):
        return pl.BlockSpec((seq, C_HD), lambda b, h: (b, (part * ng + g) * C_HEADS + h))

    out = pl.BlockSpec((seq, C_HD), lambda b, h: (b, h))
    return pl.pallas_call(
        functools.partial(_band_attn_body, dil=dil, seq=seq),
        grid=(batch, C_HEADS),
        in_specs=[col(0), col(1), col(2)],
        out_specs=[out, out],
        out_shape=[jax.ShapeDtypeStruct((batch * seq, C_HEADS * C_HD), F32)] * 2,
        compiler_params=_cparams("parallel", "parallel"),
        name=f"band_attn_d{dil}",
    )(qkv, qkv, qkv)


def _step_attn_body(qkv_ref, *refs, g, dil, n_tok):
    past_refs, (o_ref, l_ref) = refs[:-2], refs[-2:]
    scale = C_HD ** -0.5
    ng = len(C_GROUPS)
    hs = C_HEADS
    tok = lax.broadcasted_iota(jnp.int32, (n_tok, 1, 1), 0)
    prow = lax.broadcasted_iota(jnp.int32, (past_refs[0].shape[1], 1, 1), 0)
    kn = qkv_ref[0, :, (ng + g) * hs:(ng + g + 1) * hs, :]
    vn = qkv_ref[0, :, (2 * ng + g) * hs:(2 * ng + g + 1) * hs, :]
    for t in range(n_tok):
        past_ref = past_refs[t % dil]
        new_ok = (tok <= t) & (((t - tok) % dil) == 0)
        past_ok = prow >= (t if dil == 1 else 0)
        q = qkv_ref[0, t, g * hs:(g + 1) * hs, :][None]
        kp = past_ref[0, :, 0:hs, :]
        vp = past_ref[0, :, hs:2 * hs, :]
        sp = jnp.where(past_ok, jnp.sum(kp * q, axis=-1, keepdims=True) * scale, -jnp.inf)
        sn = jnp.where(new_ok, jnp.sum(kn * q, axis=-1, keepdims=True) * scale, -jnp.inf)
        mx = jnp.maximum(jnp.max(sp, axis=0, keepdims=True), jnp.max(sn, axis=0, keepdims=True))
        pp = jnp.exp(sp - mx)
        pn = jnp.exp(sn - mx)
        den = jnp.sum(pp, axis=0, keepdims=True) + jnp.sum(pn, axis=0, keepdims=True)
        o = (jnp.sum(pp * vp, axis=0, keepdims=True) + jnp.sum(pn * vn, axis=0, keepdims=True)) / den
        o_ref[0, t] = o[0]
        l_ref[0, t] = jnp.broadcast_to(mx + jnp.log(den), (1, hs, C_HD))[0]


def _step_attn(qkv, cache, g, window, dil, batch, n_tok):
    past = cache.reshape(batch, window // dil, dil, 2 * C_HEADS, C_HD)
    n_res = min(dil, n_tok)
    out = pl.BlockSpec((1, n_tok, C_HEADS, C_HD), lambda b: (b, 0, 0, 0))
    past_specs = [pl.BlockSpec((1, window // dil, None, 2 * C_HEADS, C_HD),
                               functools.partial(lambda b, r: (b, 0, r, 0, 0), r=r)) for r in range(n_res)]
    return pl.pallas_call(
        functools.partial(_step_attn_body, g=g, dil=dil, n_tok=n_tok),
        grid=(batch,),
        in_specs=[pl.BlockSpec((1, n_tok) + qkv.shape[2:], lambda b: (b, 0, 0, 0))] + past_specs,
        out_specs=[out, out],
        out_shape=[jax.ShapeDtypeStruct((batch, n_tok, C_HEADS, C_HD), F32)] * 2,
        compiler_params=_cparams("parallel"),
        name=f"step_attn_d{dil}",
    )(qkv, *([past] * n_res))


def _rope_tables(pos):
    half = C_HD // 2
    inv = ROPE_THETA ** (-jnp.arange(half, dtype=F32) / half)
    ang = pos.astype(F32)[:, None] * inv[None, :]
    cos, sin = jnp.cos(ang), jnp.sin(ang)
    return jnp.concatenate([cos, cos], -1), jnp.concatenate([-sin, sin], -1)


def _row_tile(m, cap):
    return min(m, cap)


def _hgrn_layer(x, batch, seq, s0, w_in, lb, ng, w_out, ln_g, ln_b):
    m = batch * seq
    bn = 1024
    p = _proj(x, w_in, (), (), functools.partial(_hgrn_proj_epilogue, bn=bn), m, _row_tile(m, PROJ_ROWS), bn,
              PROJ_CHUNK, "hgrn_proj")
    if s0 is None:
        o, s_new = _hgrn_scan(p, lb, ng, batch, seq, HGRN_ROWS if seq % HGRN_ROWS == 0 else HGRN_SUPER)
    else:
        rows = 8
        pp = jnp.pad(p.reshape(batch, seq, -1), ((0, 0), (0, rows - seq), (0, 0))).reshape(batch * rows, -1)
        o, s_new = _hgrn_step(pp, s0, lb, ng, batch, seq)
        o = o.reshape(batch, rows, -1)[:, :seq].reshape(m, -1)
    bm = _row_tile(m, 512)
    x = _out_ln((o,), (pl.BlockSpec((bm, o.shape[1]), lambda i: (i, 0)),), _cast_prologue,
                w_out, x, ln_g, ln_b, bm, "hgrn_out")
    return x, s_new


def _gmlp_layer(x, batch, seq, w_in, b_in, g1, b1, ws, bs, w_out, ln_g, ln_b):
    m = batch * seq
    bm = _row_tile(m, 512)
    vec = pl.BlockSpec((1, D_MODEL), lambda i, j: (0, 0))
    uv = _proj(x, w_in, (b_in, g1, b1),
               (pl.BlockSpec((1, D_MODEL), lambda i, j: (0, j)), vec, vec),
               _gmlp_proj_epilogue, m, bm, D_MODEL, D_MODEL, "gmlp_proj")
    if seq % B_CHUNK == 0:
        chunk, ws_c, bs_t = B_CHUNK, ws, bs.T
    else:
        chunk = m
        eye = jnp.eye(batch, dtype=ws.dtype)
        ws_c = jnp.einsum("ab,hts->hatbs", eye, ws[:, :seq, :seq]).reshape(B_HEADS, m, m)
        bs_t = jnp.tile(bs[:, :seq].T, (batch, 1))
    x = _gmlp_out(uv, ws_c, bs_t, w_out, x, ln_g, ln_b, _row_tile(m, 256), chunk)
    return x, uv


def _attn_layer(x, batch, seq, caches, pos0, w_in, w_out, ln_g, ln_b):
    m = batch * seq
    bn = 1024
    bm = _row_tile(m, PROJ_ROWS)
    cos, sin = _rope_tables(pos0 + jnp.arange(seq, dtype=jnp.int32))
    cos, sin = jnp.tile(cos, (batch, 1)), jnp.tile(sin, (batch, 1))
    tab = pl.BlockSpec((bm, C_HD), lambda i, j: (i, 0))
    qkv = _proj(x, w_in, (cos, sin), (tab, tab), functools.partial(_attn_proj_epilogue, bn=bn), m, bm, bn,
                PROJ_CHUNK, "attn_proj")
    qkv3 = qkv.reshape(batch, seq, C_QKV)
    outs, lses = [], []
    for g, (window, dil) in enumerate(C_GROUPS):
        if caches is None:
            o, lse = _band_attn(qkv, g, dil, batch, seq)
        else:
            o, lse = _step_attn(qkv.reshape(batch, seq, C_QKV // C_HD, C_HD), caches[g], g, window, dil,
                                batch, seq)
            o, lse = o.reshape(m, -1), lse.reshape(m, -1)
        outs.append(o)
        lses.append(lse)
    bm2 = _row_tile(m, 256)
    spec = pl.BlockSpec((bm2, C_HEADS * C_HD), lambda i: (i, 0))
    x = _out_ln(tuple(outs) + tuple(lses), (spec,) * 6, _merge_prologue, w_out, x, ln_g, ln_b, bm2,
                "attn_out")
    hw = C_HEADS * C_HD
    ng = len(C_GROUPS)
    kv = []
    for g, (window, _) in enumerate(C_GROUPS):
        tail = qkv3[:, seq - min(window, seq):]
        k = tail[:, :, (ng + g) * hw:(ng + g + 1) * hw].reshape(batch, -1, C_HEADS, C_HD)
        v = tail[:, :, (2 * ng + g) * hw:(2 * ng + g + 1) * hw].reshape(batch, -1, C_HEADS, C_HD)
        kv.append(jnp.stack([k, v], axis=2))
    return x, kv


def kernel(x_prompt, x_sample, state_hgrn, cache_c_kv_w128, cache_c_kv_w512, cache_c_kv_w2048, ln_g, ln_b, a_w_in, a_lb_logits, a_norm_g, a_w_out, b_w_in, b_b_in, b_ln_g, b_ln_b, b_w_s, b_b_s, b_w_out, c_w_in, c_w_out, moe_w_group, moe_w_expert, moe_w1, moe_w3, moe_w2):
    bp, tp, _ = x_prompt.shape
    bs, ts, _ = x_sample.shape
    assert tp % HGRN_SUPER == 0 and tp % B_CHUNK == 0 and tp // C_GROUPS[-1][1] >= 2 * C_QBLOCK
    assert ts <= 8 and ts <= C_GROUPS[1][1] and (bp * tp) % MOE_TILE == 0

    lb_p = jax.nn.softmax(a_lb_logits.astype(F32), axis=0)
    lb_all = jnp.clip(jnp.cumsum(lb_p, axis=0) - lb_p[0:1], 0.0, 1.0 - 1e-6)
    caches = (cache_c_kv_w128, cache_c_kv_w512, cache_c_kv_w2048)

    xp = x_prompt.reshape(bp * tp, D_MODEL)
    xs = x_sample.reshape(bs * ts, D_MODEL)
    hgrn_p, hgrn_s, chunk_v_s = [], [], []
    kv_p = [[] for _ in C_GROUPS]
    kv_s = [[] for _ in C_GROUPS]

    for i in range(DEPTH):
        kind, j = i % 3, i // 3
        g0, b0 = ln_g[i, 0][None], ln_b[i, 0][None]
        if kind == 0:
            w_in, w_out = a_w_in[j].astype(BF16), a_w_out[j].astype(BF16)
            lb, ng = lb_all[j][None], a_norm_g[j][None]
            xp, sp = _hgrn_layer(xp, bp, tp, None, w_in, lb, ng, w_out, g0, b0)
            xs, ss = _hgrn_layer(xs, bs, ts, state_hgrn[j].astype(F32), w_in, lb, ng, w_out, g0, b0)
            hgrn_p.append(sp)
            hgrn_s.append(ss)
        elif kind == 1:
            w_in, w_out = b_w_in[j].astype(BF16), b_w_out[j].astype(BF16)
            args = (w_in, b_b_in[j][None], b_ln_g[j][None], b_ln_b[j][None], b_w_s[j], b_b_s[j], w_out, g0, b0)
            xp, _ = _gmlp_layer(xp, bp, tp, *args)
            xs, uvs = _gmlp_layer(xs, bs, ts, *args)
            chunk_v_s.append(uvs[:, D_MODEL:].reshape(bs, ts, D_MODEL))
        else:
            w_in, w_out = c_w_in[j].astype(BF16), c_w_out[j].astype(BF16)
            xp, kvp = _attn_layer(xp, bp, tp, None, 0, w_in, w_out, g0, b0)
            xs, kvs = _attn_layer(xs, bs, ts, tuple(c[j] for c in caches), PAST_LEN, w_in, w_out, g0, b0)
            for g in range(len(C_GROUPS)):
                kv_p[g].append(kvp[g])
                kv_s[g].append(kvs[g])
        wr = jnp.pad(jnp.concatenate([moe_w_expert[i], moe_w_group[i]], axis=1),
                     ((0, 0), (0, LANES - MOE_GE - MOE_GROUPS)))
        w1g, w3g = moe_w1[i].astype(BF16), moe_w3[i].astype(BF16)
        w2g = moe_w2[i].astype(BF16).reshape(MOE_GROUPS, MOE_EXPERTS * MOE_FF, D_MODEL)
        g1, b1 = ln_g[i, 1][None], ln_b[i, 1][None]
        xp = _moe_dispatch(xp, bp * tp, wr, w1g, w3g, w2g, g1, b1, MOE_TILE)
        xs = _moe(xs, wr, w1g, w3g, w2g, g1, b1, bs * ts)

    return (xp[:bp * tp].reshape(bp, tp, D_MODEL), xs.reshape(bs, ts, D_MODEL),
            jnp.stack(hgrn_p), jnp.stack(hgrn_s), jnp.stack(chunk_v_s),
            jnp.stack(kv_p[0]), jnp.stack(kv_s[0]), jnp.stack(kv_p[1]), jnp.stack(kv_s[1]),
            jnp.stack(kv_p[2]), jnp.stack(kv_s[2]))
```

```python
import functools

import jax
import jax.numpy as jnp
from jax import lax
from jax.experimental import pallas as pl
from jax.experimental.pallas import tpu as pltpu

F32 = jnp.float32
BF16 = jnp.bfloat16
HIGHEST = lax.Precision.HIGHEST

D_MODEL = 2048
DEPTH = 4
PAST_LEN = 16384
A_HEADS = 16
A_DK = 128
A_DV = 128
A_WIDTH = A_HEADS * A_DK
HGRN_CHUNK = 64
HGRN_SUB = 8
LOG2E = 1.4426950408889634
HGRN_SUPER = 256
HGRN_ROWS = 2048
B_CHUNK = 128
B_HEADS = 16
B_HD = 128
C_HEADS = 8
C_HD = 128
C_GROUPS = ((128, 1), (512, 4), (2048, 16))
C_KEYS = 129
C_QBLOCK = 128
C_QKV = 3 * len(C_GROUPS) * C_HEADS * C_HD
ROPE_THETA = 10000.0
MOE_GROUPS = 4
MOE_EXPERTS = 4
MOE_GE = MOE_GROUPS * MOE_EXPERTS
MOE_FF = 256
MOE_TILE = 256
LN_EPS = 1e-5
RMS_EPS = 1e-6
ALPHA = (2 * DEPTH) ** 0.25
LANES = 128
VMEM_LIMIT = 56 * 1024 * 1024
PROJ_ROWS = 1024
PROJ_CHUNK = 256


def _cparams(*sem):
    return pltpu.CompilerParams(dimension_semantics=sem, vmem_limit_bytes=VMEM_LIMIT)


def _sigmoid(x):
    return 1.0 / (1.0 + jnp.exp(-x))


def _ln_rows(y, g, b):
    mu = jnp.mean(y, axis=-1, keepdims=True)
    d = y - mu
    var = jnp.mean(d * d, axis=-1, keepdims=True)
    return d * lax.rsqrt(var + LN_EPS) * g + b


def _nt_dot(a, b):
    return lax.dot_general(a, b, (((1,), (1,)), ((), ())), preferred_element_type=F32)


def _tn_dot(a, b):
    return lax.dot_general(a, b, (((0,), (0,)), ((), ())), preferred_element_type=F32)


def _proj_body(x_ref, w_ref, *rest, epilogue, n_extra, chunk, finish):
    extras = rest[:n_extra]
    o_ref = rest[n_extra]
    xb_ref = rest[n_extra + 1]
    j = pl.program_id(1)

    @pl.when(j == 0)
    def _cast():
        xb_ref[...] = x_ref[...].astype(BF16)

    bn = w_ref.shape[1]
    for c in range(bn // chunk):
        cols = slice(c * chunk, (c + 1) * chunk)
        acc = jnp.dot(xb_ref[...], w_ref[:, cols], preferred_element_type=F32)
        epilogue(acc, j, extras, o_ref, cols)
    if finish is not None:
        finish(j, extras, o_ref)


def _proj(x, w, extras, extra_specs, epilogue, m, bm, bn, chunk, name, finish=None):
    k = x.shape[1]
    n = w.shape[1]
    return pl.pallas_call(
        functools.partial(_proj_body, epilogue=epilogue, n_extra=len(extras), chunk=chunk, finish=finish),
        grid=(m // bm, n // bn),
        in_specs=[pl.BlockSpec((bm, k), lambda i, j: (i, 0)),
                  pl.BlockSpec((k, bn), lambda i, j: (0, j))] + list(extra_specs),
        out_specs=pl.BlockSpec((bm, bn), lambda i, j: (i, j)),
        out_shape=jax.ShapeDtypeStruct((m, n), F32),
        scratch_shapes=[pltpu.VMEM((bm, k), BF16)],
        compiler_params=_cparams("parallel", "arbitrary"),
        name=name,
    )(x, w, *extras)


def _hgrn_proj_epilogue(acc, j, extras, o_ref, cols, *, bn):
    nq = A_WIDTH // bn
    is_silu = jnp.logical_or(j < nq, j >= 3 * nq)
    o_ref[:, cols] = acc * jnp.where(is_silu, _sigmoid(acc), 1.0)


def _gelu_tanh(z):
    return 0.5 * z * (1.0 + jnp.tanh(0.7978845608028654 * (z + 0.044715 * (z * z * z))))


def _gmlp_proj_epilogue(acc, j, extras, o_ref, cols):
    bias_ref = extras[0]
    o_ref[:, cols] = _gelu_tanh(acc + bias_ref[:, cols])


def _gmlp_proj_finish(j, extras, o_ref):
    _, g_ref, b_ref = extras

    @pl.when(j == 1)
    def _():
        o_ref[...] = _ln_rows(o_ref[...], g_ref[...], b_ref[...])


def _attn_proj_epilogue(acc, j, extras, o_ref, cols, *, bn):
    cos_ref, sin_ref = extras
    is_rot = j < 2 * len(C_GROUPS) * C_HEADS * C_HD // bn
    cos = cos_ref[...]
    sin = sin_ref[...]
    for h in range(acc.shape[1] // C_HD):
        xh = acc[:, h * C_HD:(h + 1) * C_HD]
        rot = xh * cos + pltpu.roll(xh, C_HD // 2, 1) * sin
        o_ref[:, cols.start + h * C_HD:cols.start + (h + 1) * C_HD] = jnp.where(is_rot, rot, xh)


def _out_ln_body(*refs, prologue, n_in):
    ins = refs[:n_in]
    w_ref, r_ref, g_ref, b_ref, o_ref = refs[n_in:n_in + 5]
    a = prologue(*ins)
    acc = jnp.dot(a, w_ref[...], preferred_element_type=F32)
    y = ALPHA * r_ref[...] + acc
    o_ref[...] = _ln_rows(y, g_ref[...], b_ref[...])


def _out_ln(ins, in_specs, prologue, w, resid, g, b, bm, name):
    m = ins[0].shape[0]
    k = w.shape[0]
    row = pl.BlockSpec((bm, D_MODEL), lambda i: (i, 0))
    vec = pl.BlockSpec((1, D_MODEL), lambda i: (0, 0))
    return pl.pallas_call(
        functools.partial(_out_ln_body, prologue=prologue, n_in=len(ins)),
        grid=(m // bm,),
        in_specs=list(in_specs) + [pl.BlockSpec((k, D_MODEL), lambda i: (0, 0)), row, vec, vec],
        out_specs=row,
        out_shape=jax.ShapeDtypeStruct((m, D_MODEL), F32),
        compiler_params=_cparams("parallel"),
        name=name,
    )(*ins, w, resid, g, b)


def _cast_prologue(a_ref):
    return a_ref[...].astype(BF16)


def _merge_prologue(o0, o1, o2, l0, l1, l2):
    a0, a1, a2 = l0[...], l1[...], l2[...]
    mx = jnp.maximum(jnp.maximum(a0, a1), a2)
    e0, e1, e2 = jnp.exp(a0 - mx), jnp.exp(a1 - mx), jnp.exp(a2 - mx)
    inv = 1.0 / (e0 + e1 + e2)
    w0, w1, w2 = e0 * inv, e1 * inv, e2 * inv
    heads = []
    for h in range(C_HEADS):
        cols = slice(h * C_HD, (h + 1) * C_HD)
        heads.append(w0[:, h:h + 1] * o0[:, cols] + w1[:, h:h + 1] * o1[:, cols] + w2[:, h:h + 1] * o2[:, cols])
    return jnp.concatenate(heads, axis=1).astype(BF16)


def _moe_gate(logits, group=None):
    lane = lax.broadcasted_iota(jnp.int32, logits.shape, 1).astype(F32)
    neg = -jnp.inf
    big = 4.0 * LANES
    gl = jnp.where((lane >= MOE_GE) & (lane < MOE_GE + MOE_GROUPS), logits, neg)
    gmax = jnp.max(gl, axis=-1, keepdims=True)
    den = jnp.sum(jnp.exp(gl - gmax), axis=-1, keepdims=True)
    if group is None:
        g_idx = jnp.min(jnp.where(gl == gmax, lane - MOE_GE, big), axis=-1, keepdims=True)
        g_top = 1.0 / den
    else:
        g_idx = group
        g_sel = jnp.sum(jnp.where(lane == MOE_GE + group, logits, 0.0), axis=-1, keepdims=True)
        g_top = jnp.exp(g_sel - gmax) / den
    lo = g_idx * MOE_EXPERTS
    el = jnp.where((lane >= lo) & (lane < lo + MOE_EXPERTS), logits, neg)
    m1 = jnp.max(el, axis=-1, keepdims=True)
    i1 = jnp.min(jnp.where(el == m1, lane, big), axis=-1, keepdims=True)
    el2 = jnp.where(lane == i1, neg, el)
    m2 = jnp.max(el2, axis=-1, keepdims=True)
    i2 = jnp.min(jnp.where(el2 == m2, lane, big), axis=-1, keepdims=True)
    r = jnp.exp(m2 - m1)
    w1 = g_top / (1.0 + r)
    w2 = w1 * r
    return jnp.where(lane == i1, w1, 0.0) + jnp.where(lane == i2, w2, 0.0), g_idx


def _gate_columns(gate, first):
    lane = lax.broadcasted_iota(jnp.int32, gate.shape, 1)
    return jnp.concatenate(
        [jnp.broadcast_to(jnp.sum(jnp.where(lane == first + e, gate, 0.0), axis=-1, keepdims=True),
                          (gate.shape[0], MOE_FF)) for e in range(MOE_EXPERTS)], axis=1)


def _group_ffn(xb, gate, first, w1_ref, w3_ref, w2_ref):
    h1 = jnp.concatenate([jnp.dot(xb, w1_ref[e], preferred_element_type=F32) for e in range(MOE_EXPERTS)], axis=1)
    h3 = jnp.concatenate([jnp.dot(xb, w3_ref[e], preferred_element_type=F32) for e in range(MOE_EXPERTS)], axis=1)
    hg = (h1 * _sigmoid(h1) * h3 * _gate_columns(gate, first)).astype(BF16)
    return jnp.dot(hg, w2_ref[0], preferred_element_type=F32)


def _moe_body(x_ref, wr_ref, w1_ref, w3_ref, w2_ref, g_ref, b_ref, o_ref, xb_ref, gate_ref, acc_ref):
    gi = pl.program_id(1)

    @pl.when(gi == 0)
    def _route():
        x = x_ref[...]
        xb_ref[...] = x.astype(BF16)
        logits = jnp.dot(x, wr_ref[...], precision=HIGHEST, preferred_element_type=F32)
        gate_ref[...] = _moe_gate(logits)[0]
        acc_ref[...] = jnp.zeros_like(acc_ref)

    acc_ref[...] += _group_ffn(xb_ref[...], gate_ref[...], gi * MOE_EXPERTS, w1_ref, w3_ref, w2_ref)

    @pl.when(gi == MOE_GROUPS - 1)
    def _finish():
        y = ALPHA * x_ref[...] + acc_ref[...]
        o_ref[...] = _ln_rows(y, g_ref[...], b_ref[...])


def _moe(x, wr, w1g, w3g, w2g, g, b, bm):
    m = x.shape[0]
    eff = MOE_EXPERTS * MOE_FF
    row = pl.BlockSpec((bm, D_MODEL), lambda i, e: (i, 0))
    vec = pl.BlockSpec((1, D_MODEL), lambda i, e: (0, 0))
    return pl.pallas_call(
        _moe_body,
        grid=(m // bm, MOE_GROUPS),
        in_specs=[row,
                  pl.BlockSpec((D_MODEL, LANES), lambda i, e: (0, 0)),
                  pl.BlockSpec((MOE_EXPERTS, D_MODEL, MOE_FF), lambda i, e: (e, 0, 0)),
                  pl.BlockSpec((MOE_EXPERTS, D_MODEL, MOE_FF), lambda i, e: (e, 0, 0)),
                  pl.BlockSpec((1, eff, D_MODEL), lambda i, e: (e, 0, 0)),
                  vec, vec],
        out_specs=row,
        out_shape=jax.ShapeDtypeStruct((m, D_MODEL), F32),
        scratch_shapes=[pltpu.VMEM((bm, D_MODEL), BF16),
                        pltpu.VMEM((bm, LANES), F32),
                        pltpu.VMEM((bm, D_MODEL), F32)],
        compiler_params=_cparams("parallel", "arbitrary"),
        name="moe",
    )(x, wr, w1g, w3g, w2g, g, b)


def _router_logits(x, xh, wr_ref):
    xl = (x - xh.astype(F32)).astype(BF16)
    a = jnp.dot(xh, wr_ref[...], preferred_element_type=F32)
    return a[:, :LANES] + a[:, LANES:] + jnp.dot(xl, wr_ref[:, :LANES], preferred_element_type=F32)


def _router_body(x_ref, wr_ref, gidx_ref, cnt_ref):
    x = x_ref[...]
    _, g_idx = _moe_gate(_router_logits(x, x.astype(BF16), wr_ref))
    bm = x.shape[0]
    t = lax.broadcasted_iota(jnp.int32, (bm, LANES), 0)
    lane = lax.broadcasted_iota(jnp.int32, (bm, LANES), 1)
    spread = jnp.where((t & (LANES - 1)) == lane, g_idx, 0.0).astype(BF16)
    r_i = lax.broadcasted_iota(jnp.int32, (bm // LANES, bm), 0)
    t_i = lax.broadcasted_iota(jnp.int32, (bm // LANES, bm), 1)
    sel = (jnp.right_shift(t_i, LANES.bit_length() - 1) == r_i).astype(BF16)
    gidx_ref[...] = jnp.dot(sel, spread, preferred_element_type=F32)
    onehot = (lane.astype(F32) == g_idx).astype(F32)
    cnt_ref[...] = jnp.broadcast_to(jnp.sum(onehot, axis=0, keepdims=True), cnt_ref.shape)


def _router(x, wr, m, bm):
    assert bm % (8 * LANES) == 0 and m % bm == 0
    rows = bm // LANES
    small = pl.BlockSpec((rows, LANES), lambda i: (i, 0))
    return pl.pallas_call(
        _router_body,
        grid=(m // bm,),
        in_specs=[pl.BlockSpec((bm, D_MODEL), lambda i: (i, 0)),
                  pl.BlockSpec((D_MODEL, 2 * LANES), lambda i: (0, 0))],
        out_specs=[small, small],
        out_shape=[jax.ShapeDtypeStruct((m // LANES, LANES), F32),
                   jax.ShapeDtypeStruct((m // LANES, LANES), F32)],
        compiler_params=_cparams("parallel"),
        name="moe_router",
    )(x, wr)


def _moe_sorted_body(tg_ref, grp_ref, lay_ref, x_hbm, wr_ref, w1_ref, w3_ref, w2_ref, g_ref, b_ref, o_hbm,
                     xbuf, obuf, src_ref, dst_ref, pos_ref, xsem, osem, *, bm, n):
    i = pl.program_id(0)
    last = pl.num_programs(0) - 1
    slot = i % 2
    shift = bm.bit_length() - 1

    def plan():
        ng = MOE_GROUPS
        total = src_ref.shape[0] - bm

        def fill(t, carry):
            g = grp_ref[t]
            d = pos_ref[g]
            pos_ref[g] = d + 1
            src_ref[d] = t
            dst_ref[bm + d] = t
            return carry

        def pad(s, carry):
            src_ref[s] = 0
            dst_ref[bm + s] = n + ((s >> shift) & 1) * bm + (s & (bm - 1))
            return carry

        def lead(u, carry):
            dst_ref[u] = n + bm + u
            src_ref[total + u] = 0
            return carry

        for g in range(ng):
            pos_ref[g] = lay_ref[g]
        lax.fori_loop(0, n, fill, 0, unroll=8)
        for g in range(ng):
            lax.fori_loop(lay_ref[ng + g], lay_ref[2 * ng + g], pad, 0)
        lax.fori_loop(lay_ref[3 * ng - 1], total, pad, 0)
        lax.fori_loop(0, bm, lead, 0, unroll=8)

    def gather(tile, s):
        for j in range(bm):
            r = src_ref[tile * bm + j]
            pltpu.make_async_copy(x_hbm.at[pl.ds(r, 1), :], xbuf.at[s, pl.ds(j, 1), :], xsem.at[s]).start()

    def wait_gather(s):
        pltpu.make_async_copy(x_hbm.at[pl.ds(0, bm), :], xbuf.at[s], xsem.at[s]).wait()

    def scatter(tile, s):
        for j in range(bm):
            r = dst_ref[(tile + 1) * bm + j]
            pltpu.make_async_copy(obuf.at[s, pl.ds(j, 1), :], o_hbm.at[pl.ds(r, 1), :], osem.at[s]).start()

    def wait_scatter(s):
        pltpu.make_async_copy(obuf.at[s], o_hbm.at[pl.ds(0, bm), :], osem.at[s]).wait()

    @pl.when(i == 0)
    def _first():
        plan()
        gather(0, 0)
        obuf[1] = jnp.zeros(obuf.shape[1:], F32)

    wait_gather(slot)

    @pl.when(i >= 1)
    def _reuse():
        wait_scatter(slot)

    gather(i + 1, 1 - slot)
    scatter(i - 1, 1 - slot)
    x = xbuf[slot]
    xb = x.astype(BF16)
    gate, _ = _moe_gate(_router_logits(x, xb, wr_ref), tg_ref[i].astype(F32))
    ffn = _group_ffn(xb, gate, tg_ref[i] * MOE_EXPERTS, w1_ref, w3_ref, w2_ref)
    obuf[slot] = _ln_rows(ALPHA * x + ffn, g_ref[...], b_ref[...])

    @pl.when(i == last)
    def _drain():
        scatter(i, slot)
        wait_gather(1 - slot)
        wait_scatter(1 - slot)
        wait_scatter(slot)


def _moe_sorted(tile_group, grp, layout, x, wr2, w1g, w3g, w2g, g, b, bm):
    n_tiles = tile_group.shape[0]
    n = grp.shape[0]
    assert bm & (bm - 1) == 0
    eff = MOE_EXPERTS * MOE_FF
    vec = pl.BlockSpec((1, D_MODEL), lambda i, tg, gr, st: (0, 0))
    hbm = pl.BlockSpec(memory_space=pl.ANY)
    return pl.pallas_call(
        functools.partial(_moe_sorted_body, bm=bm, n=n),
        grid_spec=pltpu.PrefetchScalarGridSpec(
            num_scalar_prefetch=3,
            grid=(n_tiles,),
            in_specs=[hbm,
                      pl.BlockSpec((D_MODEL, 2 * LANES), lambda i, tg, gr, st: (0, 0)),
                      pl.BlockSpec((MOE_EXPERTS, D_MODEL, MOE_FF), lambda i, tg, gr, st: (tg[i], 0, 0)),
                      pl.BlockSpec((MOE_EXPERTS, D_MODEL, MOE_FF), lambda i, tg, gr, st: (tg[i], 0, 0)),
                      pl.BlockSpec((1, eff, D_MODEL), lambda i, tg, gr, st: (tg[i], 0, 0)),
                      vec, vec],
            out_specs=hbm,
            scratch_shapes=[pltpu.VMEM((2, bm, D_MODEL), F32),
                            pltpu.VMEM((2, bm, D_MODEL), F32),
                            pltpu.SMEM(((n_tiles + 1) * bm,), jnp.int32),
                            pltpu.SMEM(((n_tiles + 1) * bm,), jnp.int32),
                            pltpu.SMEM((MOE_GROUPS,), jnp.int32),
                            pltpu.SemaphoreType.DMA((2,)),
                            pltpu.SemaphoreType.DMA((2,))]),
        out_shape=jax.ShapeDtypeStruct((n + 2 * bm, D_MODEL), F32),
        compiler_params=_cparams("arbitrary"),
        name="moe_sorted",
    )(tile_group, grp, layout, x, wr2, w1g, w3g, w2g, g, b)


def _moe_dispatch(x, n, wr, w1g, w3g, w2g, g, b, bm):
    wr_hi = wr.astype(BF16)
    wr_lo = (wr - wr_hi.astype(F32)).astype(BF16)
    wr2 = jnp.concatenate([wr_hi, wr_lo], axis=1)
    rt = 8 * LANES
    gidx, cnt = _router(x, wr2, n, rt)
    grp = gidx.astype(jnp.int32).reshape(n)
    counts = jnp.sum(cnt.reshape(n // rt, rt // LANES, LANES)[:, 0, :MOE_GROUPS], axis=0).astype(jnp.int32)
    padded = (counts + bm - 1) // bm * bm
    ends = jnp.cumsum(padded)
    starts = ends - padded
    n_tiles = n // bm + MOE_GROUPS
    tile_start = jnp.arange(n_tiles, dtype=jnp.int32) * bm
    tile_group = jnp.minimum(jnp.sum(tile_start[:, None] >= ends[None, :], axis=1), MOE_GROUPS - 1)
    layout = jnp.concatenate([starts, starts + counts, ends]).astype(jnp.int32)
    return _moe_sorted(tile_group.astype(jnp.int32), grp, layout, x, wr2, w1g, w3g, w2g, g, b, bm)


def _hgrn_gates(f, lb):
    log_sig = jnp.minimum(f, 0.0) - jnp.log1p(jnp.exp(-jnp.abs(f)))
    a = jnp.log1p(-lb) + log_sig
    log_lb = jnp.log(lb)
    log_f = jnp.maximum(log_lb, a) + jnp.log1p(jnp.exp(-jnp.abs(log_lb - a)))
    k = (1.0 - lb) / (1.0 + jnp.exp(f))
    return log_f, k


def _hgrn_finish(o, gate, ng):
    o = o * lax.rsqrt(jnp.mean(o * o, axis=-1, keepdims=True) + RMS_EPS) * ng
    return o * gate


def _rows_of(x, idx, n):
    return jnp.concatenate([jnp.broadcast_to(x[i:i + 1, :], (n, x.shape[1])) for i in idx], axis=0)


def _hgrn_scan_body(q_ref, f_ref, v_ref, g_ref, lb_ref, ng_ref, tri_ref, lm_ref, bsel_ref, dm_ref,
                    o_ref, s_ref, st_ref, *, n_super):
    C, c, N = HGRN_CHUNK, HGRN_SUB, HGRN_SUPER
    neg = -1e30
    tb = pl.program_id(2)

    @pl.when(tb == 0)
    def _init():
        st_ref[...] = jnp.zeros_like(st_ref)

    lb = lb_ref[...]
    ng = ng_ref[...]
    row = lax.broadcasted_iota(jnp.int32, (N, 1), 0)
    srow = lax.broadcasted_iota(jnp.int32, (c, 1), 0)
    levels = [(C >> l, C >> (l + 1)) for l in range(lm_ref.shape[0])]

    def super_chunk(si, carry):
        r0 = pl.multiple_of(si * N, N)
        q = q_ref[pl.ds(r0, N), :]
        v = v_ref[pl.ds(r0, N), :].astype(BF16)
        gate = g_ref[pl.ds(r0, N), :]
        log_f, k = _hgrn_gates(f_ref[pl.ds(r0, N), :], lb)

        hi = log_f.astype(BF16)
        r1 = log_f - hi.astype(F32)
        mid = r1.astype(BF16)
        lo = (r1 - mid.astype(F32)).astype(BF16)
        cs = jnp.dot(tri_ref[...], jnp.concatenate([hi, mid, lo], axis=1), preferred_element_type=F32)
        b2 = (cs[:, :A_DK] + cs[:, A_DK:2 * A_DK] + cs[:, 2 * A_DK:]) * LOG2E

        att_t = None
        for li, (blk, half) in enumerate(levels):
            up = (row & (blk - 1)) >= half
            ref = _rows_of(b2, [blk * m + half - 1 for m in range(N // blk)], blk)
            e = jnp.exp2(jnp.where(up, b2 - ref, ref - b2))
            ql = jnp.where(up, q * e, 0.0).astype(BF16)
            kl = jnp.where(up, 0.0, k * e).astype(BF16)
            term = _nt_dot(kl, ql) * lm_ref[li]
            att_t = term if att_t is None else att_t + term

        ys = []
        for i in range(N // c):
            base = c * i
            kb = k[base:base + c, :]
            bb = b2[base:base + c, :]
            units = []
            for tl in range(c):
                d = jnp.where(srow <= tl, b2[base + tl:base + tl + 1, :] - bb, neg)
                units.append(q[base + tl:base + tl + 1, :] * kb * jnp.exp2(d))
            ys.append(jnp.concatenate(units, axis=1))
        y = jnp.concatenate(ys, axis=0).astype(BF16)
        r = jnp.dot(y, bsel_ref[...], preferred_element_type=F32)
        diag = jnp.concatenate([r * dm_ref[0], r * dm_ref[1]], axis=1)
        o_intra = _tn_dot((att_t + diag).astype(BF16), v)

        bl = _rows_of(b2, [C * m + C - 1 for m in range(N // C)], C)
        qe = (q * jnp.exp2(b2)).astype(BF16)
        kd = (k * jnp.exp2(bl - b2)).astype(BF16)
        st = st_ref[...]
        nc = N // C
        states, lhs = [], []
        zero = jnp.zeros((C, A_DK), BF16)
        for m in range(nc):
            rows = slice(C * m, C * m + C)
            states.append(st.astype(BF16))
            lhs.append(jnp.concatenate([qe[rows, :] if j == m else zero for j in range(nc)], axis=1))
            st = st * jnp.exp2(b2[C * m + C - 1:C * m + C, :]) + _tn_dot(v[rows, :], kd[rows, :])
        st_ref[...] = st
        o = o_intra + _nt_dot(jnp.concatenate(lhs, axis=0), jnp.concatenate(states, axis=1))
        o_ref[pl.ds(r0, N), :] = _hgrn_finish(o, gate, ng).astype(o_ref.dtype)
        return carry

    for si in range(n_super):
        super_chunk(si, 0)

    @pl.when(tb == pl.num_programs(2) - 1)
    def _emit():
        s_ref[0, 0] = st_ref[...].T


def _hgrn_scan(p, lb, ng, batch, seq, tb):
    nt = seq // tb
    nh = A_HEADS
    n, c = HGRN_SUPER, HGRN_SUB
    assert n == 2 * LANES and LANES % c == 0
    s_i = jnp.arange(n)[:, None]
    t_i = jnp.arange(n)[None, :]
    tri = ((s_i >= t_i) & (s_i // HGRN_CHUNK == t_i // HGRN_CHUNK)).astype(BF16)
    blocks = []
    blk = HGRN_CHUNK
    while blk > c:
        blocks.append(blk)
        blk //= 2
    lm = jnp.stack([(s_i // bk == t_i // bk) for bk in blocks]).astype(F32)
    lane = jnp.arange(LANES)[None, :]
    bsel = (jnp.arange(c * A_DK)[:, None] // A_DK == lane % c).astype(BF16)
    dm = jnp.stack([(s_i // c == g * (LANES // c) + lane // c) for g in range(2)]).astype(F32)

    def col(off):
        return pl.BlockSpec((tb, A_DK), lambda b, h, t: (b * nt + t, off * nh + h))

    def const(a):
        return pl.BlockSpec(a.shape, lambda b, h, t: (0,) * a.ndim)

    return pl.pallas_call(
        functools.partial(_hgrn_scan_body, n_super=tb // n),
        grid=(batch, nh, nt),
        in_specs=[col(0), col(1), col(2), col(3),
                  pl.BlockSpec((1, A_DK), lambda b, h, t: (0, h)),
                  pl.BlockSpec((1, A_DV), lambda b, h, t: (0, 0)),
                  const(tri), const(lm), const(bsel), const(dm)],
        out_specs=[pl.BlockSpec((tb, A_DV), lambda b, h, t: (b * nt + t, h)),
                   pl.BlockSpec((1, 1, A_DK, A_DV), lambda b, h, t: (b, h, 0, 0))],
        out_shape=[jax.ShapeDtypeStruct((batch * seq, nh * A_DV), BF16),
                   jax.ShapeDtypeStruct((batch, nh, A_DK, A_DV), F32)],
        scratch_shapes=[pltpu.VMEM((A_DV, A_DK), F32)],
        compiler_params=_cparams("parallel", "parallel", "arbitrary"),
        name="hgrn_scan",
    )(p, p, p, p, lb, ng, tri, lm, bsel, dm)


def _hgrn_step_body(p_ref, s0_ref, lb_ref, ng_ref, o_ref, s_ref, *, n_tok):
    R = p_ref.shape[0]
    row = lax.broadcasted_iota(jnp.int32, (R, 1), 0)
    valid = row < n_tok
    r2 = lax.broadcasted_iota(jnp.int32, (R, R), 0)
    c2 = lax.broadcasted_iota(jnp.int32, (R, R), 1)
    tril = (r2 >= c2).astype(F32)
    ng = ng_ref[...]
    for h in range(A_HEADS):
        sl = slice(h * A_DK, (h + 1) * A_DK)
        q = p_ref[:, sl]
        v = p_ref[:, 2 * A_WIDTH + h * A_DV:2 * A_WIDTH + (h + 1) * A_DV]
        gate = p_ref[:, 3 * A_WIDTH + h * A_DV:3 * A_WIDTH + (h + 1) * A_DV]
        log_f, k = _hgrn_gates(p_ref[:, A_WIDTH + h * A_DK:A_WIDTH + (h + 1) * A_DK], lb_ref[:, sl])
        b = jnp.dot(tril, log_f, precision=HIGHEST, preferred_element_type=F32)
        st = s0_ref[0, h].T
        o = _nt_dot((q * jnp.exp(b)).astype(BF16), st.astype(BF16))
        for s in range(n_tok):
            m = row >= s
            w = jnp.where(m, q * k[s:s + 1, :] * jnp.exp(jnp.where(m, b - b[s:s + 1, :], 0.0)), 0.0)
            o = o + jnp.sum(w, axis=-1, keepdims=True) * v[s:s + 1, :]
        o_ref[:, h * A_DV:(h + 1) * A_DV] = _hgrn_finish(o, gate, ng)
        bl = b[n_tok - 1:n_tok, :]
        kd = jnp.where(valid, k * jnp.exp(jnp.where(valid, bl - b, 0.0)), 0.0)
        st_new = st * jnp.exp(bl) + _tn_dot(v.astype(BF16), kd.astype(BF16))
        s_ref[0, h] = st_new.T


def _hgrn_step(p, s0, lb, ng, batch, n_tok):
    rows = p.shape[0] // batch
    return pl.pallas_call(
        functools.partial(_hgrn_step_body, n_tok=n_tok),
        grid=(batch,),
        in_specs=[pl.BlockSpec((rows, 4 * A_WIDTH), lambda b: (b, 0)),
                  pl.BlockSpec((1, A_HEADS, A_DK, A_DV), lambda b: (b, 0, 0, 0)),
                  pl.BlockSpec((1, A_WIDTH), lambda b: (0, 0)),
                  pl.BlockSpec((1, A_DV), lambda b: (0, 0))],
        out_specs=[pl.BlockSpec((rows, A_HEADS * A_DV), lambda b: (b, 0)),
                   pl.BlockSpec((1, A_HEADS, A_DK, A_DV), lambda b: (b, 0, 0, 0))],
        out_shape=[jax.ShapeDtypeStruct((batch * rows, A_HEADS * A_DV), F32),
                   jax.ShapeDtypeStruct((batch, A_HEADS, A_DK, A_DV), F32)],
        compiler_params=_cparams("parallel"),
        name="hgrn_step",
    )(p, s0, lb, ng)


def _gmlp_prologue(u_ref, v_ref, ws_ref, bs_ref, gated_ref, *, chunk):
    bm = u_ref.shape[0]
    r2 = lax.broadcasted_iota(jnp.int32, (chunk, chunk), 0)
    c2 = lax.broadcasted_iota(jnp.int32, (chunk, chunk), 1)
    causal = r2 >= c2
    for h in range(B_HEADS):
        wc = jnp.where(causal, ws_ref[h], 0.0).astype(BF16)
        bias = bs_ref[:, h:h + 1]
        cols = slice(h * B_HD, (h + 1) * B_HD)
        for n in range(bm // chunk):
            rows = slice(n * chunk, (n + 1) * chunk)
            mixed = jnp.dot(wc, v_ref[rows, cols].astype(BF16), preferred_element_type=F32) + bias
            gated_ref[rows, cols] = (u_ref[rows, cols] * mixed).astype(BF16)
    return gated_ref[...]


def _gmlp_out_body(u_ref, v_ref, ws_ref, bs_ref, w_ref, r_ref, g_ref, b_ref, o_ref, gated_ref, *, chunk):
    a = _gmlp_prologue(u_ref, v_ref, ws_ref, bs_ref, gated_ref, chunk=chunk)
    acc = jnp.dot(a, w_ref[...], preferred_element_type=F32)
    y = ALPHA * r_ref[...] + acc
    o_ref[...] = _ln_rows(y, g_ref[...], b_ref[...])


def _gmlp_out(uv, ws, bs_t, w, resid, g, b, bm, chunk):
    m = uv.shape[0]
    row = pl.BlockSpec((bm, D_MODEL), lambda i: (i, 0))
    vec = pl.BlockSpec((1, D_MODEL), lambda i: (0, 0))
    return pl.pallas_call(
        functools.partial(_gmlp_out_body, chunk=chunk),
        grid=(m // bm,),
        in_specs=[pl.BlockSpec((bm, D_MODEL), lambda i: (i, 0)),
                  pl.BlockSpec((bm, D_MODEL), lambda i: (i, 1)),
                  pl.BlockSpec(ws.shape, lambda i: (0, 0, 0)),
                  pl.BlockSpec(bs_t.shape, lambda i: (0, 0)),
                  pl.BlockSpec((D_MODEL, D_MODEL), lambda i: (0, 0)),
                  row, vec, vec],
        out_specs=row,
        out_shape=jax.ShapeDtypeStruct((m, D_MODEL), F32),
        scratch_shapes=[pltpu.VMEM((bm, D_MODEL), BF16)],
        compiler_params=_cparams("parallel"),
        name="gmlp_out",
    )(uv, uv, ws, bs_t, w, resid, g, b)


def _band_attn_body(q_ref, k_ref, v_ref, o_ref, l_ref, *, dil, seq):
    bq = C_QBLOCK
    span = C_KEYS - 1
    scale = C_HD ** -0.5
    n_blocks = seq // dil // bq
    qi = lax.broadcasted_iota(jnp.int32, (bq, 2 * bq), 0)
    ki = lax.broadcasted_iota(jnp.int32, (bq, 2 * bq), 1)
    h = pl.program_id(1)
    mine = lax.broadcasted_iota(jnp.int32, (bq, LANES), 1) == h

    @pl.when(h == 0)
    def _clear():
        l_ref[...] = jnp.zeros_like(l_ref)

    def rows(first, n):
        return pl.ds(first, n) if dil == 1 else pl.ds(first, n, stride=dil)

    for r in range(dil):
        for i in range(n_blocks):
            w = max(i - 1, 0)
            qs = rows(r + dil * bq * i, bq)
            ws = rows(r + dil * bq * w, 2 * bq)
            q = q_ref[qs, :].astype(BF16)
            kw = k_ref[ws, :].astype(BF16)
            vw = v_ref[ws, :].astype(BF16)
            s = _nt_dot(q, kw) * scale
            rel = bq * (i - w) + qi - ki
            s = jnp.where((rel >= 0) & (rel <= span), s, -jnp.inf)
            mx = jnp.max(s, axis=-1, keepdims=True)
            p = jnp.exp(s - mx)
            den = jnp.sum(p, axis=-1, keepdims=True)
            o_ref[qs, :] = jnp.dot(p.astype(BF16), vw, preferred_element_type=F32) / den
            l_ref[qs, :] = jnp.where(mine, mx + jnp.log(den), l_ref[qs, :])


def _band_attn(qkv, g, dil, batch, seq):
    ng = len(C_GROUPS)

    def col(part):
        return pl.BlockSpec((seq, C_HD), lambda b, h: (b, (part * ng + g) * C_HEADS + h))

    return pl.pallas_call(
        functools.partial(_band_attn_body, dil=dil, seq=seq),
        grid=(batch, C_HEADS),
        in_specs=[col(0), col(1), col(2)],
        out_specs=[pl.BlockSpec((seq, C_HD), lambda b, h: (b, h)),
                   pl.BlockSpec((seq, LANES), lambda b, h: (b, 0))],
        out_shape=[jax.ShapeDtypeStruct((batch * seq, C_HEADS * C_HD), F32),
                   jax.ShapeDtypeStruct((batch * seq, LANES), F32)],
        compiler_params=_cparams("parallel", "arbitrary"),
        name=f"band_attn_d{dil}",
    )(qkv, qkv, qkv)


def _step_attn_body(qkv_ref, *refs, g, dil, n_tok):
    past_refs, (o_ref, l_ref) = refs[:-2], refs[-2:]
    scale = C_HD ** -0.5
    ng = len(C_GROUPS)
    hs = C_HEADS
    tok = lax.broadcasted_iota(jnp.int32, (n_tok, 1, 1), 0)
    prow = lax.broadcasted_iota(jnp.int32, (past_refs[0].shape[1], 1, 1), 0)
    kn = qkv_ref[0, :, (ng + g) * hs:(ng + g + 1) * hs, :]
    vn = qkv_ref[0, :, (2 * ng + g) * hs:(2 * ng + g + 1) * hs, :]
    for t in range(n_tok):
        past_ref = past_refs[t % dil]
        new_ok = (tok <= t) & (((t - tok) % dil) == 0)
        past_ok = prow >= (t if dil == 1 else 0)
        q = qkv_ref[0, t, g * hs:(g + 1) * hs, :][None]
        kp = past_ref[0, :, 0:hs, :]
        vp = past_ref[0, :, hs:2 * hs, :]
        sp = jnp.where(past_ok, jnp.sum(kp * q, axis=-1, keepdims=True) * scale, -jnp.inf)
        sn = jnp.where(new_ok, jnp.sum(kn * q, axis=-1, keepdims=True) * scale, -jnp.inf)
        mx = jnp.maximum(jnp.max(sp, axis=0, keepdims=True), jnp.max(sn, axis=0, keepdims=True))
        pp = jnp.exp(sp - mx)
        pn = jnp.exp(sn - mx)
        den = jnp.sum(pp, axis=0, keepdims=True) + jnp.sum(pn, axis=0, keepdims=True)
        o = (jnp.sum(pp * vp, axis=0, keepdims=True) + jnp.sum(pn * vn, axis=0, keepdims=True)) / den
        o_ref[0, t] = o[0]
        l_ref[0, t] = jnp.broadcast_to(mx + jnp.log(den), (1, hs, C_HD))[0]


def _step_attn(qkv, cache, g, window, dil, batch, n_tok):
    past = cache.reshape(batch, window // dil, dil, 2 * C_HEADS, C_HD)
    n_res = min(dil, n_tok)
    out = pl.BlockSpec((1, n_tok, C_HEADS, C_HD), lambda b: (b, 0, 0, 0))
    past_specs = [pl.BlockSpec((1, window // dil, None, 2 * C_HEADS, C_HD),
                               functools.partial(lambda b, r: (b, 0, r, 0, 0), r=r)) for r in range(n_res)]
    return pl.pallas_call(
        functools.partial(_step_attn_body, g=g, dil=dil, n_tok=n_tok),
        grid=(batch,),
        in_specs=[pl.BlockSpec((1, n_tok) + qkv.shape[2:], lambda b: (b, 0, 0, 0))] + past_specs,
        out_specs=[out, out],
        out_shape=[jax.ShapeDtypeStruct((batch, n_tok, C_HEADS, C_HD), F32)] * 2,
        compiler_params=_cparams("parallel"),
        name=f"step_attn_d{dil}",
    )(qkv, *([past] * n_res))


def _rope_tables(pos):
    half = C_HD // 2
    inv = ROPE_THETA ** (-jnp.arange(half, dtype=F32) / half)
    ang = pos.astype(F32)[:, None] * inv[None, :]
    cos, sin = jnp.cos(ang), jnp.sin(ang)
    return jnp.concatenate([cos, cos], -1), jnp.concatenate([-sin, sin], -1)


def _row_tile(m, cap):
    return min(m, cap)


def _hgrn_layer(x, batch, seq, s0, w_in, lb, ng, w_out, ln_g, ln_b):
    m = batch * seq
    bn = 1024
    p = _proj(x, w_in, (), (), functools.partial(_hgrn_proj_epilogue, bn=bn), m, _row_tile(m, PROJ_ROWS), bn,
              PROJ_CHUNK, "hgrn_proj")
    if s0 is None:
        o, s_new = _hgrn_scan(p, lb, ng, batch, seq, HGRN_ROWS if seq % HGRN_ROWS == 0 else HGRN_SUPER)
    else:
        rows = 8
        pp = jnp.pad(p.reshape(batch, seq, -1), ((0, 0), (0, rows - seq), (0, 0))).reshape(batch * rows, -1)
        o, s_new = _hgrn_step(pp, s0, lb, ng, batch, seq)
        o = o.reshape(batch, rows, -1)[:, :seq].reshape(m, -1)
    bm = _row_tile(m, 512)
    x = _out_ln((o,), (pl.BlockSpec((bm, o.shape[1]), lambda i: (i, 0)),), _cast_prologue,
                w_out, x, ln_g, ln_b, bm, "hgrn_out")
    return x, s_new


def _gmlp_layer(x, batch, seq, w_in, b_in, g1, b1, ws, bs, w_out, ln_g, ln_b):
    m = batch * seq
    bm = _row_tile(m, 512)
    vec = pl.BlockSpec((1, D_MODEL), lambda i, j: (0, 0))
    uv = _proj(x, w_in, (b_in, g1, b1),
               (pl.BlockSpec((1, D_MODEL), lambda i, j: (0, j)), vec, vec),
               _gmlp_proj_epilogue, m, bm, D_MODEL, PROJ_CHUNK, "gmlp_proj", finish=_gmlp_proj_finish)
    if seq % B_CHUNK == 0:
        chunk, ws_c, bs_t = B_CHUNK, ws, bs.T
    else:
        chunk = m
        eye = jnp.eye(batch, dtype=ws.dtype)
        ws_c = jnp.einsum("ab,hts->hatbs", eye, ws[:, :seq, :seq]).reshape(B_HEADS, m, m)
        bs_t = jnp.tile(bs[:, :seq].T, (batch, 1))
    x = _gmlp_out(uv, ws_c, bs_t, w_out, x, ln_g, ln_b, _row_tile(m, 256), chunk)
    return x, uv


def _attn_layer(x, batch, seq, caches, pos0, w_in, w_out, ln_g, ln_b):
    m = batch * seq
    bn = 1024
    bm = _row_tile(m, PROJ_ROWS)
    cos, sin = _rope_tables(pos0 + jnp.arange(seq, dtype=jnp.int32))
    cos, sin = jnp.tile(cos, (batch, 1)), jnp.tile(sin, (batch, 1))
    tab = pl.BlockSpec((bm, C_HD), lambda i, j: (i, 0))
    qkv = _proj(x, w_in, (cos, sin), (tab, tab), functools.partial(_attn_proj_epilogue, bn=bn), m, bm, bn,
                PROJ_CHUNK, "attn_proj")
    qkv3 = qkv.reshape(batch, seq, C_QKV)
    outs, lses = [], []
    for g, (window, dil) in enumerate(C_GROUPS):
        if caches is None:
            o, lse = _band_attn(qkv, g, dil, batch, seq)
        else:
            o, lse = _step_attn(qkv.reshape(batch, seq, C_QKV // C_HD, C_HD), caches[g], g, window, dil,
                                batch, seq)
            o = o.reshape(m, -1)
            lse = jnp.pad(lse[:, :, :, 0].reshape(m, C_HEADS), ((0, 0), (0, LANES - C_HEADS)))
        outs.append(o)
        lses.append(lse)
    bm2 = _row_tile(m, 512)
    spec = pl.BlockSpec((bm2, C_HEADS * C_HD), lambda i: (i, 0))
    lspec = pl.BlockSpec((bm2, LANES), lambda i: (i, 0))
    x = _out_ln(tuple(outs) + tuple(lses), (spec,) * 3 + (lspec,) * 3, _merge_prologue, w_out, x, ln_g, ln_b,
                bm2, "attn_out")
    hw = C_HEADS * C_HD
    ng = len(C_GROUPS)
    kv = []
    for g, (window, _) in enumerate(C_GROUPS):
        tail = qkv3[:, seq - min(window, seq):]
        k = tail[:, :, (ng + g) * hw:(ng + g + 1) * hw].reshape(batch, -1, C_HEADS, C_HD)
        v = tail[:, :, (2 * ng + g) * hw:(2 * ng + g + 1) * hw].reshape(batch, -1, C_HEADS, C_HD)
        kv.append(jnp.stack([k, v], axis=2))
    return x, kv


def kernel(x_prompt, x_sample, state_hgrn, cache_c_kv_w128, cache_c_kv_w512, cache_c_kv_w2048, ln_g, ln_b, a_w_in, a_lb_logits, a_norm_g, a_w_out, b_w_in, b_b_in, b_ln_g, b_ln_b, b_w_s, b_b_s, b_w_out, c_w_in, c_w_out, moe_w_group, moe_w_expert, moe_w1, moe_w3, moe_w2):
    bp, tp, _ = x_prompt.shape
    bs, ts, _ = x_sample.shape
    assert tp % HGRN_SUPER == 0 and tp % B_CHUNK == 0 and tp // C_GROUPS[-1][1] >= 2 * C_QBLOCK
    assert ts <= 8 and ts <= C_GROUPS[1][1] and (bp * tp) % MOE_TILE == 0

    lb_p = jax.nn.softmax(a_lb_logits.astype(F32), axis=0)
    lb_all = jnp.clip(jnp.cumsum(lb_p, axis=0) - lb_p[0:1], 0.0, 1.0 - 1e-6)
    caches = (cache_c_kv_w128, cache_c_kv_w512, cache_c_kv_w2048)

    xp = x_prompt.reshape(bp * tp, D_MODEL)
    xs = x_sample.reshape(bs * ts, D_MODEL)
    hgrn_p, hgrn_s, chunk_v_s = [], [], []
    kv_p = [[] for _ in C_GROUPS]
    kv_s = [[] for _ in C_GROUPS]

    for i in range(DEPTH):
        kind, j = i % 3, i // 3
        g0, b0 = ln_g[i, 0][None], ln_b[i, 0][None]
        if kind == 0:
            w_in, w_out = a_w_in[j].astype(BF16), a_w_out[j].astype(BF16)
            lb, ng = lb_all[j][None], a_norm_g[j][None]
            xp, sp = _hgrn_layer(xp, bp, tp, None, w_in, lb, ng, w_out, g0, b0)
            xs, ss = _hgrn_layer(xs, bs, ts, state_hgrn[j].astype(F32), w_in, lb, ng, w_out, g0, b0)
            hgrn_p.append(sp)
            hgrn_s.append(ss)
        elif kind == 1:
            w_in, w_out = b_w_in[j].astype(BF16), b_w_out[j].astype(BF16)
            args = (w_in, b_b_in[j][None], b_ln_g[j][None], b_ln_b[j][None], b_w_s[j], b_b_s[j], w_out, g0, b0)
            xp, _ = _gmlp_layer(xp, bp, tp, *args)
            xs, uvs = _gmlp_layer(xs, bs, ts, *args)
            chunk_v_s.append(uvs[:, D_MODEL:].reshape(bs, ts, D_MODEL))
        else:
            w_in, w_out = c_w_in[j].astype(BF16), c_w_out[j].astype(BF16)
            xp, kvp = _attn_layer(xp, bp, tp, None, 0, w_in, w_out, g0, b0)
            xs, kvs = _attn_layer(xs, bs, ts, tuple(c[j] for c in caches), PAST_LEN, w_in, w_out, g0, b0)
            for g in range(len(C_GROUPS)):
                kv_p[g].append(kvp[g])
                kv_s[g].append(kvs[g])
        wr = jnp.pad(jnp.concatenate([moe_w_expert[i], moe_w_group[i]], axis=1),
                     ((0, 0), (0, LANES - MOE_GE - MOE_GROUPS)))
        w1g, w3g = moe_w1[i].astype(BF16), moe_w3[i].astype(BF16)
        w2g = moe_w2[i].astype(BF16).reshape(MOE_GROUPS, MOE_EXPERTS * MOE_FF, D_MODEL)
        g1, b1 = ln_g[i, 1][None], ln_b[i, 1][None]
        xp = _moe_dispatch(xp, bp * tp, wr, w1g, w3g, w2g, g1, b1, MOE_TILE)
        xs = _moe(xs, wr, w1g, w3g, w2g, g1, b1, bs * ts)

    return (xp[:bp * tp].reshape(bp, tp, D_MODEL), xs.reshape(bs, ts, D_MODEL),
            jnp.stack(hgrn_p), jnp.stack(hgrn_s), jnp.stack(chunk_v_s),
            jnp.stack(kv_p[0]), jnp.stack(kv_s[0]), jnp.stack(kv_p[1]), jnp.stack(kv_s[1]),
            jnp.stack(kv_p[2]), jnp.stack(kv_s[2]))
```

```python
import functools

import jax
import jax.numpy as jnp
from jax import lax
from jax.experimental import pallas as pl
from jax.experimental.pallas import tpu as pltpu

F32 = jnp.float32
BF16 = jnp.bfloat16
HIGHEST = lax.Precision.HIGHEST

D_MODEL = 2048
DEPTH = 4
PAST_LEN = 16384
A_HEADS = 16
A_DK = 128
A_DV = 128
A_WIDTH = A_HEADS * A_DK
HGRN_CHUNK = 64
HGRN_SUB = 8
LOG2E = 1.4426950408889634
HGRN_SUPER = 256
HGRN_ROWS = 2048
B_CHUNK = 128
B_HEADS = 16
B_HD = 128
C_HEADS = 8
C_HD = 128
C_GROUPS = ((128, 1), (512, 4), (2048, 16))
C_KEYS = 129
C_QBLOCK = 128
C_QKV = 3 * len(C_GROUPS) * C_HEADS * C_HD
ROPE_THETA = 10000.0
MOE_GROUPS = 4
MOE_EXPERTS = 4
MOE_GE = MOE_GROUPS * MOE_EXPERTS
MOE_FF = 256
MOE_TILE = 512
LN_EPS = 1e-5
RMS_EPS = 1e-6
ALPHA = (2 * DEPTH) ** 0.25
LANES = 128
VMEM_LIMIT = 56 * 1024 * 1024
PROJ_ROWS = 1024
PROJ_CHUNK = 256


def _cparams(*sem):
    return pltpu.CompilerParams(dimension_semantics=sem, vmem_limit_bytes=VMEM_LIMIT)


def _sigmoid(x):
    return 1.0 / (1.0 + jnp.exp(-x))


def _ln_rows(y, g, b):
    mu = jnp.mean(y, axis=-1, keepdims=True)
    d = y - mu
    var = jnp.mean(d * d, axis=-1, keepdims=True)
    return d * lax.rsqrt(var + LN_EPS) * g + b


def _nt_dot(a, b):
    return lax.dot_general(a, b, (((1,), (1,)), ((), ())), preferred_element_type=F32)


def _tn_dot(a, b):
    return lax.dot_general(a, b, (((0,), (0,)), ((), ())), preferred_element_type=F32)


def _proj_body(x_ref, w_ref, *rest, epilogue, n_extra, chunk, finish):
    extras = rest[:n_extra]
    o_ref = rest[n_extra]
    xb_ref = rest[n_extra + 1]
    j = pl.program_id(1)

    @pl.when(j == 0)
    def _cast():
        xb_ref[...] = x_ref[...].astype(BF16)

    bn = w_ref.shape[1]
    for c in range(bn // chunk):
        cols = slice(c * chunk, (c + 1) * chunk)
        acc = jnp.dot(xb_ref[...], w_ref[:, cols], preferred_element_type=F32)
        epilogue(acc, j, extras, o_ref, cols)
    if finish is not None:
        finish(j, extras, o_ref)


def _proj(x, w, extras, extra_specs, epilogue, m, bm, bn, chunk, name, finish=None):
    k = x.shape[1]
    n = w.shape[1]
    return pl.pallas_call(
        functools.partial(_proj_body, epilogue=epilogue, n_extra=len(extras), chunk=chunk, finish=finish),
        grid=(m // bm, n // bn),
        in_specs=[pl.BlockSpec((bm, k), lambda i, j: (i, 0)),
                  pl.BlockSpec((k, bn), lambda i, j: (0, j))] + list(extra_specs),
        out_specs=pl.BlockSpec((bm, bn), lambda i, j: (i, j)),
        out_shape=jax.ShapeDtypeStruct((m, n), F32),
        scratch_shapes=[pltpu.VMEM((bm, k), BF16)],
        compiler_params=_cparams("parallel", "arbitrary"),
        name=name,
    )(x, w, *extras)


def _hgrn_proj_epilogue(acc, j, extras, o_ref, cols, *, bn):
    nq = A_WIDTH // bn
    is_silu = jnp.logical_or(j < nq, j >= 3 * nq)
    o_ref[:, cols] = acc * jnp.where(is_silu, _sigmoid(acc), 1.0)


def _gelu_tanh(z):
    return 0.5 * z * (1.0 + jnp.tanh(0.7978845608028654 * (z + 0.044715 * (z * z * z))))


def _gmlp_proj_epilogue(acc, j, extras, o_ref, cols):
    bias_ref = extras[0]
    o_ref[:, cols] = _gelu_tanh(acc + bias_ref[:, cols])


def _gmlp_proj_finish(j, extras, o_ref):
    _, g_ref, b_ref = extras

    @pl.when(j == 1)
    def _():
        o_ref[...] = _ln_rows(o_ref[...], g_ref[...], b_ref[...])


def _attn_proj_epilogue(acc, j, extras, o_ref, cols, *, bn):
    cos_ref, sin_ref = extras
    is_rot = j < 2 * len(C_GROUPS) * C_HEADS * C_HD // bn
    cos = cos_ref[...]
    sin = sin_ref[...]
    for h in range(acc.shape[1] // C_HD):
        xh = acc[:, h * C_HD:(h + 1) * C_HD]
        rot = xh * cos + pltpu.roll(xh, C_HD // 2, 1) * sin
        o_ref[:, cols.start + h * C_HD:cols.start + (h + 1) * C_HD] = jnp.where(is_rot, rot, xh)


def _out_ln_body(*refs, prologue, n_in):
    ins = refs[:n_in]
    w_ref, r_ref, g_ref, b_ref, o_ref = refs[n_in:n_in + 5]
    a = prologue(*ins)
    acc = jnp.dot(a, w_ref[...], preferred_element_type=F32)
    y = ALPHA * r_ref[...] + acc
    o_ref[...] = _ln_rows(y, g_ref[...], b_ref[...])


def _out_ln(ins, in_specs, prologue, w, resid, g, b, bm, name):
    m = ins[0].shape[0]
    k = w.shape[0]
    row = pl.BlockSpec((bm, D_MODEL), lambda i: (i, 0))
    vec = pl.BlockSpec((1, D_MODEL), lambda i: (0, 0))
    return pl.pallas_call(
        functools.partial(_out_ln_body, prologue=prologue, n_in=len(ins)),
        grid=(m // bm,),
        in_specs=list(in_specs) + [pl.BlockSpec((k, D_MODEL), lambda i: (0, 0)), row, vec, vec],
        out_specs=row,
        out_shape=jax.ShapeDtypeStruct((m, D_MODEL), F32),
        compiler_params=_cparams("parallel"),
        name=name,
    )(*ins, w, resid, g, b)


def _cast_prologue(a_ref):
    return a_ref[...].astype(BF16)


def _merge_prologue(o0, o1, o2, l0, l1, l2):
    a0, a1, a2 = l0[...], l1[...], l2[...]
    mx = jnp.maximum(jnp.maximum(a0, a1), a2)
    e0, e1, e2 = jnp.exp(a0 - mx), jnp.exp(a1 - mx), jnp.exp(a2 - mx)
    inv = 1.0 / (e0 + e1 + e2)
    w0, w1, w2 = e0 * inv, e1 * inv, e2 * inv
    heads = []
    for h in range(C_HEADS):
        cols = slice(h * C_HD, (h + 1) * C_HD)
        heads.append(w0[:, h:h + 1] * o0[:, cols] + w1[:, h:h + 1] * o1[:, cols] + w2[:, h:h + 1] * o2[:, cols])
    return jnp.concatenate(heads, axis=1).astype(BF16)


def _moe_gate(logits, group=None):
    lane = lax.broadcasted_iota(jnp.int32, logits.shape, 1).astype(F32)
    neg = -jnp.inf
    big = 4.0 * LANES
    gl = jnp.where((lane >= MOE_GE) & (lane < MOE_GE + MOE_GROUPS), logits, neg)
    gmax = jnp.max(gl, axis=-1, keepdims=True)
    den = jnp.sum(jnp.exp(gl - gmax), axis=-1, keepdims=True)
    if group is None:
        g_idx = jnp.min(jnp.where(gl == gmax, lane - MOE_GE, big), axis=-1, keepdims=True)
        g_top = 1.0 / den
    else:
        g_idx = group
        g_sel = jnp.sum(jnp.where(lane == MOE_GE + group, logits, 0.0), axis=-1, keepdims=True)
        g_top = jnp.exp(g_sel - gmax) / den
    lo = g_idx * MOE_EXPERTS
    el = jnp.where((lane >= lo) & (lane < lo + MOE_EXPERTS), logits, neg)
    m1 = jnp.max(el, axis=-1, keepdims=True)
    i1 = jnp.min(jnp.where(el == m1, lane, big), axis=-1, keepdims=True)
    el2 = jnp.where(lane == i1, neg, el)
    m2 = jnp.max(el2, axis=-1, keepdims=True)
    i2 = jnp.min(jnp.where(el2 == m2, lane, big), axis=-1, keepdims=True)
    r = jnp.exp(m2 - m1)
    w1 = g_top / (1.0 + r)
    w2 = w1 * r
    return jnp.where(lane == i1, w1, 0.0) + jnp.where(lane == i2, w2, 0.0), g_idx


def _gate_columns(gate, first):
    lane = lax.broadcasted_iota(jnp.int32, gate.shape, 1)
    return jnp.concatenate(
        [jnp.broadcast_to(jnp.sum(jnp.where(lane == first + e, gate, 0.0), axis=-1, keepdims=True),
                          (gate.shape[0], MOE_FF)) for e in range(MOE_EXPERTS)], axis=1)


def _group_ffn(xb, gate, first, w1_ref, w3_ref, w2_ref):
    h1 = jnp.concatenate([jnp.dot(xb, w1_ref[e], preferred_element_type=F32) for e in range(MOE_EXPERTS)], axis=1)
    h3 = jnp.concatenate([jnp.dot(xb, w3_ref[e], preferred_element_type=F32) for e in range(MOE_EXPERTS)], axis=1)
    hg = (h1 * _sigmoid(h1) * h3 * _gate_columns(gate, first)).astype(BF16)
    return jnp.dot(hg, w2_ref[0], preferred_element_type=F32)


def _moe_body(x_ref, wr_ref, w1_ref, w3_ref, w2_ref, g_ref, b_ref, o_ref, xb_ref, gate_ref, acc_ref):
    gi = pl.program_id(1)

    @pl.when(gi == 0)
    def _route():
        x = x_ref[...]
        xb_ref[...] = x.astype(BF16)
        logits = jnp.dot(x, wr_ref[...], precision=HIGHEST, preferred_element_type=F32)
        gate_ref[...] = _moe_gate(logits)[0]
        acc_ref[...] = jnp.zeros_like(acc_ref)

    acc_ref[...] += _group_ffn(xb_ref[...], gate_ref[...], gi * MOE_EXPERTS, w1_ref, w3_ref, w2_ref)

    @pl.when(gi == MOE_GROUPS - 1)
    def _finish():
        y = ALPHA * x_ref[...] + acc_ref[...]
        o_ref[...] = _ln_rows(y, g_ref[...], b_ref[...])


def _moe(x, wr, w1g, w3g, w2g, g, b, bm):
    m = x.shape[0]
    eff = MOE_EXPERTS * MOE_FF
    row = pl.BlockSpec((bm, D_MODEL), lambda i, e: (i, 0))
    vec = pl.BlockSpec((1, D_MODEL), lambda i, e: (0, 0))
    return pl.pallas_call(
        _moe_body,
        grid=(m // bm, MOE_GROUPS),
        in_specs=[row,
                  pl.BlockSpec((D_MODEL, LANES), lambda i, e: (0, 0)),
                  pl.BlockSpec((MOE_EXPERTS, D_MODEL, MOE_FF), lambda i, e: (e, 0, 0)),
                  pl.BlockSpec((MOE_EXPERTS, D_MODEL, MOE_FF), lambda i, e: (e, 0, 0)),
                  pl.BlockSpec((1, eff, D_MODEL), lambda i, e: (e, 0, 0)),
                  vec, vec],
        out_specs=row,
        out_shape=jax.ShapeDtypeStruct((m, D_MODEL), F32),
        scratch_shapes=[pltpu.VMEM((bm, D_MODEL), BF16),
                        pltpu.VMEM((bm, LANES), F32),
                        pltpu.VMEM((bm, D_MODEL), F32)],
        compiler_params=_cparams("parallel", "arbitrary"),
        name="moe",
    )(x, wr, w1g, w3g, w2g, g, b)


def _router_logits(x, xh, wr_ref):
    xl = (x - xh.astype(F32)).astype(BF16)
    a = jnp.dot(xh, wr_ref[...], preferred_element_type=F32)
    return a[:, :LANES] + a[:, LANES:] + jnp.dot(xl, wr_ref[:, :LANES], preferred_element_type=F32)


def _router_body(x_ref, wr_ref, gidx_ref, cnt_ref):
    x = x_ref[...]
    _, g_idx = _moe_gate(_router_logits(x, x.astype(BF16), wr_ref))
    bm = x.shape[0]
    t = lax.broadcasted_iota(jnp.int32, (bm, LANES), 0)
    lane = lax.broadcasted_iota(jnp.int32, (bm, LANES), 1)
    spread = jnp.where((t & (LANES - 1)) == lane, g_idx, 0.0).astype(BF16)
    r_i = lax.broadcasted_iota(jnp.int32, (bm // LANES, bm), 0)
    t_i = lax.broadcasted_iota(jnp.int32, (bm // LANES, bm), 1)
    sel = (jnp.right_shift(t_i, LANES.bit_length() - 1) == r_i).astype(BF16)
    gidx_ref[...] = jnp.dot(sel, spread, preferred_element_type=F32)
    onehot = (lane.astype(F32) == g_idx).astype(F32)
    cnt_ref[...] = jnp.broadcast_to(jnp.sum(onehot, axis=0, keepdims=True), cnt_ref.shape)


def _router(x, wr, m, bm):
    assert bm % (8 * LANES) == 0 and m % bm == 0
    rows = bm // LANES
    small = pl.BlockSpec((rows, LANES), lambda i: (i, 0))
    return pl.pallas_call(
        _router_body,
        grid=(m // bm,),
        in_specs=[pl.BlockSpec((bm, D_MODEL), lambda i: (i, 0)),
                  pl.BlockSpec((D_MODEL, 2 * LANES), lambda i: (0, 0))],
        out_specs=[small, small],
        out_shape=[jax.ShapeDtypeStruct((m // LANES, LANES), F32),
                   jax.ShapeDtypeStruct((m // LANES, LANES), F32)],
        compiler_params=_cparams("parallel"),
        name="moe_router",
    )(x, wr)


def _moe_sorted_body(tg_ref, grp_ref, lay_ref, x_hbm, wr_ref, w1_ref, w3_ref, w2_ref, g_ref, b_ref, o_hbm,
                     xbuf, obuf, src_ref, dst_ref, pos_ref, xsem, osem, *, bm, n):
    i = pl.program_id(0)
    last = pl.num_programs(0) - 1
    slot = i % 2
    shift = bm.bit_length() - 1

    def plan():
        ng = MOE_GROUPS
        total = src_ref.shape[0] - bm

        def fill(t, carry):
            g = grp_ref[t]
            d = pos_ref[g]
            pos_ref[g] = d + 1
            src_ref[d] = t
            dst_ref[bm + d] = t
            return carry

        def pad(s, carry):
            src_ref[s] = 0
            dst_ref[bm + s] = n + ((s >> shift) & 1) * bm + (s & (bm - 1))
            return carry

        def lead(u, carry):
            dst_ref[u] = n + bm + u
            src_ref[total + u] = 0
            return carry

        for g in range(ng):
            pos_ref[g] = lay_ref[g]
        lax.fori_loop(0, n, fill, 0, unroll=8)
        for g in range(ng):
            lax.fori_loop(lay_ref[ng + g], lay_ref[2 * ng + g], pad, 0)
        lax.fori_loop(lay_ref[3 * ng - 1], total, pad, 0)
        lax.fori_loop(0, bm, lead, 0, unroll=8)

    def gather(tile, s):
        for j in range(bm):
            r = src_ref[tile * bm + j]
            pltpu.make_async_copy(x_hbm.at[pl.ds(r, 1), :], xbuf.at[s, pl.ds(j, 1), :], xsem.at[s]).start()

    def wait_gather(s):
        pltpu.make_async_copy(x_hbm.at[pl.ds(0, bm), :], xbuf.at[s], xsem.at[s]).wait()

    def scatter(tile, s):
        for j in range(bm):
            r = dst_ref[(tile + 1) * bm + j]
            pltpu.make_async_copy(obuf.at[s, pl.ds(j, 1), :], o_hbm.at[pl.ds(r, 1), :], osem.at[s]).start()

    def wait_scatter(s):
        pltpu.make_async_copy(obuf.at[s], o_hbm.at[pl.ds(0, bm), :], osem.at[s]).wait()

    @pl.when(i == 0)
    def _first():
        plan()
        gather(0, 0)
        obuf[1] = jnp.zeros(obuf.shape[1:], F32)

    wait_gather(slot)

    @pl.when(i >= 1)
    def _reuse():
        wait_scatter(slot)

    gather(i + 1, 1 - slot)
    scatter(i - 1, 1 - slot)
    x = xbuf[slot]
    xb = x.astype(BF16)
    gate, _ = _moe_gate(_router_logits(x, xb, wr_ref), tg_ref[i].astype(F32))
    ffn = _group_ffn(xb, gate, tg_ref[i] * MOE_EXPERTS, w1_ref, w3_ref, w2_ref)
    obuf[slot] = _ln_rows(ALPHA * x + ffn, g_ref[...], b_ref[...])

    @pl.when(i == last)
    def _drain():
        scatter(i, slot)
        wait_gather(1 - slot)
        wait_scatter(1 - slot)
        wait_scatter(slot)


def _moe_sorted(tile_group, grp, layout, x, wr2, w1g, w3g, w2g, g, b, bm):
    n_tiles = tile_group.shape[0]
    n = grp.shape[0]
    assert bm & (bm - 1) == 0
    eff = MOE_EXPERTS * MOE_FF
    vec = pl.BlockSpec((1, D_MODEL), lambda i, tg, gr, st: (0, 0))
    hbm = pl.BlockSpec(memory_space=pl.ANY)
    return pl.pallas_call(
        functools.partial(_moe_sorted_body, bm=bm, n=n),
        grid_spec=pltpu.PrefetchScalarGridSpec(
            num_scalar_prefetch=3,
            grid=(n_tiles,),
            in_specs=[hbm,
                      pl.BlockSpec((D_MODEL, 2 * LANES), lambda i, tg, gr, st: (0, 0)),
                      pl.BlockSpec((MOE_EXPERTS, D_MODEL, MOE_FF), lambda i, tg, gr, st: (tg[i], 0, 0)),
                      pl.BlockSpec((MOE_EXPERTS, D_MODEL, MOE_FF), lambda i, tg, gr, st: (tg[i], 0, 0)),
                      pl.BlockSpec((1, eff, D_MODEL), lambda i, tg, gr, st: (tg[i], 0, 0)),
                      vec, vec],
            out_specs=hbm,
            scratch_shapes=[pltpu.VMEM((2, bm, D_MODEL), F32),
                            pltpu.VMEM((2, bm, D_MODEL), F32),
                            pltpu.SMEM(((n_tiles + 1) * bm,), jnp.int32),
                            pltpu.SMEM(((n_tiles + 1) * bm,), jnp.int32),
                            pltpu.SMEM((MOE_GROUPS,), jnp.int32),
                            pltpu.SemaphoreType.DMA((2,)),
                            pltpu.SemaphoreType.DMA((2,))]),
        out_shape=jax.ShapeDtypeStruct((n + 2 * bm, D_MODEL), F32),
        compiler_params=_cparams("arbitrary"),
        name="moe_sorted",
    )(tile_group, grp, layout, x, wr2, w1g, w3g, w2g, g, b)


def _moe_dispatch(x, n, wr, w1g, w3g, w2g, g, b, bm):
    wr_hi = wr.astype(BF16)
    wr_lo = (wr - wr_hi.astype(F32)).astype(BF16)
    wr2 = jnp.concatenate([wr_hi, wr_lo], axis=1)
    rt = 8 * LANES
    gidx, cnt = _router(x, wr2, n, rt)
    grp = gidx.astype(jnp.int32).reshape(n)
    counts = jnp.sum(cnt.reshape(n // rt, rt // LANES, LANES)[:, 0, :MOE_GROUPS], axis=0).astype(jnp.int32)
    padded = (counts + bm - 1) // bm * bm
    ends = jnp.cumsum(padded)
    starts = ends - padded
    n_tiles = n // bm + MOE_GROUPS
    tile_start = jnp.arange(n_tiles, dtype=jnp.int32) * bm
    tile_group = jnp.minimum(jnp.sum(tile_start[:, None] >= ends[None, :], axis=1), MOE_GROUPS - 1)
    layout = jnp.concatenate([starts, starts + counts, ends]).astype(jnp.int32)
    return _moe_sorted(tile_group.astype(jnp.int32), grp, layout, x, wr2, w1g, w3g, w2g, g, b, bm)


def _hgrn_gates(f, lb):
    log_sig = jnp.minimum(f, 0.0) - jnp.log1p(jnp.exp(-jnp.abs(f)))
    a = jnp.log1p(-lb) + log_sig
    log_lb = jnp.log(lb)
    log_f = jnp.maximum(log_lb, a) + jnp.log1p(jnp.exp(-jnp.abs(log_lb - a)))
    k = (1.0 - lb) / (1.0 + jnp.exp(f))
    return log_f, k


def _hgrn_finish(o, gate, ng):
    o = o * lax.rsqrt(jnp.mean(o * o, axis=-1, keepdims=True) + RMS_EPS) * ng
    return o * gate


def _rows_of(x, idx, n):
    return jnp.concatenate([jnp.broadcast_to(x[i:i + 1, :], (n, x.shape[1])) for i in idx], axis=0)


def _hgrn_scan_body(q_ref, f_ref, v_ref, g_ref, lb_ref, ng_ref, tri_ref, lm_ref, bsel_ref, dm_ref,
                    o_ref, s_ref, st_ref, *, n_super):
    C, c, N = HGRN_CHUNK, HGRN_SUB, HGRN_SUPER
    neg = -1e30
    tb = pl.program_id(2)

    @pl.when(tb == 0)
    def _init():
        st_ref[...] = jnp.zeros_like(st_ref)

    lb = lb_ref[...]
    ng = ng_ref[...]
    row = lax.broadcasted_iota(jnp.int32, (N, 1), 0)
    srow = lax.broadcasted_iota(jnp.int32, (c, 1), 0)
    levels = [(C >> l, C >> (l + 1)) for l in range(lm_ref.shape[0])]

    def super_chunk(si, carry):
        r0 = pl.multiple_of(si * N, N)
        q = q_ref[pl.ds(r0, N), :]
        v = v_ref[pl.ds(r0, N), :].astype(BF16)
        gate = g_ref[pl.ds(r0, N), :]
        log_f, k = _hgrn_gates(f_ref[pl.ds(r0, N), :], lb)

        hi = log_f.astype(BF16)
        r1 = log_f - hi.astype(F32)
        mid = r1.astype(BF16)
        lo = (r1 - mid.astype(F32)).astype(BF16)
        cs = jnp.dot(tri_ref[...], jnp.concatenate([hi, mid, lo], axis=1), preferred_element_type=F32)
        b2 = (cs[:, :A_DK] + cs[:, A_DK:2 * A_DK] + cs[:, 2 * A_DK:]) * LOG2E

        att_t = None
        for li, (blk, half) in enumerate(levels):
            up = (row & (blk - 1)) >= half
            ref = _rows_of(b2, [blk * m + half - 1 for m in range(N // blk)], blk)
            e = jnp.exp2(jnp.where(up, b2 - ref, ref - b2))
            ql = jnp.where(up, q * e, 0.0).astype(BF16)
            kl = jnp.where(up, 0.0, k * e).astype(BF16)
            term = _nt_dot(kl, ql) * lm_ref[li]
            att_t = term if att_t is None else att_t + term

        ys = []
        for i in range(N // c):
            base = c * i
            kb = k[base:base + c, :]
            bb = b2[base:base + c, :]
            units = []
            for tl in range(c):
                d = jnp.where(srow <= tl, b2[base + tl:base + tl + 1, :] - bb, neg)
                units.append(q[base + tl:base + tl + 1, :] * kb * jnp.exp2(d))
            ys.append(jnp.concatenate(units, axis=1))
        y = jnp.concatenate(ys, axis=0).astype(BF16)
        r = jnp.dot(y, bsel_ref[...], preferred_element_type=F32)
        diag = jnp.concatenate([r * dm_ref[0], r * dm_ref[1]], axis=1)
        o_intra = _tn_dot((att_t + diag).astype(BF16), v)

        bl = _rows_of(b2, [C * m + C - 1 for m in range(N // C)], C)
        qe = (q * jnp.exp2(b2)).astype(BF16)
        kd = (k * jnp.exp2(bl - b2)).astype(BF16)
        st = st_ref[...]
        nc = N // C
        states, lhs = [], []
        zero = jnp.zeros((C, A_DK), BF16)
        for m in range(nc):
            rows = slice(C * m, C * m + C)
            states.append(st.astype(BF16))
            lhs.append(jnp.concatenate([qe[rows, :] if j == m else zero for j in range(nc)], axis=1))
            st = st * jnp.exp2(b2[C * m + C - 1:C * m + C, :]) + _tn_dot(v[rows, :], kd[rows, :])
        st_ref[...] = st
        o = o_intra + _nt_dot(jnp.concatenate(lhs, axis=0), jnp.concatenate(states, axis=1))
        o_ref[pl.ds(r0, N), :] = _hgrn_finish(o, gate, ng).astype(o_ref.dtype)
        return carry

    for si in range(n_super):
        super_chunk(si, 0)

    @pl.when(tb == pl.num_programs(2) - 1)
    def _emit():
        s_ref[0, 0] = st_ref[...].T


def _hgrn_scan(p, lb, ng, batch, seq, tb):
    nt = seq // tb
    nh = A_HEADS
    n, c = HGRN_SUPER, HGRN_SUB
    assert n == 2 * LANES and LANES % c == 0
    s_i = jnp.arange(n)[:, None]
    t_i = jnp.arange(n)[None, :]
    tri = ((s_i >= t_i) & (s_i // HGRN_CHUNK == t_i // HGRN_CHUNK)).astype(BF16)
    blocks = []
    blk = HGRN_CHUNK
    while blk > c:
        blocks.append(blk)
        blk //= 2
    lm = jnp.stack([(s_i // bk == t_i // bk) for bk in blocks]).astype(F32)
    lane = jnp.arange(LANES)[None, :]
    bsel = (jnp.arange(c * A_DK)[:, None] // A_DK == lane % c).astype(BF16)
    dm = jnp.stack([(s_i // c == g * (LANES // c) + lane // c) for g in range(2)]).astype(F32)

    def col(off):
        return pl.BlockSpec((tb, A_DK), lambda b, h, t: (b * nt + t, off * nh + h))

    def const(a):
        return pl.BlockSpec(a.shape, lambda b, h, t: (0,) * a.ndim)

    return pl.pallas_call(
        functools.partial(_hgrn_scan_body, n_super=tb // n),
        grid=(batch, nh, nt),
        in_specs=[col(0), col(1), col(2), col(3),
                  pl.BlockSpec((1, A_DK), lambda b, h, t: (0, h)),
                  pl.BlockSpec((1, A_DV), lambda b, h, t: (0, 0)),
                  const(tri), const(lm), const(bsel), const(dm)],
        out_specs=[pl.BlockSpec((tb, A_DV), lambda b, h, t: (b * nt + t, h)),
                   pl.BlockSpec((1, 1, A_DK, A_DV), lambda b, h, t: (b, h, 0, 0))],
        out_shape=[jax.ShapeDtypeStruct((batch * seq, nh * A_DV), BF16),
                   jax.ShapeDtypeStruct((batch, nh, A_DK, A_DV), F32)],
        scratch_shapes=[pltpu.VMEM((A_DV, A_DK), F32)],
        compiler_params=_cparams("parallel", "parallel", "arbitrary"),
        name="hgrn_scan",
    )(p, p, p, p, lb, ng, tri, lm, bsel, dm)


def _hgrn_step_body(p_ref, s0_ref, lb_ref, ng_ref, o_ref, s_ref, *, n_tok):
    R = p_ref.shape[0]
    row = lax.broadcasted_iota(jnp.int32, (R, 1), 0)
    valid = row < n_tok
    r2 = lax.broadcasted_iota(jnp.int32, (R, R), 0)
    c2 = lax.broadcasted_iota(jnp.int32, (R, R), 1)
    tril = (r2 >= c2).astype(F32)
    ng = ng_ref[...]
    for h in range(A_HEADS):
        sl = slice(h * A_DK, (h + 1) * A_DK)
        q = p_ref[:, sl]
        v = p_ref[:, 2 * A_WIDTH + h * A_DV:2 * A_WIDTH + (h + 1) * A_DV]
        gate = p_ref[:, 3 * A_WIDTH + h * A_DV:3 * A_WIDTH + (h + 1) * A_DV]
        log_f, k = _hgrn_gates(p_ref[:, A_WIDTH + h * A_DK:A_WIDTH + (h + 1) * A_DK], lb_ref[:, sl])
        b = jnp.dot(tril, log_f, precision=HIGHEST, preferred_element_type=F32)
        st = s0_ref[0, h].T
        o = _nt_dot((q * jnp.exp(b)).astype(BF16), st.astype(BF16))
        for s in range(n_tok):
            m = row >= s
            w = jnp.where(m, q * k[s:s + 1, :] * jnp.exp(jnp.where(m, b - b[s:s + 1, :], 0.0)), 0.0)
            o = o + jnp.sum(w, axis=-1, keepdims=True) * v[s:s + 1, :]
        o_ref[:, h * A_DV:(h + 1) * A_DV] = _hgrn_finish(o, gate, ng)
        bl = b[n_tok - 1:n_tok, :]
        kd = jnp.where(valid, k * jnp.exp(jnp.where(valid, bl - b, 0.0)), 0.0)
        st_new = st * jnp.exp(bl) + _tn_dot(v.astype(BF16), kd.astype(BF16))
        s_ref[0, h] = st_new.T


def _hgrn_step(p, s0, lb, ng, batch, n_tok):
    rows = p.shape[0] // batch
    return pl.pallas_call(
        functools.partial(_hgrn_step_body, n_tok=n_tok),
        grid=(batch,),
        in_specs=[pl.BlockSpec((rows, 4 * A_WIDTH), lambda b: (b, 0)),
                  pl.BlockSpec((1, A_HEADS, A_DK, A_DV), lambda b: (b, 0, 0, 0)),
                  pl.BlockSpec((1, A_WIDTH), lambda b: (0, 0)),
                  pl.BlockSpec((1, A_DV), lambda b: (0, 0))],
        out_specs=[pl.BlockSpec((rows, A_HEADS * A_DV), lambda b: (b, 0)),
                   pl.BlockSpec((1, A_HEADS, A_DK, A_DV), lambda b: (b, 0, 0, 0))],
        out_shape=[jax.ShapeDtypeStruct((batch * rows, A_HEADS * A_DV), F32),
                   jax.ShapeDtypeStruct((batch, A_HEADS, A_DK, A_DV), F32)],
        compiler_params=_cparams("parallel"),
        name="hgrn_step",
    )(p, s0, lb, ng)


def _gmlp_prologue(u_ref, v_ref, ws_ref, bs_ref, gated_ref, *, chunk):
    bm = u_ref.shape[0]
    r2 = lax.broadcasted_iota(jnp.int32, (chunk, chunk), 0)
    c2 = lax.broadcasted_iota(jnp.int32, (chunk, chunk), 1)
    causal = r2 >= c2
    for h in range(B_HEADS):
        wc = jnp.where(causal, ws_ref[h], 0.0).astype(BF16)
        bias = bs_ref[:, h:h + 1]
        cols = slice(h * B_HD, (h + 1) * B_HD)
        for n in range(bm // chunk):
            rows = slice(n * chunk, (n + 1) * chunk)
            mixed = jnp.dot(wc, v_ref[rows, cols].astype(BF16), preferred_element_type=F32) + bias
            gated_ref[rows, cols] = (u_ref[rows, cols] * mixed).astype(BF16)
    return gated_ref[...]


def _gmlp_out_body(u_ref, v_ref, ws_ref, bs_ref, w_ref, r_ref, g_ref, b_ref, o_ref, gated_ref, *, chunk):
    a = _gmlp_prologue(u_ref, v_ref, ws_ref, bs_ref, gated_ref, chunk=chunk)
    acc = jnp.dot(a, w_ref[...], preferred_element_type=F32)
    y = ALPHA * r_ref[...] + acc
    o_ref[...] = _ln_rows(y, g_ref[...], b_ref[...])


def _gmlp_out(uv, ws, bs_t, w, resid, g, b, bm, chunk):
    m = uv.shape[0]
    row = pl.BlockSpec((bm, D_MODEL), lambda i: (i, 0))
    vec = pl.BlockSpec((1, D_MODEL), lambda i: (0, 0))
    return pl.pallas_call(
        functools.partial(_gmlp_out_body, chunk=chunk),
        grid=(m // bm,),
        in_specs=[pl.BlockSpec((bm, D_MODEL), lambda i: (i, 0)),
                  pl.BlockSpec((bm, D_MODEL), lambda i: (i, 1)),
                  pl.BlockSpec(ws.shape, lambda i: (0, 0, 0)),
                  pl.BlockSpec(bs_t.shape, lambda i: (0, 0)),
                  pl.BlockSpec((D_MODEL, D_MODEL), lambda i: (0, 0)),
                  row, vec, vec],
        out_specs=row,
        out_shape=jax.ShapeDtypeStruct((m, D_MODEL), F32),
        scratch_shapes=[pltpu.VMEM((bm, D_MODEL), BF16)],
        compiler_params=_cparams("parallel"),
        name="gmlp_out",
    )(uv, uv, ws, bs_t, w, resid, g, b)


def _band_attn_body(q_ref, k_ref, v_ref, o_ref, l_ref, *, dil, seq):
    bq = C_QBLOCK
    span = C_KEYS - 1
    scale = C_HD ** -0.5
    n_blocks = seq // dil // bq
    qi = lax.broadcasted_iota(jnp.int32, (bq, 2 * bq), 0)
    ki = lax.broadcasted_iota(jnp.int32, (bq, 2 * bq), 1)
    h = pl.program_id(1)
    mine = lax.broadcasted_iota(jnp.int32, (bq, LANES), 1) == h

    @pl.when(h == 0)
    def _clear():
        l_ref[...] = jnp.zeros_like(l_ref)

    def rows(first, n):
        return pl.ds(first, n) if dil == 1 else pl.ds(first, n, stride=dil)

    for r in range(dil):
        for i in range(n_blocks):
            w = max(i - 1, 0)
            qs = rows(r + dil * bq * i, bq)
            ws = rows(r + dil * bq * w, 2 * bq)
            q = q_ref[qs, :].astype(BF16)
            kw = k_ref[ws, :].astype(BF16)
            vw = v_ref[ws, :].astype(BF16)
            s = _nt_dot(q, kw) * scale
            rel = bq * (i - w) + qi - ki
            s = jnp.where((rel >= 0) & (rel <= span), s, -jnp.inf)
            mx = jnp.max(s, axis=-1, keepdims=True)
            p = jnp.exp(s - mx)
            den = jnp.sum(p, axis=-1, keepdims=True)
            o_ref[qs, :] = jnp.dot(p.astype(BF16), vw, preferred_element_type=F32) / den
            l_ref[qs, :] = jnp.where(mine, mx + jnp.log(den), l_ref[qs, :])


def _band_attn(qkv, g, dil, batch, seq):
    ng = len(C_GROUPS)

    def col(part):
        return pl.BlockSpec((seq, C_HD), lambda b, h: (b, (part * ng + g) * C_HEADS + h))

    return pl.pallas_call(
        functools.partial(_band_attn_body, dil=dil, seq=seq),
        grid=(batch, C_HEADS),
        in_specs=[col(0), col(1), col(2)],
        out_specs=[pl.BlockSpec((seq, C_HD), lambda b, h: (b, h)),
                   pl.BlockSpec((seq, LANES), lambda b, h: (b, 0))],
        out_shape=[jax.ShapeDtypeStruct((batch * seq, C_HEADS * C_HD), F32),
                   jax.ShapeDtypeStruct((batch * seq, LANES), F32)],
        compiler_params=_cparams("parallel", "arbitrary"),
        name=f"band_attn_d{dil}",
    )(qkv, qkv, qkv)


def _step_attn_body(qkv_ref, *refs, g, dil, n_tok):
    past_refs, (o_ref, l_ref) = refs[:-2], refs[-2:]
    scale = C_HD ** -0.5
    ng = len(C_GROUPS)
    hs = C_HEADS
    tok = lax.broadcasted_iota(jnp.int32, (n_tok, 1, 1), 0)
    prow = lax.broadcasted_iota(jnp.int32, (past_refs[0].shape[1], 1, 1), 0)
    kn = qkv_ref[0, :, (ng + g) * hs:(ng + g + 1) * hs, :]
    vn = qkv_ref[0, :, (2 * ng + g) * hs:(2 * ng + g + 1) * hs, :]
    for t in range(n_tok):
        past_ref = past_refs[t % dil]
        new_ok = (tok <= t) & (((t - tok) % dil) == 0)
        past_ok = prow >= (t if dil == 1 else 0)
        q = qkv_ref[0, t, g * hs:(g + 1) * hs, :][None]
        kp = past_ref[0, :, 0:hs, :]
        vp = past_ref[0, :, hs:2 * hs, :]
        sp = jnp.where(past_ok, jnp.sum(kp * q, axis=-1, keepdims=True) * scale, -jnp.inf)
        sn = jnp.where(new_ok, jnp.sum(kn * q, axis=-1, keepdims=True) * scale, -jnp.inf)
        mx = jnp.maximum(jnp.max(sp, axis=0, keepdims=True), jnp.max(sn, axis=0, keepdims=True))
        pp = jnp.exp(sp - mx)
        pn = jnp.exp(sn - mx)
        den = jnp.sum(pp, axis=0, keepdims=True) + jnp.sum(pn, axis=0, keepdims=True)
        o = (jnp.sum(pp * vp, axis=0, keepdims=True) + jnp.sum(pn * vn, axis=0, keepdims=True)) / den
        o_ref[0, t] = o[0]
        l_ref[0, t] = jnp.broadcast_to(mx + jnp.log(den), (1, hs, C_HD))[0]


def _step_attn(qkv, cache, g, window, dil, batch, n_tok):
    past = cache.reshape(batch, window // dil, dil, 2 * C_HEADS, C_HD)
    n_res = min(dil, n_tok)
    out = pl.BlockSpec((1, n_tok, C_HEADS, C_HD), lambda b: (b, 0, 0, 0))
    past_specs = [pl.BlockSpec((1, window // dil, None, 2 * C_HEADS, C_HD),
                               functools.partial(lambda b, r: (b, 0, r, 0, 0), r=r)) for r in range(n_res)]
    return pl.pallas_call(
        functools.partial(_step_attn_body, g=g, dil=dil, n_tok=n_tok),
        grid=(batch,),
        in_specs=[pl.BlockSpec((1, n_tok) + qkv.shape[2:], lambda b: (b, 0, 0, 0))] + past_specs,
        out_specs=[out, out],
        out_shape=[jax.ShapeDtypeStruct((batch, n_tok, C_HEADS, C_HD), F32)] * 2,
        compiler_params=_cparams("parallel"),
        name=f"step_attn_d{dil}",
    )(qkv, *([past] * n_res))


def _rope_tables(pos):
    half = C_HD // 2
    inv = ROPE_THETA ** (-jnp.arange(half, dtype=F32) / half)
    ang = pos.astype(F32)[:, None] * inv[None, :]
    cos, sin = jnp.cos(ang), jnp.sin(ang)
    return jnp.concatenate([cos, cos], -1), jnp.concatenate([-sin, sin], -1)


def _row_tile(m, cap):
    return min(m, cap)


def _hgrn_layer(x, batch, seq, s0, w_in, lb, ng, w_out, ln_g, ln_b):
    m = batch * seq
    bn = 1024
    p = _proj(x, w_in, (), (), functools.partial(_hgrn_proj_epilogue, bn=bn), m, _row_tile(m, PROJ_ROWS), bn,
              PROJ_CHUNK, "hgrn_proj")
    if s0 is None:
        o, s_new = _hgrn_scan(p, lb, ng, batch, seq, HGRN_ROWS if seq % HGRN_ROWS == 0 else HGRN_SUPER)
    else:
        rows = 8
        pp = jnp.pad(p.reshape(batch, seq, -1), ((0, 0), (0, rows - seq), (0, 0))).reshape(batch * rows, -1)
        o, s_new = _hgrn_step(pp, s0, lb, ng, batch, seq)
        o = o.reshape(batch, rows, -1)[:, :seq].reshape(m, -1)
    bm = _row_tile(m, 512)
    x = _out_ln((o,), (pl.BlockSpec((bm, o.shape[1]), lambda i: (i, 0)),), _cast_prologue,
                w_out, x, ln_g, ln_b, bm, "hgrn_out")
    return x, s_new


def _gmlp_layer(x, batch, seq, w_in, b_in, g1, b1, ws, bs, w_out, ln_g, ln_b):
    m = batch * seq
    bm = _row_tile(m, 512)
    vec = pl.BlockSpec((1, D_MODEL), lambda i, j: (0, 0))
    uv = _proj(x, w_in, (b_in, g1, b1),
               (pl.BlockSpec((1, D_MODEL), lambda i, j: (0, j)), vec, vec),
               _gmlp_proj_epilogue, m, bm, D_MODEL, PROJ_CHUNK, "gmlp_proj", finish=_gmlp_proj_finish)
    if seq % B_CHUNK == 0:
        chunk, ws_c, bs_t = B_CHUNK, ws, bs.T
    else:
        chunk = m
        eye = jnp.eye(batch, dtype=ws.dtype)
        ws_c = jnp.einsum("ab,hts->hatbs", eye, ws[:, :seq, :seq]).reshape(B_HEADS, m, m)
        bs_t = jnp.tile(bs[:, :seq].T, (batch, 1))
    x = _gmlp_out(uv, ws_c, bs_t, w_out, x, ln_g, ln_b, _row_tile(m, 256), chunk)
    return x, uv


def _attn_layer(x, batch, seq, caches, pos0, w_in, w_out, ln_g, ln_b):
    m = batch * seq
    bn = 1024
    bm = _row_tile(m, PROJ_ROWS)
    cos, sin = _rope_tables(pos0 + jnp.arange(seq, dtype=jnp.int32))
    cos, sin = jnp.tile(cos, (batch, 1)), jnp.tile(sin, (batch, 1))
    tab = pl.BlockSpec((bm, C_HD), lambda i, j: (i, 0))
    qkv = _proj(x, w_in, (cos, sin), (tab, tab), functools.partial(_attn_proj_epilogue, bn=bn), m, bm, bn,
                PROJ_CHUNK, "attn_proj")
    qkv3 = qkv.reshape(batch, seq, C_QKV)
    outs, lses = [], []
    for g, (window, dil) in enumerate(C_GROUPS):
        if caches is None:
            o, lse = _band_attn(qkv, g, dil, batch, seq)
        else:
            o, lse = _step_attn(qkv.reshape(batch, seq, C_QKV // C_HD, C_HD), caches[g], g, window, dil,
                                batch, seq)
            o = o.reshape(m, -1)
            lse = jnp.pad(lse[:, :, :, 0].reshape(m, C_HEADS), ((0, 0), (0, LANES - C_HEADS)))
        outs.append(o)
        lses.append(lse)
    bm2 = _row_tile(m, 512)
    spec = pl.BlockSpec((bm2, C_HEADS * C_HD), lambda i: (i, 0))
    lspec = pl.BlockSpec((bm2, LANES), lambda i: (i, 0))
    x = _out_ln(tuple(outs) + tuple(lses), (spec,) * 3 + (lspec,) * 3, _merge_prologue, w_out, x, ln_g, ln_b,
                bm2, "attn_out")
    hw = C_HEADS * C_HD
    ng = len(C_GROUPS)
    kv = []
    for g, (window, _) in enumerate(C_GROUPS):
        tail = qkv3[:, seq - min(window, seq):]
        k = tail[:, :, (ng + g) * hw:(ng + g + 1) * hw].reshape(batch, -1, C_HEADS, C_HD)
        v = tail[:, :, (2 * ng + g) * hw:(2 * ng + g + 1) * hw].reshape(batch, -1, C_HEADS, C_HD)
        kv.append(jnp.stack([k, v], axis=2))
    return x, kv


def kernel(x_prompt, x_sample, state_hgrn, cache_c_kv_w128, cache_c_kv_w512, cache_c_kv_w2048, ln_g, ln_b, a_w_in, a_lb_logits, a_norm_g, a_w_out, b_w_in, b_b_in, b_ln_g, b_ln_b, b_w_s, b_b_s, b_w_out, c_w_in, c_w_out, moe_w_group, moe_w_expert, moe_w1, moe_w3, moe_w2):
    bp, tp, _ = x_prompt.shape
    bs, ts, _ = x_sample.shape
    assert tp % HGRN_SUPER == 0 and tp % B_CHUNK == 0 and tp // C_GROUPS[-1][1] >= 2 * C_QBLOCK
    assert ts <= 8 and ts <= C_GROUPS[1][1] and (bp * tp) % MOE_TILE == 0

    lb_p = jax.nn.softmax(a_lb_logits.astype(F32), axis=0)
    lb_all = jnp.clip(jnp.cumsum(lb_p, axis=0) - lb_p[0:1], 0.0, 1.0 - 1e-6)
    caches = (cache_c_kv_w128, cache_c_kv_w512, cache_c_kv_w2048)

    xp = x_prompt.reshape(bp * tp, D_MODEL)
    xs = x_sample.reshape(bs * ts, D_MODEL)
    hgrn_p, hgrn_s, chunk_v_s = [], [], []
    kv_p = [[] for _ in C_GROUPS]
    kv_s = [[] for _ in C_GROUPS]

    for i in range(DEPTH):
        kind, j = i % 3, i // 3
        g0, b0 = ln_g[i, 0][None], ln_b[i, 0][None]
        if kind == 0:
            w_in, w_out = a_w_in[j].astype(BF16), a_w_out[j].astype(BF16)
            lb, ng = lb_all[j][None], a_norm_g[j][None]
            xp, sp = _hgrn_layer(xp, bp, tp, None, w_in, lb, ng, w_out, g0, b0)
            xs, ss = _hgrn_layer(xs, bs, ts, state_hgrn[j].astype(F32), w_in, lb, ng, w_out, g0, b0)
            hgrn_p.append(sp)
            hgrn_s.append(ss)
        elif kind == 1:
            w_in, w_out = b_w_in[j].astype(BF16), b_w_out[j].astype(BF16)
            args = (w_in, b_b_in[j][None], b_ln_g[j][None], b_ln_b[j][None], b_w_s[j], b_b_s[j], w_out, g0, b0)
            xp, _ = _gmlp_layer(xp, bp, tp, *args)
            xs, uvs = _gmlp_layer(xs, bs, ts, *args)
            chunk_v_s.append(uvs[:, D_MODEL:].reshape(bs, ts, D_MODEL))
        else:
            w_in, w_out = c_w_in[j].astype(BF16), c_w_out[j].astype(BF16)
            xp, kvp = _attn_layer(xp, bp, tp, None, 0, w_in, w_out, g0, b0)
            xs, kvs = _attn_layer(xs, bs, ts, tuple(c[j] for c in caches), PAST_LEN, w_in, w_out, g0, b0)
            for g in range(len(C_GROUPS)):
                kv_p[g].append(kvp[g])
                kv_s[g].append(kvs[g])
        wr = jnp.pad(jnp.concatenate([moe_w_expert[i], moe_w_group[i]], axis=1),
                     ((0, 0), (0, LANES - MOE_GE - MOE_GROUPS)))
        w1g, w3g = moe_w1[i].astype(BF16), moe_w3[i].astype(BF16)
        w2g = moe_w2[i].astype(BF16).reshape(MOE_GROUPS, MOE_EXPERTS * MOE_FF, D_MODEL)
        g1, b1 = ln_g[i, 1][None], ln_b[i, 1][None]
        xp = _moe_dispatch(xp, bp * tp, wr, w1g, w3g, w2g, g1, b1, MOE_TILE)
        xs = _moe(xs, wr, w1g, w3g, w2g, g1, b1, bs * ts)

    return (xp[:bp * tp].reshape(bp, tp, D_MODEL), xs.reshape(bs, ts, D_MODEL),
            jnp.stack(hgrn_p), jnp.stack(hgrn_s), jnp.stack(chunk_v_s),
            jnp.stack(kv_p[0]), jnp.stack(kv_s[0]), jnp.stack(kv_p[1]), jnp.stack(kv_s[1]),
            jnp.stack(kv_p[2]), jnp.stack(kv_s[2]))
```
